```python
import jax, jax.numpy as jnp
from jax import lax
import numpy as np

D_MODEL = 1024
BATCH = 32
SEQ = 256
DEPTH = 2
DEC_BATCH = 8
DEC_SEQ = 4096
PAST_LEN = 512

GRID_W = 64
CONV_W = 256
CONV_K = 3
RWKV_HEADS = 8
RWKV_HEAD = 64
RWKV_W = RWKV_HEADS * RWKV_HEAD
LORA_DECAY = 64
LORA_A = 64
LORA_GATE = 128
N_DIR = 2
GN_EPS = 64e-5
SGU_GROUPS = 4
SGU_W = 256
SGU_GROUP_W = SGU_W // SGU_GROUPS
CHUNK = 128
N_BRANCH = 3
IN_SIZES = (CONV_W, CONV_W, CONV_W,
            RWKV_W, RWKV_W, RWKV_W,
            N_DIR * LORA_DECAY, N_DIR * LORA_A, LORA_GATE,
            SGU_W, SGU_W,
            N_BRANCH * D_MODEL)
IN_OFFSETS = tuple(np.cumsum(IN_SIZES)[:-1].tolist())
IN_COLS = sum(IN_SIZES)
N_EXPERTS = 16
N_GROUPS = 4
GROUP_SIZE = N_EXPERTS // N_GROUPS
TOP_K = 2
D_EXPERT = 256
MOE_BLOCK = 128
NORM_EPS = 1e-6
N_MOD = 6

kernel_name = "hybrid_flow_conv_rwkv7_sgu_moe_step"


def _normal(k, shape, scale):
    return scale * jax.random.normal(k, shape, jnp.float32)


def rms_norm(x, g):
    x32 = x.astype(jnp.float32)
    y = x32 * lax.rsqrt(jnp.mean(x32 * x32, axis=-1, keepdims=True) + NORM_EPS)
    return (y * g.astype(jnp.float32)).astype(x.dtype)


def layer_norm(x, g, b):
    x32 = x.astype(jnp.float32)
    mu = jnp.mean(x32, axis=-1, keepdims=True)
    var = jnp.mean(jnp.square(x32 - mu), axis=-1, keepdims=True)
    y = (x32 - mu) * lax.rsqrt(var + NORM_EPS)
    return (y * g.astype(jnp.float32) + b.astype(jnp.float32)).astype(x.dtype)


def short_conv3(x, w, grid):
    rows, cols = grid
    B, T, C = x.shape
    xp = jnp.pad(x.reshape(B, rows, cols, C), ((0, 0), (0, 0), (1, 1), (0, 0)))
    y = w[0] * xp[:, :, :-2] + w[1] * xp[:, :, 1:-1] + w[2] * xp[:, :, 2:]
    return y.reshape(B, T, C)


def rwkv7_bidirectional(r, k, v, w_lo, a_lo, g_lo, s0, w0, w2, a0, a2, g2, k_k, k_a, r_k, lnx_w, lnx_b):
    f32 = jnp.float32
    B, T, C = r.shape
    H, N = RWKV_HEADS, RWKV_HEAD
    r, k, v = r.astype(f32), k.astype(f32), v.astype(f32)
    w_logit = w0.astype(f32) + jnp.einsum('btdr,drc->btdc', jnp.tanh(w_lo.astype(f32)), w2.astype(f32))
    decay = jnp.exp(-jnp.exp(-jax.nn.softplus(-w_logit) - 0.5))
    a = jax.nn.sigmoid(a0.astype(f32) + jnp.einsum('btdr,drc->btdc', a_lo.astype(f32), a2.astype(f32)))
    g = jax.nn.sigmoid(g_lo.astype(f32)) @ g2.astype(f32)
    kk = (k * k_k.astype(f32)).reshape(B, T, H, N)
    kk = (kk / jnp.maximum(jnp.sqrt(jnp.sum(kk * kk, axis=-1, keepdims=True)), 1e-12)).reshape(B, T, C)
    k_dir = k[:, :, None, :] * (1.0 + (a - 1.0) * k_a.astype(f32))
    kka = kk[:, :, None, :] * a

    def to_scan(t):
        fwd, bwd = (t, t) if t.ndim == 3 else (t[:, :, 0], t[:, :, 1])
        st = jnp.stack([fwd, jnp.flip(bwd, axis=1)], axis=2).reshape(B, T, N_DIR, H, N)
        return jnp.moveaxis(st, 1, 0)

    xs = (to_scan(r), to_scan(decay), to_scan(k_dir), to_scan(v), to_scan(kk), to_scan(kka))

    def step(S, inp):
        r_t, w_t, k_t, v_t, kk_t, kka_t = inp
        sa = jnp.einsum('bdhvk,bdhk->bdhv', S, kk_t)
        S = (S * w_t[..., None, :] - sa[..., :, None] * kka_t[..., None, :]
             + v_t[..., :, None] * k_t[..., None, :])
        return S, jnp.einsum('bdhvk,bdhk->bdhv', S, r_t)

    s_final, ys = lax.scan(step, s0.astype(f32), xs)
    ys = jnp.moveaxis(ys, 0, 1)
    y = ys[:, :, 0] + jnp.flip(ys[:, :, 1], axis=1)
    mu = jnp.mean(y, axis=-1, keepdims=True)
    var = jnp.mean(jnp.square(y - mu), axis=-1, keepdims=True)
    yn = ((y - mu) * lax.rsqrt(var + GN_EPS)).reshape(B, T, C) * lnx_w.astype(f32) + lnx_b.astype(f32)
    rk = (r[:, :, None, :] * k_dir * r_k.astype(f32).reshape(C)).reshape(B, T, N_DIR, H, N).sum(-1).sum(2)
    bonus = (rk[..., None] * v.reshape(B, T, H, N)).reshape(B, T, C)
    return (yn + bonus) * g, s_final


def spatial_gating(u, v, ln_g, ln_b, ws, bs):
    B, T, _ = v.shape
    vn = layer_norm(v, ln_g, ln_b).reshape(B, T // CHUNK, CHUNK, SGU_GROUPS, SGU_GROUP_W)
    mixed = jnp.einsum('gpq,bnqgc->bnpgc', ws, vn) + jnp.swapaxes(bs, 0, 1)[:, :, None]
    return u * mixed.reshape(B, T, SGU_W)


def parallel_mixer(h, s0, grid, lp):
    B, T, _ = h.shape
    z = h @ lp['w_in']
    cb, cc, cx, r, k, v, w_lo, a_lo, g_lo, su, sv, gates = jnp.split(z, IN_OFFSETS, axis=-1)
    y_conv = cb * short_conv3(cc * cx, lp['conv_w'], grid)
    y_rwkv, s_final = rwkv7_bidirectional(
        r, k, v, w_lo.reshape(B, T, N_DIR, LORA_DECAY), a_lo.reshape(B, T, N_DIR, LORA_A), g_lo, s0,
        lp['rwkv_w0'], lp['rwkv_w2'], lp['rwkv_a0'], lp['rwkv_a2'], lp['rwkv_g2'],
        lp['rwkv_k_k'], lp['rwkv_k_a'], lp['rwkv_r_k'], lp['rwkv_lnx_w'], lp['rwkv_lnx_b'])
    y_sgu = spatial_gating(su, sv, lp['sgu_ln_g'], lp['sgu_ln_b'], lp['sgu_ws'], lp['sgu_bs'])
    g_conv, g_rwkv, g_sgu = jnp.split(jax.nn.sigmoid(gates.astype(jnp.float32)).astype(h.dtype), N_BRANCH, axis=-1)
    merged = (g_conv * (y_conv @ lp['proj_conv'])
              + g_rwkv * (y_rwkv.astype(h.dtype) @ lp['proj_rwkv'])
              + g_sgu * (y_sgu @ lp['proj_sgu']))
    return merged @ lp['w_out'], s_final


def routed_moe(h, router_w, router_b, w1, w3, w2):
    B, T, D = h.shape
    hf = h.reshape(B * T, D)
    n_tok = B * T
    scores = jax.nn.sigmoid(hf.astype(jnp.float32) @ router_w.astype(jnp.float32))
    biased = (scores + router_b.astype(jnp.float32)).reshape(n_tok, N_GROUPS, GROUP_SIZE)
    group_score = lax.top_k(biased, TOP_K)[0].sum(-1)
    g_sel = jnp.argmax(group_score, axis=-1).astype(jnp.int32)
    in_group = jnp.take_along_axis(biased, g_sel[:, None, None], axis=1)[:, 0]
    local = lax.top_k(in_group, TOP_K)[1].astype(jnp.int32)
    expert_idx = g_sel[:, None] * GROUP_SIZE + local
    gate = jnp.take_along_axis(scores, expert_idx, axis=1)
    gate = (gate / jnp.sum(gate, axis=-1, keepdims=True)).astype(h.dtype)

    nk = n_tok * TOP_K
    flat_e = expert_idx.reshape(nk)
    order = jnp.argsort(flat_e).astype(jnp.int32)
    sorted_e = flat_e[order]
    counts = jnp.bincount(flat_e, length=N_EXPERTS).astype(jnp.int32)
    padded = (counts + MOE_BLOCK - 1) // MOE_BLOCK * MOE_BLOCK
    pad_end = jnp.cumsum(padded)
    pad_start = pad_end - padded
    start = jnp.cumsum(counts) - counts
    dest = (pad_start[sorted_e] + jnp.arange(nk, dtype=jnp.int32) - start[sorted_e]).astype(jnp.int32)
    n_blocks = -(-nk // MOE_BLOCK) + N_EXPERTS
    n_rows = n_blocks * MOE_BLOCK
    src_token = jnp.full((n_rows,), n_tok, jnp.int32).at[dest].set(order // TOP_K)
    block_expert = jnp.minimum(
        jnp.searchsorted(pad_end, jnp.arange(n_blocks, dtype=jnp.int32) * MOE_BLOCK, side='right'),
        N_EXPERTS - 1)
    x_rows = jnp.concatenate([hf, jnp.zeros((1, D), hf.dtype)], axis=0)[src_token]
    x_rows = x_rows.reshape(n_blocks, MOE_BLOCK, D)

    def block_ffn(args):
        xb, e = args
        return (jax.nn.silu(xb @ w1[e]) * (xb @ w3[e])) @ w2[e]

    y_rows = lax.map(block_ffn, (x_rows, block_expert)).reshape(n_rows, D)
    dest_of_assign = jnp.zeros((nk,), jnp.int32).at[order].set(dest)
    y = jnp.sum(y_rows[dest_of_assign].reshape(n_tok, TOP_K, D) * gate[..., None], axis=1)
    return y.reshape(B, T, D)


def trunk_layer(x, mod, s0, grid, lp, router_w, router_b):
    sh1, sc1, ga1, sh2, sc2, ga2 = jnp.split(mod[:, None, :], N_MOD, axis=-1)
    h = rms_norm(x, lp['norm1_g']) * (1 + sc1) + sh1
    mix, s_final = parallel_mixer(h, s0, grid, lp)
    x = x + ga1 * mix
    h = rms_norm(x, lp['norm2_g']) * (1 + sc2) + sh2
    x = x + ga2 * routed_moe(h, router_w, router_b, lp['exp_w1'], lp['exp_w3'], lp['exp_w2'])
    return x, s_final


def setup_inputs(seed: int = 0) -> dict:
    key = jax.random.key(seed)
    ks = iter(jax.random.split(key, 40))
    L, D, H, N = DEPTH, D_MODEL, RWKV_HEADS, RWKV_HEAD
    return {
        'x_prompt': _normal(next(ks), (BATCH, SEQ, D), 1.0),
        'x_sample': _normal(next(ks), (DEC_BATCH, DEC_SEQ, D), 1.0),
        'state_rwkv': _normal(next(ks), (DEC_BATCH, L, N_DIR, H, N, N), 0.5),
        'c': _normal(next(ks), (DEC_BATCH, D), 1.0),
        'c_ctx': _normal(next(ks), (D,), 1.0),
        'norm1_g': 1.0 + _normal(next(ks), (L, D), 0.02),
        'norm2_g': 1.0 + _normal(next(ks), (L, D), 0.02),
        'final_g': 1.0 + _normal(next(ks), (D,), 0.02),
        'ada_w': _normal(next(ks), (L, D, N_MOD * D), 0.5 * D ** -0.5),
        'ada_b': _normal(next(ks), (L, N_MOD * D), 0.02),
        'w_in': _normal(next(ks), (L, D, IN_COLS), D ** -0.5),
        'conv_w': _normal(next(ks), (L, CONV_K, CONV_W), CONV_K ** -0.5),
        'rwkv_w0': _normal(next(ks), (L, N_DIR, RWKV_W), 0.5),
        'rwkv_w2': _normal(next(ks), (L, N_DIR, LORA_DECAY, RWKV_W), 0.5 * LORA_DECAY ** -0.5),
        'rwkv_a0': _normal(next(ks), (L, N_DIR, RWKV_W), 0.1),
        'rwkv_a2': _normal(next(ks), (L, N_DIR, LORA_A, RWKV_W), 0.5 * LORA_A ** -0.5),
        'rwkv_g2': _normal(next(ks), (L, LORA_GATE, RWKV_W), LORA_GATE ** -0.5),
        'rwkv_k_k': 1.0 + _normal(next(ks), (L, RWKV_W), 0.1),
        'rwkv_k_a': 1.0 + _normal(next(ks), (L, RWKV_W), 0.1),
        'rwkv_r_k': _normal(next(ks), (L, H, N), 0.1),
        'rwkv_lnx_w': 1.0 + _normal(next(ks), (L, RWKV_W), 0.02),
        'rwkv_lnx_b': _normal(next(ks), (L, RWKV_W), 0.02),
        'sgu_ln_g': 1.0 + _normal(next(ks), (L, SGU_W), 0.02),
        'sgu_ln_b': _normal(next(ks), (L, SGU_W), 0.02),
        'sgu_ws': _normal(next(ks), (L, SGU_GROUPS, CHUNK, CHUNK), CHUNK ** -0.5),
        'sgu_bs': 1.0 + _normal(next(ks), (L, SGU_GROUPS, CHUNK), 0.1),
        'proj_conv': _normal(next(ks), (L, CONV_W, D), CONV_W ** -0.5),
        'proj_rwkv': _normal(next(ks), (L, RWKV_W, D), RWKV_W ** -0.5),
        'proj_sgu': _normal(next(ks), (L, SGU_W, D), SGU_W ** -0.5),
        'w_out': _normal(next(ks), (L, D, D), D ** -0.5),
        'router_w': _normal(next(ks), (D, N_EXPERTS), D ** -0.5),
        'router_b': _normal(next(ks), (N_EXPERTS,), 0.01),
        'exp_w1': _normal(next(ks), (L, N_EXPERTS, D, D_EXPERT), D ** -0.5),
        'exp_w3': _normal(next(ks), (L, N_EXPERTS, D, D_EXPERT), D ** -0.5),
        'exp_w2': _normal(next(ks), (L, N_EXPERTS, D_EXPERT, D), D_EXPERT ** -0.5),
    }


def reference(x_prompt, x_sample, state_rwkv, c, c_ctx, norm1_g, norm2_g, final_g, ada_w, ada_b, w_in,
              conv_w, rwkv_w0, rwkv_w2, rwkv_a0, rwkv_a2, rwkv_g2, rwkv_k_k, rwkv_k_a, rwkv_r_k,
              rwkv_lnx_w, rwkv_lnx_b, sgu_ln_g, sgu_ln_b, sgu_ws, sgu_bs, proj_conv, proj_rwkv, proj_sgu,
              w_out, router_w, router_b, exp_w1, exp_w3, exp_w2):
    ctx_grid = (1, x_prompt.shape[1])
    rows = x_sample.shape[1] // GRID_W
    lat_grid = (rows, GRID_W)
    x_ctx, x_lat = x_prompt, x_sample
    s_ctx0 = jnp.zeros((x_prompt.shape[0], N_DIR, RWKV_HEADS, RWKV_HEAD, RWKV_HEAD), jnp.float32)
    new_states = []
    for l in range(DEPTH):
        lp = {
            'norm1_g': norm1_g[l], 'norm2_g': norm2_g[l], 'w_in': w_in[l], 'conv_w': conv_w[l],
            'rwkv_w0': rwkv_w0[l], 'rwkv_w2': rwkv_w2[l], 'rwkv_a0': rwkv_a0[l], 'rwkv_a2': rwkv_a2[l],
            'rwkv_g2': rwkv_g2[l], 'rwkv_k_k': rwkv_k_k[l], 'rwkv_k_a': rwkv_k_a[l], 'rwkv_r_k': rwkv_r_k[l],
            'rwkv_lnx_w': rwkv_lnx_w[l], 'rwkv_lnx_b': rwkv_lnx_b[l],
            'sgu_ln_g': sgu_ln_g[l], 'sgu_ln_b': sgu_ln_b[l], 'sgu_ws': sgu_ws[l], 'sgu_bs': sgu_bs[l],
            'proj_conv': proj_conv[l], 'proj_rwkv': proj_rwkv[l], 'proj_sgu': proj_sgu[l], 'w_out': w_out[l],
            'exp_w1': exp_w1[l], 'exp_w3': exp_w3[l], 'exp_w2': exp_w2[l],
        }
        mod_ctx = (jax.nn.silu(c_ctx) @ ada_w[l] + ada_b[l])[None]
        mod_lat = jax.nn.silu(c) @ ada_w[l] + ada_b[l]
        x_ctx, s_ctx = trunk_layer(x_ctx, mod_ctx, s_ctx0, ctx_grid, lp, router_w, router_b)
        x_lat, _ = trunk_layer(x_lat, mod_lat, state_rwkv[:, l], lat_grid, lp, router_w, router_b)
        new_states.append(s_ctx)
    y_prompt = rms_norm(x_ctx, final_g)
    y_sample = rms_norm(x_lat, final_g)
    new_state_rwkv = jnp.stack(new_states, axis=1).astype(x_prompt.dtype)
    return (y_prompt, y_sample, new_state_rwkv)
```

```python
import functools
import math

import jax
import jax.numpy as jnp
import numpy as np
from jax import lax
from jax.experimental import pallas as pl
from jax.experimental.pallas import tpu as pltpu

F32 = jnp.float32
BF16 = jnp.bfloat16

GRID_W = 64
N_GROUPS = 4
GROUP_SIZE = 4
NORM_EPS = 1e-6
GN_EPS = 64e-5
KK_EPS = 1e-12
DECAY_SCALE = math.exp(-0.5)

TOKEN_TILE = 256
MOE_TILE = 1024
ROUTE_TILE = 2048
SCAN_CHUNK = 64
SCAN_HEADS = 4
MOD_COLS = 1536
VMEM_LIMIT = 56 * 1024 * 1024


def _dot(a, b):
    return jnp.dot(a, b, preferred_element_type=F32)


def _dot_nt(a, b):
    return lax.dot_general(a, b, (((1,), (1,)), ((), ())), preferred_element_type=F32)


def _split(x):
    hi = x.astype(BF16)
    lo = (x - hi.astype(F32)).astype(BF16)
    return hi, lo


def _seg_sum(x, seg):
    hi, lo = _split(x)
    return _dot(hi, seg) + _dot(lo, seg)


def _sigmoid(x):
    return 1.0 / (1.0 + jnp.exp(-x))


def _norm_mod(x, gain, shift, scale):
    ms = jnp.mean(x * x, axis=-1, keepdims=True)
    return (x * lax.rsqrt(ms + NORM_EPS) * gain) * (1.0 + scale) + shift


def _const_spec(shape):
    nd = len(shape)
    return pl.BlockSpec(shape, lambda *_: (0,) * nd)


def _mod_kernel(c_ref, w_ref, b_ref, o_ref):
    c = c_ref[...]
    s = c * _sigmoid(c)
    o_ref[0] = _dot(s.astype(BF16), w_ref[0].astype(BF16)) + b_ref[0]


def _modulation(cvec, ada_w, ada_b):
    n_layer, d, cols = ada_w.shape
    rows = cvec.shape[0]
    return pl.pallas_call(
        _mod_kernel,
        grid=(n_layer, cols // MOD_COLS),
        in_specs=[
            pl.BlockSpec((rows, d), lambda l, j: (0, 0)),
            pl.BlockSpec((1, d, MOD_COLS), lambda l, j: (l, 0, j)),
            pl.BlockSpec((1, 1, MOD_COLS), lambda l, j: (l, 0, j)),
        ],
        out_specs=pl.BlockSpec((1, rows, MOD_COLS), lambda l, j: (l, 0, j)),
        out_shape=jax.ShapeDtypeStruct((n_layer, rows, cols), F32),
        compiler_params=pltpu.CompilerParams(
            dimension_semantics=("arbitrary", "arbitrary"), vmem_limit_bytes=VMEM_LIMIT),
        name="mod",
    )(cvec, ada_w, ada_b.reshape(n_layer, 1, cols))


def _prep_kernel(rowid_ref, x_ref, mod_ref, g1_ref, win_ref, w0_ref, w2_ref, a0_ref, a2_ref, g2_ref,
                 kk_ref, ka_ref, rk_ref, seg_ref,
                 r_o, k_o, v_o, kk_o, lw_o, aa_o, bonus_o, g_o):
    row = rowid_ref[pl.program_id(0)]
    x = x_ref[...]
    d = x.shape[1]
    c = r_o.shape[1]
    shift = mod_ref[pl.ds(row, 1), 0:d]
    scale = mod_ref[pl.ds(row, 1), d:2 * d]
    h = _norm_mod(x, g1_ref[...], shift, scale)
    z = _dot(h.astype(BF16), win_ref[...])
    r = z[:, 0:c]
    k = z[:, c:2 * c]
    v = z[:, 2 * c:3 * c]
    lo = 3 * c
    w_lo = z[:, lo:lo + 128]
    a_lo = z[:, lo + 128:lo + 256]
    g_lo = z[:, lo + 256:lo + 384]
    w_logit = w0_ref[...] + _dot(jnp.tanh(w_lo).astype(BF16), w2_ref[...])
    lw = -DECAY_SCALE * _sigmoid(w_logit)
    aa = _sigmoid(a0_ref[...] + _dot(a_lo.astype(BF16), a2_ref[...]))
    g = _dot(_sigmoid(g_lo).astype(BF16), g2_ref[...])
    seg = seg_ref[...]
    kk0 = k * kk_ref[...]
    kk = kk0 / jnp.maximum(jnp.sqrt(_seg_sum(kk0 * kk0, seg)), KK_EPS)
    ka = ka_ref[...]
    k_dirs = k * (1.0 + (aa[:, 0:c] - 1.0) * ka) + k * (1.0 + (aa[:, c:2 * c] - 1.0) * ka)
    rk = _seg_sum(r * k_dirs * rk_ref[...], seg)
    r_o[...] = r
    k_o[...] = k
    v_o[...] = v
    kk_o[...] = kk
    lw_o[0] = lw[:, 0:c]
    lw_o[1] = lw[:, c:2 * c]
    aa_o[0] = aa[:, 0:c]
    aa_o[1] = aa[:, c:2 * c]
    bonus_o[...] = rk * v
    g_o[...] = g


def _prep(rowid, x, mod, g1, win_r, w0, w2, a0, a2, g2, k_k, k_a, r_k, seg):
    n, d = x.shape
    c = seg.shape[0]
    tm = TOKEN_TILE
    tok = lambda w: pl.BlockSpec((tm, w), lambda i, *_: (i, 0))
    tok2 = pl.BlockSpec((2, tm, c), lambda i, *_: (0, i, 0))
    consts = [mod, g1, win_r, w0, w2, a0, a2, g2, k_k, k_a, r_k, seg]
    grid_spec = pltpu.PrefetchScalarGridSpec(
        num_scalar_prefetch=1,
        grid=(n // tm,),
        in_specs=[tok(d)] + [_const_spec(a.shape) for a in consts],
        out_specs=[tok(c), tok(c), tok(c), tok(c), tok2, tok2, tok(c), tok(c)],
    )
    sds = jax.ShapeDtypeStruct
    return pl.pallas_call(
        _prep_kernel,
        grid_spec=grid_spec,
        out_shape=[sds((n, c), F32)] * 4 + [sds((2, n, c), F32)] * 2 + [sds((n, c), F32)] * 2,
        compiler_params=pltpu.CompilerParams(
            dimension_semantics=("arbitrary",), vmem_limit_bytes=VMEM_LIMIT),
        name="prep",
    )(rowid, x, *consts)


def _branch_kernel(rowid_ref, cols_ref, x_ref, mod_ref, g1_ref, win_ref, cw_ref, lng_ref, lnb_ref,
                   ws_ref, bs_ref, pc_ref, ps_ref, part_o, grw_o):
    i = pl.program_id(0)
    row = rowid_ref[i]
    cols = cols_ref[i]
    x = x_ref[...]
    tm, d = x.shape
    cw_w = cw_ref.shape[1]
    sg_w = lng_ref.shape[1]
    chunk = ws_ref.shape[1]
    n_grp = ws_ref.shape[0]
    shift = mod_ref[pl.ds(row, 1), 0:d]
    scale = mod_ref[pl.ds(row, 1), d:2 * d]
    h = _norm_mod(x, g1_ref[...], shift, scale)
    z = _dot(h.astype(BF16), win_ref[...])
    cb = z[:, 0:cw_w]
    cc = z[:, cw_w:2 * cw_w]
    cx = z[:, 2 * cw_w:3 * cw_w]
    o = 3 * cw_w
    su = z[:, o:o + sg_w]
    sv = z[:, o + sg_w:o + 2 * sg_w]
    gates = _sigmoid(z[:, o + 2 * sg_w:])
    xc = cc * cx
    col = lax.broadcasted_iota(jnp.int32, xc.shape, 0) & (cols - 1)
    prev = jnp.where(col == 0, 0.0, pltpu.roll(xc, 1, 0))
    nxt = jnp.where(col == cols - 1, 0.0, pltpu.roll(xc, tm - 1, 0))
    cw = cw_ref[...]
    y_conv = cb * (cw[0:1] * prev + cw[1:2] * xc + cw[2:3] * nxt)
    mu = jnp.mean(sv, axis=-1, keepdims=True)
    dv = sv - mu
    var = jnp.mean(dv * dv, axis=-1, keepdims=True)
    vn = (dv * lax.rsqrt(var + NORM_EPS) * lng_ref[...] + lnb_ref[...]).astype(BF16)
    lane_grp = lax.broadcasted_iota(jnp.int32, (chunk, sg_w), 1) // (sg_w // n_grp)
    parts = []
    for ci in range(tm // chunk):
        vc = vn[ci * chunk:(ci + 1) * chunk]
        mixed = bs_ref[...]
        for gi in range(n_grp):
            mixed = mixed + jnp.where(lane_grp == gi, _dot(ws_ref[gi], vc), 0.0)
        parts.append(su[ci * chunk:(ci + 1) * chunk] * mixed)
    y_sgu = jnp.concatenate(parts, axis=0)
    part_o[...] = (gates[:, 0:d] * _dot(y_conv.astype(BF16), pc_ref[...])
                   + gates[:, 2 * d:3 * d] * _dot(y_sgu.astype(BF16), ps_ref[...]))
    grw_o[...] = gates[:, d:2 * d]


def _branch(rowid, cols, x, mod, g1, win_c, conv_w, ln_g, ln_b, ws, bs_full, proj_conv, proj_sgu):
    n, d = x.shape
    tm = TOKEN_TILE
    tok = pl.BlockSpec((tm, d), lambda i, *_: (i, 0))
    consts = [mod, g1, win_c, conv_w, ln_g, ln_b, ws, bs_full, proj_conv, proj_sgu]
    grid_spec = pltpu.PrefetchScalarGridSpec(
        num_scalar_prefetch=2,
        grid=(n // tm,),
        in_specs=[tok] + [_const_spec(a.shape) for a in consts],
        out_specs=[tok, tok],
    )
    return pl.pallas_call(
        _branch_kernel,
        grid_spec=grid_spec,
        out_shape=[jax.ShapeDtypeStruct((n, d), F32)] * 2,
        compiler_params=pltpu.CompilerParams(
            dimension_semantics=("arbitrary",), vmem_limit_bytes=VMEM_LIMIT),
        name="branch",
    )(rowid, cols, x, *consts)


def _scan_stream(r, k, v, kk, lw, aa, ka, z, reverse, consts):
    tri, strict, incl, levels, eye, bd_f32, bd_bf16, end = consts
    n_l = r.shape[0]

    def bd(x):
        xb = x.astype(BF16)
        return jnp.concatenate([xb] * (bd_bf16.shape[0] // n_l), axis=0) * bd_bf16

    def mm(x, y):
        return _dot(x.astype(BF16), bd(y))

    b_vec = kk * aa
    kd = k * (1.0 + (aa - 1.0) * ka)
    l_hi = lw.astype(BF16)
    l_mid = (lw - l_hi.astype(F32)).astype(BF16)
    l_lo = (lw - l_hi.astype(F32) - l_mid.astype(F32)).astype(BF16)
    c = _dot(tri, l_hi) + _dot(tri, l_mid) + _dot(tri, l_lo)
    c_end = c[end:end + 1]
    e_neg = jnp.exp(-c)
    e_end = jnp.exp(c_end - c)
    r_t = r * jnp.exp(c)
    a_t = -kk * jnp.exp(c - lw)
    lhs = jnp.concatenate([a_t, r_t], axis=0).astype(BF16)
    m_b = _dot_nt(lhs, bd(b_vec * e_neg))
    m_k = _dot_nt(lhs, bd(kd * e_neg))
    m_s = _dot_nt(lhs, z.astype(BF16))
    a_ab = jnp.where(strict, m_b[0:n_l], 0.0)
    a_ak = jnp.where(strict, m_k[0:n_l], 0.0)
    a_rb = jnp.where(incl, m_b[n_l:], 0.0)
    a_rk = jnp.where(incl, m_k[n_l:], 0.0)
    t_inv = eye + jnp.where(levels[0], a_ab, 0.0)
    for lvl in levels[1:]:
        t_inv = t_inv + mm(mm(t_inv, jnp.where(lvl, a_ab, 0.0)), t_inv)
    bd_v = bd(v)
    u = mm(t_inv, m_s[0:n_l] + _dot(a_ak.astype(BF16), bd_v))
    y = (m_s[n_l:] + _dot(a_rb.astype(BF16), bd(u)) + _dot(a_rk.astype(BF16), bd_v))
    uv_t = jnp.concatenate([u, v], axis=0).T.astype(BF16)
    bk = jnp.concatenate([b_vec * e_end, kd * e_end], axis=0).astype(BF16)
    z_new = z * jnp.exp(c_end) + _dot(uv_t, bk) * bd_f32
    return y, z_new


def _scan_kernel(fb_ref, bb_ref, seq_ref, first_ref, last_ref,
                 rf_ref, kf_ref, vf_ref, kkf_ref, lwf_ref, aaf_ref,
                 rb_ref, kb_ref, vb_ref, kkb_ref, lwb_ref, aab_ref,
                 ka_ref, s0_ref, yf_o, yb_o, sfin_o, z_scr):
    i = pl.program_id(0)
    n_l = rf_ref.shape[0]
    lanes = z_scr.shape[1]
    n_str = rf_ref.shape[1] // lanes
    hs = lanes // SCAN_HEADS

    row = lax.broadcasted_iota(jnp.int32, (lanes, lanes), 0)
    colm = lax.broadcasted_iota(jnp.int32, (lanes, lanes), 1)
    bd_f32 = ((row // hs) == (colm // hs)).astype(F32)
    bd_bf16 = bd_f32.astype(BF16)

    @pl.when(first_ref[i] == 1)
    def _():
        for di in range(2):
            for hg in range(n_str):
                x = s0_ref[0, di, hg]
                z_scr[di * n_str + hg] = jnp.concatenate([x] * SCAN_HEADS, axis=0) * bd_f32

    t = lax.broadcasted_iota(jnp.int32, (n_l, lanes), 0)
    s = lax.broadcasted_iota(jnp.int32, (n_l, lanes), 1) % n_l
    ti = lax.broadcasted_iota(jnp.int32, (n_l, n_l), 0)
    si = lax.broadcasted_iota(jnp.int32, (n_l, n_l), 1)
    eye = (t == s).astype(F32)
    n_lvl = n_l.bit_length() - 1

    def make_consts(reverse):
        if not reverse:
            tri = (ti >= si).astype(BF16)
            strict, incl = t > s, t >= s
            levels = [(((t >> q) & 1) == 1) & ((s >> q) == (t >> q) - 1) for q in range(n_lvl)]
            end = n_l - 1
        else:
            tri = (ti <= si).astype(BF16)
            strict, incl = t < s, t <= s
            levels = [(((t >> q) & 1) == 0) & ((s >> q) == (t >> q) + 1) for q in range(n_lvl)]
            end = 0
        return tri, strict, incl, levels, eye, bd_f32, bd_bf16, end

    ka = ka_ref[...]
    dirs = ((rf_ref, kf_ref, vf_ref, kkf_ref, lwf_ref, aaf_ref, yf_o, False),
            (rb_ref, kb_ref, vb_ref, kkb_ref, lwb_ref, aab_ref, yb_o, True))
    for di, (r_ref, k_ref, v_ref, kk_ref, lw_ref, aa_ref, y_o, reverse) in enumerate(dirs):
        consts = make_consts(reverse)
        for hg in range(n_str):
            ls = slice(hg * lanes, (hg + 1) * lanes)
            y, z_new = _scan_stream(r_ref[:, ls], k_ref[:, ls], v_ref[:, ls], kk_ref[:, ls],
                                    lw_ref[0, :, ls], aa_ref[0, :, ls], ka[:, ls],
                                    z_scr[di * n_str + hg], reverse, consts)
            y_o[:, ls] = y
            z_scr[di * n_str + hg] = z_new

    @pl.when(last_ref[i] == 1)
    def _():
        for di in range(2):
            for hg in range(n_str):
                zz = z_scr[di * n_str + hg]
                acc = zz[0:hs]
                for hh in range(1, SCAN_HEADS):
                    acc = acc + zz[hh * hs:(hh + 1) * hs]
                sfin_o[0, di, hg] = acc


def _scan(tables, r, k, v, kk, lw, aa, k_a, s0):
    n, c = r.shape
    n_seq = s0.shape[0]
    n_l = SCAN_CHUNK
    hs = s0.shape[3]
    lanes = SCAN_HEADS * hs
    n_str = c // lanes
    fwd = pl.BlockSpec((n_l, c), lambda i, fb, bb, sq, fi, la: (fb[i], 0))
    bwd = pl.BlockSpec((n_l, c), lambda i, fb, bb, sq, fi, la: (bb[i], 0))
    fwd2 = pl.BlockSpec((1, n_l, c), lambda i, fb, bb, sq, fi, la: (0, fb[i], 0))
    bwd2 = pl.BlockSpec((1, n_l, c), lambda i, fb, bb, sq, fi, la: (1, bb[i], 0))
    st = pl.BlockSpec((1, 2, n_str, hs, lanes), lambda i, fb, bb, sq, fi, la: (sq[i], 0, 0, 0, 0))
    grid_spec = pltpu.PrefetchScalarGridSpec(
        num_scalar_prefetch=5,
        grid=(tables[0].shape[0],),
        in_specs=[fwd, fwd, fwd, fwd, fwd2, fwd2, bwd, bwd, bwd, bwd, bwd2, bwd2,
                  _const_spec(k_a.shape), st],
        out_specs=[fwd, bwd, st],
        scratch_shapes=[pltpu.VMEM((2 * n_str, lanes, lanes), F32)],
    )
    return pl.pallas_call(
        _scan_kernel,
        grid_spec=grid_spec,
        out_shape=[jax.ShapeDtypeStruct((n, c), F32), jax.ShapeDtypeStruct((n, c), F32),
                   jax.ShapeDtypeStruct((n_seq, 2, n_str, hs, lanes), F32)],
        compiler_params=pltpu.CompilerParams(
            dimension_semantics=("arbitrary",), vmem_limit_bytes=VMEM_LIMIT),
        name="scan",
    )(*tables, r, k, v, kk, lw, aa, r, k, v, kk, lw, aa, k_a, s0)


def _merge_kernel(rowid_ref, x_ref, yf_ref, yb_ref, bonus_ref, g_ref, part_ref, grw_ref, mod_ref,
                  lnw_ref, lnb_ref, seg_ref, prw_ref, wout_ref, g2_ref, rwh_ref, rwl_ref,
                  x1_o, h2_o, lg_o, *, head_size):
    row = rowid_ref[pl.program_id(0)]
    x = x_ref[...]
    d = x.shape[1]
    seg = seg_ref[...]
    y = yf_ref[...] + yb_ref[...]
    hs_inv = 1.0 / head_size
    mu = _seg_sum(y, seg) * hs_inv
    dy = y - mu
    var = _seg_sum(dy * dy, seg) * hs_inv
    yn = dy * lax.rsqrt(var + GN_EPS) * lnw_ref[...] + lnb_ref[...]
    y_rwkv = (yn + bonus_ref[...]) * g_ref[...]
    merged = part_ref[...] + grw_ref[...] * _dot(y_rwkv.astype(BF16), prw_ref[...])
    mix = _dot(merged.astype(BF16), wout_ref[...])
    gate1 = mod_ref[pl.ds(row, 1), 2 * d:3 * d]
    shift2 = mod_ref[pl.ds(row, 1), 3 * d:4 * d]
    scale2 = mod_ref[pl.ds(row, 1), 4 * d:5 * d]
    x1 = x + gate1 * mix
    h2 = _norm_mod(x1, g2_ref[...], shift2, scale2)
    h_hi, h_lo = _split(h2)
    rwh = rwh_ref[...]
    x1_o[...] = x1
    h2_o[...] = h_hi
    lg_o[...] = _dot_nt(rwh, h_hi) + _dot_nt(rwl_ref[...], h_hi) + _dot_nt(rwh, h_lo)


def _merge(rowid, x, yf, yb, bonus, g, part, grw, mod, lnw, lnb, seg, proj_rwkv, w_out, g2n, rw_hi, rw_lo,
           head_size):
    n, d = x.shape
    c = seg.shape[0]
    n_e = rw_hi.shape[0]
    tm = TOKEN_TILE
    tok = lambda w: pl.BlockSpec((tm, w), lambda i, *_: (i, 0))
    consts = [mod, lnw, lnb, seg, proj_rwkv, w_out, g2n, rw_hi, rw_lo]
    grid_spec = pltpu.PrefetchScalarGridSpec(
        num_scalar_prefetch=1,
        grid=(n // tm,),
        in_specs=[tok(d), tok(c), tok(c), tok(c), tok(c), tok(d), tok(d)]
        + [_const_spec(a.shape) for a in consts],
        out_specs=[tok(d), tok(d), pl.BlockSpec((n_e, tm), lambda i, *_: (0, i))],
    )
    sds = jax.ShapeDtypeStruct
    return pl.pallas_call(
        functools.partial(_merge_kernel, head_size=head_size),
        grid_spec=grid_spec,
        out_shape=[sds((n, d), F32), sds((n, d), BF16), sds((n_e, n), F32)],
        compiler_params=pltpu.CompilerParams(
            dimension_semantics=("arbitrary",), vmem_limit_bytes=VMEM_LIMIT),
        name="merge",
    )(rowid, x, yf, yb, bonus, g, part, grw, *consts)


def _route_kernel(lg_ref, b_ref, gt_o):
    scores = _sigmoid(lg_ref[...])
    biased = scores + b_ref[...]
    n_e, n_t = scores.shape
    rows = [biased[e:e + 1] for e in range(n_e)]
    srow = [scores[e:e + 1] for e in range(n_e)]

    def top2_sum(a, b, c, d):
        s1, t1 = jnp.maximum(a, b), jnp.minimum(a, b)
        s2, t2 = jnp.maximum(c, d), jnp.minimum(c, d)
        return jnp.maximum(s1, s2) + jnp.maximum(jnp.minimum(s1, s2), jnp.maximum(t1, t2))

    best = top2_sum(*rows[0:GROUP_SIZE])
    g_sel = jnp.zeros(best.shape, jnp.int32)
    for gi in range(1, N_GROUPS):
        gs = top2_sum(*rows[gi * GROUP_SIZE:(gi + 1) * GROUP_SIZE])
        better = gs > best
        best = jnp.where(better, gs, best)
        g_sel = jnp.where(better, gi, g_sel)

    def pick(src, j):
        out = src[j]
        for gi in range(1, N_GROUPS):
            out = jnp.where(g_sel == gi, src[gi * GROUP_SIZE + j], out)
        return out

    in_b = [pick(rows, j) for j in range(GROUP_SIZE)]
    in_s = [pick(srow, j) for j in range(GROUP_SIZE)]

    def arg_first_max(vals):
        best_v = vals[0]
        best_i = jnp.zeros(best_v.shape, jnp.int32)
        for j in range(1, len(vals)):
            better = vals[j] > best_v
            best_v = jnp.where(better, vals[j], best_v)
            best_i = jnp.where(better, j, best_i)
        return best_i

    i1 = arg_first_max(in_b)
    i2 = arg_first_max([jnp.where(i1 == j, -jnp.inf, in_b[j]) for j in range(GROUP_SIZE)])

    def take(src, idx):
        out = src[0]
        for j in range(1, GROUP_SIZE):
            out = jnp.where(idx == j, src[j], out)
        return out

    g1, g2 = take(in_s, i1), take(in_s, i2)
    den = g1 + g2
    e_idx = lax.broadcasted_iota(jnp.int32, (n_e, n_t), 0)
    e1 = g_sel * GROUP_SIZE + i1
    e2 = g_sel * GROUP_SIZE + i2
    gt_o[...] = jnp.where(e_idx == e1, g1 / den, jnp.where(e_idx == e2, g2 / den, 0.0))


def _route(logits_t, bias_b):
    n_e, n = logits_t.shape
    rt = ROUTE_TILE
    return pl.pallas_call(
        _route_kernel,
        grid=(n // rt,),
        in_specs=[pl.BlockSpec((n_e, rt), lambda i: (0, i)), pl.BlockSpec((n_e, rt), lambda i: (0, 0))],
        out_specs=pl.BlockSpec((n_e, rt), lambda i: (0, i)),
        out_shape=jax.ShapeDtypeStruct((n_e, n), F32),
        compiler_params=pltpu.CompilerParams(dimension_semantics=("arbitrary",)),
        name="route",
    )(logits_t, bias_b)


def _moe_kernel(rowid_ref, h_ref, gate_ref, x1_ref, mod_ref, w1_ref, w3_ref, w2_ref, fg_ref, o_ref, acc_ref,
                *, final_norm):
    i = pl.program_id(0)
    e = pl.program_id(1)

    @pl.when(e == 0)
    def _():
        acc_ref[...] = jnp.zeros_like(acc_ref)

    h = h_ref[...]
    a = _dot(h, w1_ref[0])
    b = _dot(h, w3_ref[0])
    hid = (a * _sigmoid(a)) * b
    gates = gate_ref[...]
    lane = lax.broadcasted_iota(jnp.int32, gates.shape, 1)
    gcol = jnp.sum(jnp.where(lane == e, gates, 0.0), axis=-1, keepdims=True)
    acc_ref[...] += gcol * _dot(hid.astype(BF16), w2_ref[0])

    @pl.when(e == pl.num_programs(1) - 1)
    def _():
        d = acc_ref.shape[1]
        row = rowid_ref[i]
        gate2 = mod_ref[pl.ds(row, 1), 5 * d:6 * d]
        x2 = x1_ref[...] + gate2 * acc_ref[...]
        if final_norm:
            ms = jnp.mean(x2 * x2, axis=-1, keepdims=True)
            x2 = x2 * lax.rsqrt(ms + NORM_EPS) * fg_ref[...]
        o_ref[...] = x2


def _moe(rowid, h2, gates, x1, mod, w1, w3, w2, final_g, final_norm):
    n, d = x1.shape
    n_e, _, d_e = w1.shape
    tm = MOE_TILE
    tok = lambda w: pl.BlockSpec((tm, w), lambda i, e, *_: (i, 0))
    grid_spec = pltpu.PrefetchScalarGridSpec(
        num_scalar_prefetch=1,
        grid=(n // tm, n_e),
        in_specs=[tok(d), tok(n_e), tok(d), _const_spec(mod.shape),
                  pl.BlockSpec((1, d, d_e), lambda i, e, *_: (e, 0, 0)),
                  pl.BlockSpec((1, d, d_e), lambda i, e, *_: (e, 0, 0)),
                  pl.BlockSpec((1, d_e, d), lambda i, e, *_: (e, 0, 0)),
                  _const_spec(final_g.shape)],
        out_specs=tok(d),
        scratch_shapes=[pltpu.VMEM((tm, d), F32)],
    )
    return pl.pallas_call(
        functools.partial(_moe_kernel, final_norm=final_norm),
        grid_spec=grid_spec,
        out_shape=jax.ShapeDtypeStruct((n, d), F32),
        compiler_params=pltpu.CompilerParams(
            dimension_semantics=("arbitrary", "arbitrary"), vmem_limit_bytes=VMEM_LIMIT),
        name="moe",
    )(rowid, h2, gates, x1, mod, w1, w3, w2, final_g)


def _scan_tables(n_ctx, t_ctx, n_lat, t_lat):
    fb, bb, sq, fi, la = [], [], [], [], []
    base = 0
    seq = 0
    for n_seq, t_len in ((n_ctx, t_ctx), (n_lat, t_lat)):
        n_c = t_len // SCAN_CHUNK
        for _ in range(n_seq):
            for j in range(n_c):
                fb.append(base + j)
                bb.append(base + n_c - 1 - j)
                sq.append(seq)
                fi.append(int(j == 0))
                la.append(int(j == n_c - 1))
            base += n_c
            seq += 1
    return tuple(jnp.asarray(np.asarray(a, np.int32)) for a in (fb, bb, sq, fi, la))


def _tile_rows(n_ctx, t_ctx, n_lat, t_lat, tile):
    rows = [0] * (n_ctx * t_ctx // tile)
    cols = [t_ctx] * len(rows)
    for b in range(n_lat):
        rows += [1 + b] * (t_lat // tile)
        cols += [GRID_W] * (t_lat // tile)
    return jnp.asarray(np.asarray(rows, np.int32)), jnp.asarray(np.asarray(cols, np.int32))


def kernel(x_prompt, x_sample, state_rwkv, c, c_ctx, norm1_g, norm2_g, final_g, ada_w, ada_b, w_in,
           conv_w, rwkv_w0, rwkv_w2, rwkv_a0, rwkv_a2, rwkv_g2, rwkv_k_k, rwkv_k_a, rwkv_r_k,
           rwkv_lnx_w, rwkv_lnx_b, sgu_ln_g, sgu_ln_b, sgu_ws, sgu_bs, proj_conv, proj_rwkv, proj_sgu,
           w_out, router_w, router_b, exp_w1, exp_w3, exp_w2):
    n_ctx, t_ctx, d = x_prompt.shape
    n_lat, t_lat, _ = x_sample.shape
    n_layer = w_in.shape[0]
    _, n_head, hs = rwkv_r_k.shape
    c_rw = n_head * hs
    conv_c = conv_w.shape[2]
    sgu_c = sgu_ln_g.shape[1]
    n_grp, chunk, _ = sgu_ws.shape[1:]
    lora_w = rwkv_w2.shape[2]
    lora_a = rwkv_a2.shape[2]
    lora_g = rwkv_g2.shape[1]
    n_exp = router_w.shape[1]
    assert 2 * lora_w == 128 and 2 * lora_a == 128 and lora_g == 128
    assert t_ctx % TOKEN_TILE == 0 and TOKEN_TILE % t_ctx == 0 and TOKEN_TILE % GRID_W == 0
    assert t_lat % MOE_TILE == 0 and (n_ctx * t_ctx) % MOE_TILE == 0 and TOKEN_TILE % chunk == 0
    assert n_exp == N_GROUPS * GROUP_SIZE and c_rw % (SCAN_HEADS * hs) == 0
    assert SCAN_CHUNK == hs and t_ctx % SCAN_CHUNK == 0 and t_lat % SCAN_CHUNK == 0

    n_tok = n_ctx * t_ctx + n_lat * t_lat
    x = jnp.concatenate([x_prompt.reshape(n_ctx * t_ctx, d), x_sample.reshape(n_lat * t_lat, d)], axis=0)

    mod_rows = 8 * ((1 + n_lat + 7) // 8)
    cvec = jnp.zeros((mod_rows, d), F32).at[0].set(c_ctx).at[1:1 + n_lat].set(c)
    mod = _modulation(cvec, ada_w, ada_b)

    rowid, cols = _tile_rows(n_ctx, t_ctx, n_lat, t_lat, TOKEN_TILE)
    rowid_moe, _ = _tile_rows(n_ctx, t_ctx, n_lat, t_lat, MOE_TILE)
    tables = _scan_tables(n_ctx, t_ctx, n_lat, t_lat)

    sizes = (conv_c, conv_c, conv_c, c_rw, c_rw, c_rw, 2 * lora_w, 2 * lora_a, lora_g, sgu_c, sgu_c, 3 * d)
    offs = np.concatenate([[0], np.cumsum(sizes)])
    assert offs[-1] == w_in.shape[2]
    rw_lo, rw_hi = int(offs[3]), int(offs[9])
    seg_np = np.kron(np.eye(n_head), np.ones((hs, hs)))
    seg = jnp.asarray(seg_np, BF16)
    n_str = c_rw // (SCAN_HEADS * hs)

    def side_by_side(s):
        s = s.reshape(s.shape[0], 2, n_str, SCAN_HEADS, hs, hs)
        return jnp.swapaxes(s, 3, 4).reshape(s.shape[0], 2, n_str, hs, SCAN_HEADS * hs)

    def from_side_by_side(s):
        s = s.reshape(s.shape[0], 2, n_str, hs, SCAN_HEADS, hs)
        return jnp.swapaxes(s, 3, 4).reshape(s.shape[0], 2, n_head, hs, hs)

    rw_t = router_w.T
    rw_hi_b = rw_t.astype(BF16)
    rw_lo_b = (rw_t - rw_hi_b.astype(F32)).astype(BF16)
    bias_b = jnp.broadcast_to(router_b.astype(F32)[:, None], (n_exp, ROUTE_TILE))
    final_g2 = final_g.reshape(1, d)
    zeros_state = jnp.zeros((n_ctx, 2, n_head, hs, hs), F32)

    new_states = []
    for l in range(n_layer):
        mod_l = mod[l]
        g1 = norm1_g[l].reshape(1, d)
        win = w_in[l]
        win_r = win[:, rw_lo:rw_hi].astype(BF16)
        win_c = jnp.concatenate([win[:, :rw_lo], win[:, rw_hi:]], axis=1).astype(BF16)
        zero_w = jnp.zeros((lora_w, c_rw), F32)
        w2cat = jnp.concatenate([jnp.concatenate([rwkv_w2[l, 0], zero_w], axis=1),
                                 jnp.concatenate([zero_w, rwkv_w2[l, 1]], axis=1)], axis=0).astype(BF16)
        a2cat = jnp.concatenate([jnp.concatenate([rwkv_a2[l, 0], zero_w], axis=1),
                                 jnp.concatenate([zero_w, rwkv_a2[l, 1]], axis=1)], axis=0).astype(BF16)
        r, k, v, kk, lw, aa, bonus, g = _prep(
            rowid, x, mod_l, g1, win_r,
            rwkv_w0[l].reshape(1, 2 * c_rw), w2cat, rwkv_a0[l].reshape(1, 2 * c_rw), a2cat,
            rwkv_g2[l].astype(BF16), rwkv_k_k[l].reshape(1, c_rw), rwkv_k_a[l].reshape(1, c_rw),
            rwkv_r_k[l].reshape(1, c_rw), seg)
        bs_full = jnp.repeat(sgu_bs[l].T, sgu_c // n_grp, axis=1)
        part, grw = _branch(
            rowid, cols, x, mod_l, g1, win_c, conv_w[l], sgu_ln_g[l].reshape(1, sgu_c),
            sgu_ln_b[l].reshape(1, sgu_c), sgu_ws[l].astype(BF16), bs_full,
            proj_conv[l].astype(BF16), proj_sgu[l].astype(BF16))
        s0 = side_by_side(jnp.concatenate([zeros_state, state_rwkv[:, l].astype(F32)], axis=0))
        yf, yb, s_fin = _scan(tables, r, k, v, kk, lw, aa, rwkv_k_a[l].reshape(1, c_rw), s0)
        new_states.append(from_side_by_side(s_fin[:n_ctx]))
        x1, h2, logits_t = _merge(
            rowid, x, yf, yb, bonus, g, part, grw, mod_l, rwkv_lnx_w[l].reshape(1, c_rw),
            rwkv_lnx_b[l].reshape(1, c_rw), seg, proj_rwkv[l].astype(BF16), w_out[l].astype(BF16),
            norm2_g[l].reshape(1, d), rw_hi_b, rw_lo_b, hs)
        gates = _route(logits_t, bias_b).T
        x = _moe(rowid_moe, h2, gates, x1, mod_l, exp_w1[l].astype(BF16), exp_w3[l].astype(BF16),
                 exp_w2[l].astype(BF16), final_g2, final_norm=(l == n_layer - 1))

    y_prompt = x[:n_ctx * t_ctx].reshape(n_ctx, t_ctx, d)
    y_sample = x[n_ctx * t_ctx:].reshape(n_lat, t_lat, d)
    new_state = jnp.stack(new_states, axis=1).astype(x_prompt.dtype)
    return (y_prompt, y_sample, new_state)
```

```python
import functools
import math

import jax
import jax.numpy as jnp
import numpy as np
from jax import lax
from jax.experimental import pallas as pl
from jax.experimental.pallas import tpu as pltpu

F32 = jnp.float32
BF16 = jnp.bfloat16

GRID_W = 64
N_GROUPS = 4
GROUP_SIZE = 4
NORM_EPS = 1e-6
GN_EPS = 64e-5
KK_EPS = 1e-12
DECAY_SCALE = math.exp(-0.5)

TOKEN_TILE = 256
MOE_TILE = 1024
ROUTE_TILE = 2048
SCAN_CHUNK = 64
SCAN_HEADS = 4
SCAN_PAR = 2
MOD_COLS = 1536
VMEM_LIMIT = 56 * 1024 * 1024


def _dot(a, b):
    return jnp.dot(a, b, preferred_element_type=F32)


def _dot_nt(a, b):
    return lax.dot_general(a, b, (((1,), (1,)), ((), ())), preferred_element_type=F32)


def _split(x):
    hi = x.astype(BF16)
    lo = (x - hi.astype(F32)).astype(BF16)
    return hi, lo


def _seg_sum(x, seg):
    hi, lo = _split(x)
    return _dot(hi, seg) + _dot(lo, seg)


def _sigmoid(x):
    return 1.0 / (1.0 + jnp.exp(-x))


def _norm_mod(x, gain, shift, scale):
    ms = jnp.mean(x * x, axis=-1, keepdims=True)
    return (x * lax.rsqrt(ms + NORM_EPS) * gain) * (1.0 + scale) + shift


def _const_spec(shape):
    nd = len(shape)
    return pl.BlockSpec(shape, lambda *_: (0,) * nd)


def _mod_kernel(c_ref, w_ref, b_ref, o_ref):
    c = c_ref[...]
    s = c * _sigmoid(c)
    o_ref[0] = _dot(s.astype(BF16), w_ref[0].astype(BF16)) + b_ref[0]


def _modulation(cvec, ada_w, ada_b):
    n_layer, d, cols = ada_w.shape
    rows = cvec.shape[0]
    return pl.pallas_call(
        _mod_kernel,
        grid=(n_layer, cols // MOD_COLS),
        in_specs=[
            pl.BlockSpec((rows, d), lambda l, j: (0, 0)),
            pl.BlockSpec((1, d, MOD_COLS), lambda l, j: (l, 0, j)),
            pl.BlockSpec((1, 1, MOD_COLS), lambda l, j: (l, 0, j)),
        ],
        out_specs=pl.BlockSpec((1, rows, MOD_COLS), lambda l, j: (l, 0, j)),
        out_shape=jax.ShapeDtypeStruct((n_layer, rows, cols), F32),
        compiler_params=pltpu.CompilerParams(
            dimension_semantics=("arbitrary", "arbitrary"), vmem_limit_bytes=VMEM_LIMIT),
        name="mod",
    )(cvec, ada_w, ada_b.reshape(n_layer, 1, cols))


def _prep_kernel(x_ref, mod_ref, g1_ref, win_ref, w0_ref, w2_ref, a0_ref, a2_ref, g2_ref,
                 kk_ref, ka_ref, rk_ref, seg_ref,
                 r_o, k_o, v_o, kk_o, lw_o, aa_o, bonus_o, g_o, *, row0, tiles_per_seq):
    row = row0 + pl.program_id(0) // tiles_per_seq
    x = x_ref[...]
    d = x.shape[1]
    c = r_o.shape[1]
    shift = mod_ref[pl.ds(row, 1), 0:d]
    scale = mod_ref[pl.ds(row, 1), d:2 * d]
    h = _norm_mod(x, g1_ref[...], shift, scale)
    z = _dot(h.astype(BF16), win_ref[...])
    r = z[:, 0:c]
    k = z[:, c:2 * c]
    v = z[:, 2 * c:3 * c]
    lo = 3 * c
    w_lo = z[:, lo:lo + 128]
    a_lo = z[:, lo + 128:lo + 256]
    g_lo = z[:, lo + 256:lo + 384]
    w_logit = w0_ref[...] + _dot(jnp.tanh(w_lo).astype(BF16), w2_ref[...])
    lw = -DECAY_SCALE * _sigmoid(w_logit)
    aa = _sigmoid(a0_ref[...] + _dot(a_lo.astype(BF16), a2_ref[...]))
    g = _dot(_sigmoid(g_lo).astype(BF16), g2_ref[...])
    seg = seg_ref[...]
    kk0 = k * kk_ref[...]
    kk = kk0 / jnp.maximum(jnp.sqrt(_seg_sum(kk0 * kk0, seg)), KK_EPS)
    ka = ka_ref[...]
    k_dirs = k * (1.0 + (aa[:, 0:c] - 1.0) * ka) + k * (1.0 + (aa[:, c:2 * c] - 1.0) * ka)
    rk = _seg_sum(r * k_dirs * rk_ref[...], seg)
    r_o[...] = r
    k_o[...] = k
    v_o[...] = v
    kk_o[...] = kk
    lw_o[0] = lw[:, 0:c]
    lw_o[1] = lw[:, c:2 * c]
    aa_o[0] = aa[:, 0:c]
    aa_o[1] = aa[:, c:2 * c]
    bonus_o[...] = rk * v
    g_o[...] = g


def _prep(rows, x, mod, g1, win_r, w0, w2, a0, a2, g2, k_k, k_a, r_k, seg):
    n, d = x.shape
    c = seg.shape[0]
    tm = TOKEN_TILE
    tok = lambda w: pl.BlockSpec((tm, w), lambda i: (i, 0))
    tok2 = pl.BlockSpec((2, tm, c), lambda i: (0, i, 0))
    consts = [mod, g1, win_r, w0, w2, a0, a2, g2, k_k, k_a, r_k, seg]
    sds = jax.ShapeDtypeStruct
    return pl.pallas_call(
        functools.partial(_prep_kernel, row0=rows[0], tiles_per_seq=rows[1] // tm),
        grid=(n // tm,),
        in_specs=[tok(d)] + [_const_spec(a.shape) for a in consts],
        out_specs=[tok(c), tok(c), tok(c), tok(c), tok2, tok2, tok(c), tok(c)],
        out_shape=[sds((n, c), F32)] * 4 + [sds((2, n, c), F32)] * 2 + [sds((n, c), F32)] * 2,
        compiler_params=pltpu.CompilerParams(
            dimension_semantics=("arbitrary",), vmem_limit_bytes=VMEM_LIMIT),
        name="prep",
    )(x, *consts)


def _branch_kernel(x_ref, mod_ref, g1_ref, win_ref, cw_ref, lng_ref, lnb_ref,
                   ws_ref, bs_ref, pc_ref, ps_ref, part_o, grw_o, *, row0, tiles_per_seq, cols):
    row = row0 + pl.program_id(0) // tiles_per_seq
    x = x_ref[...]
    tm, d = x.shape
    cw_w = cw_ref.shape[1]
    sg_w = lng_ref.shape[1]
    chunk = ws_ref.shape[1]
    n_grp = ws_ref.shape[0]
    shift = mod_ref[pl.ds(row, 1), 0:d]
    scale = mod_ref[pl.ds(row, 1), d:2 * d]
    h = _norm_mod(x, g1_ref[...], shift, scale)
    z = _dot(h.astype(BF16), win_ref[...])
    cb = z[:, 0:cw_w]
    cc = z[:, cw_w:2 * cw_w]
    cx = z[:, 2 * cw_w:3 * cw_w]
    o = 3 * cw_w
    su = z[:, o:o + sg_w]
    sv = z[:, o + sg_w:o + 2 * sg_w]
    gates = _sigmoid(z[:, o + 2 * sg_w:])
    xc = cc * cx
    col = lax.broadcasted_iota(jnp.int32, xc.shape, 0) & (cols - 1)
    prev = jnp.where(col == 0, 0.0, pltpu.roll(xc, 1, 0))
    nxt = jnp.where(col == cols - 1, 0.0, pltpu.roll(xc, tm - 1, 0))
    cw = cw_ref[...]
    y_conv = cb * (cw[0:1] * prev + cw[1:2] * xc + cw[2:3] * nxt)
    mu = jnp.mean(sv, axis=-1, keepdims=True)
    dv = sv - mu
    var = jnp.mean(dv * dv, axis=-1, keepdims=True)
    vn = (dv * lax.rsqrt(var + NORM_EPS) * lng_ref[...] + lnb_ref[...]).astype(BF16)
    lane_grp = lax.broadcasted_iota(jnp.int32, (chunk, sg_w), 1) // (sg_w // n_grp)
    parts = []
    for ci in range(tm // chunk):
        vc = vn[ci * chunk:(ci + 1) * chunk]
        mixed = bs_ref[...]
        for gi in range(n_grp):
            mixed = mixed + jnp.where(lane_grp == gi, _dot(ws_ref[gi], vc), 0.0)
        parts.append(su[ci * chunk:(ci + 1) * chunk] * mixed)
    y_sgu = jnp.concatenate(parts, axis=0)
    part_o[...] = (gates[:, 0:d] * _dot(y_conv.astype(BF16), pc_ref[...])
                   + gates[:, 2 * d:3 * d] * _dot(y_sgu.astype(BF16), ps_ref[...]))
    grw_o[...] = gates[:, d:2 * d]


def _branch(rows, cols, x, mod, g1, win_c, conv_w, ln_g, ln_b, ws, bs_full, proj_conv, proj_sgu):
    n, d = x.shape
    tm = TOKEN_TILE
    tok = pl.BlockSpec((tm, d), lambda i: (i, 0))
    consts = [mod, g1, win_c, conv_w, ln_g, ln_b, ws, bs_full, proj_conv, proj_sgu]
    return pl.pallas_call(
        functools.partial(_branch_kernel, row0=rows[0], tiles_per_seq=rows[1] // tm, cols=cols),
        grid=(n // tm,),
        in_specs=[tok] + [_const_spec(a.shape) for a in consts],
        out_specs=[tok, tok],
        out_shape=[jax.ShapeDtypeStruct((n, d), F32)] * 2,
        compiler_params=pltpu.CompilerParams(
            dimension_semantics=("arbitrary",), vmem_limit_bytes=VMEM_LIMIT),
        name="branch",
    )(x, *consts)


def _scan_streams(streams, bd_f32, bd_bf16):
    n_l = streams[0][0].shape[0]
    n_rep = bd_bf16.shape[0] // n_l
    ns = range(len(streams))

    def bd(x):
        return jnp.concatenate([x.astype(BF16)] * n_rep, axis=0) * bd_bf16

    def mm(x, y):
        return _dot(x.astype(BF16), bd(y))

    r, k, v, kk, lw, aa, ka, z, consts = map(list, zip(*streams))
    tri, strict, incl, levels, eye, end = map(list, zip(*consts))
    b_vec = [kk[i] * aa[i] for i in ns]
    kd = [k[i] * (1.0 + (aa[i] - 1.0) * ka[i]) for i in ns]
    l_hi = [lw[i].astype(BF16) for i in ns]
    l_mid = [(lw[i] - l_hi[i].astype(F32)).astype(BF16) for i in ns]
    l_lo = [(lw[i] - l_hi[i].astype(F32) - l_mid[i].astype(F32)).astype(BF16) for i in ns]
    c = [_dot(tri[i], l_hi[i]) + _dot(tri[i], l_mid[i]) + _dot(tri[i], l_lo[i]) for i in ns]
    c_end = [c[i][end[i]:end[i] + 1] for i in ns]
    e_neg = [jnp.exp(-c[i]) for i in ns]
    e_end = [jnp.exp(c_end[i] - c[i]) for i in ns]
    lhs = [jnp.concatenate([-kk[i] * jnp.exp(c[i] - lw[i]), r[i] * jnp.exp(c[i])], axis=0).astype(BF16)
           for i in ns]
    m_b = [_dot_nt(lhs[i], bd(b_vec[i] * e_neg[i])) for i in ns]
    m_k = [_dot_nt(lhs[i], bd(kd[i] * e_neg[i])) for i in ns]
    m_s = [_dot_nt(lhs[i], z[i].astype(BF16)) for i in ns]
    a_ab = [jnp.where(strict[i], m_b[i][0:n_l], 0.0) for i in ns]
    t_inv = [eye[i] + jnp.where(levels[i][0], a_ab[i], 0.0) for i in ns]
    for q in range(1, len(levels[0])):
        x1 = [mm(t_inv[i], jnp.where(levels[i][q], a_ab[i], 0.0)) for i in ns]
        x2 = [mm(x1[i], t_inv[i]) for i in ns]
        t_inv = [t_inv[i] + x2[i] for i in ns]
    bd_v = [bd(v[i]) for i in ns]
    w = [m_s[i][0:n_l] + _dot(jnp.where(strict[i], m_k[i][0:n_l], 0.0).astype(BF16), bd_v[i]) for i in ns]
    u = [mm(t_inv[i], w[i]) for i in ns]
    y = [m_s[i][n_l:] + _dot(jnp.where(incl[i], m_b[i][n_l:], 0.0).astype(BF16), bd(u[i]))
         + _dot(jnp.where(incl[i], m_k[i][n_l:], 0.0).astype(BF16), bd_v[i]) for i in ns]
    uv_t = [jnp.concatenate([u[i], v[i]], axis=0).T.astype(BF16) for i in ns]
    bk = [jnp.concatenate([b_vec[i] * e_end[i], kd[i] * e_end[i]], axis=0).astype(BF16) for i in ns]
    z_new = [z[i] * jnp.exp(c_end[i]) + _dot(uv_t[i], bk[i]) * bd_f32 for i in ns]
    return list(zip(y, z_new))


def _scan_kernel(rf_ref, kf_ref, vf_ref, kkf_ref, lwf_ref, aaf_ref,
                 rb_ref, kb_ref, vb_ref, kkb_ref, lwb_ref, aab_ref,
                 ka_ref, s0_ref, yf_o, yb_o, sfin_o, z_scr):
    j = pl.program_id(1)
    n_par, n_l, c = rf_ref.shape
    lanes = z_scr.shape[1]
    n_str = c // lanes
    hs = lanes // SCAN_HEADS

    row = lax.broadcasted_iota(jnp.int32, (lanes, lanes), 0)
    colm = lax.broadcasted_iota(jnp.int32, (lanes, lanes), 1)
    bd_f32 = ((row // hs) == (colm // hs)).astype(F32)
    bd_bf16 = bd_f32.astype(BF16)
    slots = [(p, di, hg) for p in range(n_par) for di in range(2) for hg in range(n_str)]

    @pl.when(j == 0)
    def _():
        for zi, (p, di, hg) in enumerate(slots):
            x = s0_ref[p, 0, di, hg]
            z_scr[zi] = jnp.concatenate([x] * SCAN_HEADS, axis=0) * bd_f32

    t = lax.broadcasted_iota(jnp.int32, (n_l, lanes), 0)
    s = lax.broadcasted_iota(jnp.int32, (n_l, lanes), 1) % n_l
    ti = lax.broadcasted_iota(jnp.int32, (n_l, n_l), 0)
    si = lax.broadcasted_iota(jnp.int32, (n_l, n_l), 1)
    eye = (t == s).astype(F32)
    n_lvl = n_l.bit_length() - 1

    def make_consts(reverse):
        if not reverse:
            tri = (ti >= si).astype(BF16)
            strict, incl = t > s, t >= s
            levels = [(((t >> q) & 1) == 1) & ((s >> q) == (t >> q) - 1) for q in range(n_lvl)]
            end = n_l - 1
        else:
            tri = (ti <= si).astype(BF16)
            strict, incl = t < s, t <= s
            levels = [(((t >> q) & 1) == 0) & ((s >> q) == (t >> q) + 1) for q in range(n_lvl)]
            end = 0
        return tri, strict, incl, levels, eye, end

    ka = ka_ref[...]
    dirs = ((rf_ref, kf_ref, vf_ref, kkf_ref, lwf_ref, aaf_ref, yf_o, make_consts(False)),
            (rb_ref, kb_ref, vb_ref, kkb_ref, lwb_ref, aab_ref, yb_o, make_consts(True)))
    streams = []
    for p, di, hg in slots:
        r_ref, k_ref, v_ref, kk_ref, lw_ref, aa_ref, _, consts = dirs[di]
        ls = slice(hg * lanes, (hg + 1) * lanes)
        streams.append((r_ref[p, :, ls], k_ref[p, :, ls], v_ref[p, :, ls], kk_ref[p, :, ls],
                        lw_ref[0, p, :, ls], aa_ref[0, p, :, ls], ka[:, ls], z_scr[len(streams)], consts))
    for zi, ((y, z_new), (p, di, hg)) in enumerate(zip(_scan_streams(streams, bd_f32, bd_bf16), slots)):
        dirs[di][6][p, :, hg * lanes:(hg + 1) * lanes] = y
        z_scr[zi] = z_new

    @pl.when(j == pl.num_programs(1) - 1)
    def _():
        for zi, (p, di, hg) in enumerate(slots):
            zz = z_scr[zi]
            acc = zz[0:hs]
            for hh in range(1, SCAN_HEADS):
                acc = acc + zz[hh * hs:(hh + 1) * hs]
            sfin_o[p, 0, di, hg] = acc


def _scan(t_len, r, k, v, kk, lw, aa, k_a, s0):
    n, c = r.shape
    n_seq = s0.shape[0]
    n_par = SCAN_PAR
    n_l = SCAN_CHUNK
    n_c = t_len // n_l
    hs = s0.shape[3]
    lanes = SCAN_HEADS * hs
    n_str = c // lanes
    part = lambda a: a.reshape(n_par, n // n_par, c)
    part2 = lambda a: a.reshape(2, n_par, n // n_par, c)
    s0p = s0.reshape((n_par, n_seq // n_par) + s0.shape[1:])
    fwd = pl.BlockSpec((n_par, n_l, c), lambda q, j: (0, q * n_c + j, 0))
    bwd = pl.BlockSpec((n_par, n_l, c), lambda q, j: (0, q * n_c + n_c - 1 - j, 0))
    fwd2 = pl.BlockSpec((1, n_par, n_l, c), lambda q, j: (0, 0, q * n_c + j, 0))
    bwd2 = pl.BlockSpec((1, n_par, n_l, c), lambda q, j: (1, 0, q * n_c + n_c - 1 - j, 0))
    st = pl.BlockSpec((n_par, 1, 2, n_str, hs, lanes), lambda q, j: (0, q, 0, 0, 0, 0))
    sds = jax.ShapeDtypeStruct
    args = [part(r), part(k), part(v), part(kk), part2(lw), part2(aa)]
    yf, yb, s_fin = pl.pallas_call(
        _scan_kernel,
        grid=(n_seq // n_par, n_c),
        in_specs=[fwd, fwd, fwd, fwd, fwd2, fwd2, bwd, bwd, bwd, bwd, bwd2, bwd2,
                  _const_spec(k_a.shape), st],
        out_specs=[fwd, bwd, st],
        out_shape=[sds((n_par, n // n_par, c), F32), sds((n_par, n // n_par, c), F32), sds(s0p.shape, F32)],
        scratch_shapes=[pltpu.VMEM((n_par * 2 * n_str, lanes, lanes), F32)],
        compiler_params=pltpu.CompilerParams(
            dimension_semantics=("arbitrary", "arbitrary"), vmem_limit_bytes=VMEM_LIMIT),
        name="scan",
    )(*args, *args, k_a, s0p)
    return yf.reshape(n, c), yb.reshape(n, c), s_fin.reshape(s0.shape)


def _merge_kernel(x_ref, yf_ref, yb_ref, bonus_ref, g_ref, part_ref, grw_ref, mod_ref,
                  lnw_ref, lnb_ref, seg_ref, prw_ref, wout_ref, g2_ref, rwh_ref, rwl_ref,
                  x1_o, h2_o, lg_o, *, head_size, row0, tiles_per_seq):
    row = row0 + pl.program_id(0) // tiles_per_seq
    x = x_ref[...]
    d = x.shape[1]
    seg = seg_ref[...]
    y = yf_ref[...] + yb_ref[...]
    hs_inv = 1.0 / head_size
    mu = _seg_sum(y, seg) * hs_inv
    dy = y - mu
    var = _seg_sum(dy * dy, seg) * hs_inv
    yn = dy * lax.rsqrt(var + GN_EPS) * lnw_ref[...] + lnb_ref[...]
    y_rwkv = (yn + bonus_ref[...]) * g_ref[...]
    merged = part_ref[...] + grw_ref[...] * _dot(y_rwkv.astype(BF16), prw_ref[...])
    mix = _dot(merged.astype(BF16), wout_ref[...])
    gate1 = mod_ref[pl.ds(row, 1), 2 * d:3 * d]
    shift2 = mod_ref[pl.ds(row, 1), 3 * d:4 * d]
    scale2 = mod_ref[pl.ds(row, 1), 4 * d:5 * d]
    x1 = x + gate1 * mix
    h2 = _norm_mod(x1, g2_ref[...], shift2, scale2)
    h_hi, h_lo = _split(h2)
    rwh = rwh_ref[...]
    x1_o[...] = x1
    h2_o[...] = h_hi
    lg_o[...] = _dot_nt(rwh, h_hi) + _dot_nt(rwl_ref[...], h_hi) + _dot_nt(rwh, h_lo)


def _merge(rows, x, yf, yb, bonus, g, part, grw, mod, lnw, lnb, seg, proj_rwkv, w_out, g2n, rw_hi, rw_lo,
           head_size):
    n, d = x.shape
    c = seg.shape[0]
    n_e = rw_hi.shape[0]
    tm = TOKEN_TILE
    tok = lambda w: pl.BlockSpec((tm, w), lambda i: (i, 0))
    consts = [mod, lnw, lnb, seg, proj_rwkv, w_out, g2n, rw_hi, rw_lo]
    sds = jax.ShapeDtypeStruct
    return pl.pallas_call(
        functools.partial(_merge_kernel, head_size=head_size, row0=rows[0], tiles_per_seq=rows[1] // tm),
        grid=(n // tm,),
        in_specs=[tok(d), tok(c), tok(c), tok(c), tok(c), tok(d), tok(d)]
        + [_const_spec(a.shape) for a in consts],
        out_specs=[tok(d), tok(d), pl.BlockSpec((n_e, tm), lambda i: (0, i))],
        out_shape=[sds((n, d), F32), sds((n, d), BF16), sds((n_e, n), F32)],
        compiler_params=pltpu.CompilerParams(
            dimension_semantics=("arbitrary",), vmem_limit_bytes=VMEM_LIMIT),
        name="merge",
    )(x, yf, yb, bonus, g, part, grw, *consts)


def _route_kernel(lg_ref, b_ref, gt_o):
    scores = _sigmoid(lg_ref[...])
    biased = scores + b_ref[...]
    n_e, n_t = scores.shape
    rows = [biased[e:e + 1] for e in range(n_e)]
    srow = [scores[e:e + 1] for e in range(n_e)]

    def top2_sum(a, b, c, d):
        s1, t1 = jnp.maximum(a, b), jnp.minimum(a, b)
        s2, t2 = jnp.maximum(c, d), jnp.minimum(c, d)
        return jnp.maximum(s1, s2) + jnp.maximum(jnp.minimum(s1, s2), jnp.maximum(t1, t2))

    best = top2_sum(*rows[0:GROUP_SIZE])
    g_sel = jnp.zeros(best.shape, jnp.int32)
    for gi in range(1, N_GROUPS):
        gs = top2_sum(*rows[gi * GROUP_SIZE:(gi + 1) * GROUP_SIZE])
        better = gs > best
        best = jnp.where(better, gs, best)
        g_sel = jnp.where(better, gi, g_sel)

    def pick(src, j):
        out = src[j]
        for gi in range(1, N_GROUPS):
            out = jnp.where(g_sel == gi, src[gi * GROUP_SIZE + j], out)
        return out

    in_b = [pick(rows, j) for j in range(GROUP_SIZE)]
    in_s = [pick(srow, j) for j in range(GROUP_SIZE)]

    def arg_first_max(vals):
        best_v = vals[0]
        best_i = jnp.zeros(best_v.shape, jnp.int32)
        for j in range(1, len(vals)):
            better = vals[j] > best_v
            best_v = jnp.where(better, vals[j], best_v)
            best_i = jnp.where(better, j, best_i)
        return best_i

    i1 = arg_first_max(in_b)
    i2 = arg_first_max([jnp.where(i1 == j, -jnp.inf, in_b[j]) for j in range(GROUP_SIZE)])

    def take(src, idx):
        out = src[0]
        for j in range(1, GROUP_SIZE):
            out = jnp.where(idx == j, src[j], out)
        return out

    g1, g2 = take(in_s, i1), take(in_s, i2)
    den = g1 + g2
    e_idx = lax.broadcasted_iota(jnp.int32, (n_e, n_t), 0)
    e1 = g_sel * GROUP_SIZE + i1
    e2 = g_sel * GROUP_SIZE + i2
    gt_o[...] = jnp.where(e_idx == e1, g1 / den, jnp.where(e_idx == e2, g2 / den, 0.0))


def _route(logits_t, bias_b):
    n_e, n = logits_t.shape
    rt = ROUTE_TILE
    return pl.pallas_call(
        _route_kernel,
        grid=(n // rt,),
        in_specs=[pl.BlockSpec((n_e, rt), lambda i: (0, i)), pl.BlockSpec((n_e, rt), lambda i: (0, 0))],
        out_specs=pl.BlockSpec((n_e, rt), lambda i: (0, i)),
        out_shape=jax.ShapeDtypeStruct((n_e, n), F32),
        compiler_params=pltpu.CompilerParams(dimension_semantics=("arbitrary",)),
        name="route",
    )(logits_t, bias_b)


def _moe_kernel(h_ref, gate_ref, x1_ref, mod_ref, w1_ref, w3_ref, w2_ref, fg_ref, o_ref, acc_ref,
                *, final_norm, row0, tiles_per_seq):
    i = pl.program_id(0)
    e = pl.program_id(1)

    @pl.when(e == 0)
    def _():
        acc_ref[...] = jnp.zeros_like(acc_ref)

    h = h_ref[...]
    a = _dot(h, w1_ref[0])
    b = _dot(h, w3_ref[0])
    hid = (a * _sigmoid(a)) * b
    gates = gate_ref[...]
    lane = lax.broadcasted_iota(jnp.int32, gates.shape, 1)
    gcol = jnp.sum(jnp.where(lane == e, gates, 0.0), axis=-1, keepdims=True)
    acc_ref[...] += gcol * _dot(hid.astype(BF16), w2_ref[0])

    @pl.when(e == pl.num_programs(1) - 1)
    def _():
        d = acc_ref.shape[1]
        row = row0 + i // tiles_per_seq
        gate2 = mod_ref[pl.ds(row, 1), 5 * d:6 * d]
        x2 = x1_ref[...] + gate2 * acc_ref[...]
        if final_norm:
            ms = jnp.mean(x2 * x2, axis=-1, keepdims=True)
            x2 = x2 * lax.rsqrt(ms + NORM_EPS) * fg_ref[...]
        o_ref[...] = x2


def _moe(rows, h2, gates, x1, mod, w1, w3, w2, final_g, final_norm):
    n, d = x1.shape
    n_e, _, d_e = w1.shape
    tm = MOE_TILE
    tok = lambda w: pl.BlockSpec((tm, w), lambda i, e: (i, 0))
    return pl.pallas_call(
        functools.partial(_moe_kernel, final_norm=final_norm, row0=rows[0],
                          tiles_per_seq=max(rows[1] // tm, 1)),
        grid=(n // tm, n_e),
        in_specs=[tok(d), tok(n_e), tok(d), _const_spec(mod.shape),
                  pl.BlockSpec((1, d, d_e), lambda i, e: (e, 0, 0)),
                  pl.BlockSpec((1, d, d_e), lambda i, e: (e, 0, 0)),
                  pl.BlockSpec((1, d_e, d), lambda i, e: (e, 0, 0)),
                  _const_spec(final_g.shape)],
        out_specs=tok(d),
        out_shape=jax.ShapeDtypeStruct((n, d), F32),
        scratch_shapes=[pltpu.VMEM((tm, d), F32)],
        compiler_params=pltpu.CompilerParams(
            dimension_semantics=("arbitrary", "arbitrary"), vmem_limit_bytes=VMEM_LIMIT),
        name="moe",
    )(h2, gates, x1, mod, w1, w3, w2, final_g)


def kernel(x_prompt, x_sample, state_rwkv, c, c_ctx, norm1_g, norm2_g, final_g, ada_w, ada_b, w_in,
           conv_w, rwkv_w0, rwkv_w2, rwkv_a0, rwkv_a2, rwkv_g2, rwkv_k_k, rwkv_k_a, rwkv_r_k,
           rwkv_lnx_w, rwkv_lnx_b, sgu_ln_g, sgu_ln_b, sgu_ws, sgu_bs, proj_conv, proj_rwkv, proj_sgu,
           w_out, router_w, router_b, exp_w1, exp_w3, exp_w2):
    n_ctx, t_ctx, d = x_prompt.shape
    n_lat, t_lat, _ = x_sample.shape
    n_layer = w_in.shape[0]
    _, n_head, hs = rwkv_r_k.shape
    c_rw = n_head * hs
    conv_c = conv_w.shape[2]
    sgu_c = sgu_ln_g.shape[1]
    n_grp, chunk, _ = sgu_ws.shape[1:]
    lora_w = rwkv_w2.shape[2]
    lora_a = rwkv_a2.shape[2]
    lora_g = rwkv_g2.shape[1]
    n_exp = router_w.shape[1]
    assert 2 * lora_w == 128 and 2 * lora_a == 128 and lora_g == 128
    assert TOKEN_TILE % t_ctx == 0 and t_lat % TOKEN_TILE == 0 and TOKEN_TILE % GRID_W == 0
    assert t_lat % MOE_TILE == 0 and (n_ctx * t_ctx) % MOE_TILE == 0 and TOKEN_TILE % chunk == 0
    assert n_exp == N_GROUPS * GROUP_SIZE and c_rw % (SCAN_HEADS * hs) == 0
    assert SCAN_CHUNK == hs and t_ctx % SCAN_CHUNK == 0 and t_lat % SCAN_CHUNK == 0
    assert n_ctx % SCAN_PAR == 0 and n_lat % SCAN_PAR == 0
    assert (n_ctx * t_ctx) % ROUTE_TILE == 0 and (n_lat * t_lat) % ROUTE_TILE == 0

    mod_rows = 8 * ((1 + n_lat + 7) // 8)
    cvec = jnp.zeros((mod_rows, d), F32).at[0].set(c_ctx).at[1:1 + n_lat].set(c)
    mod = _modulation(cvec, ada_w, ada_b)

    sizes = (conv_c, conv_c, conv_c, c_rw, c_rw, c_rw, 2 * lora_w, 2 * lora_a, lora_g, sgu_c, sgu_c, 3 * d)
    offs = np.concatenate([[0], np.cumsum(sizes)])
    assert offs[-1] == w_in.shape[2]
    rw_lo, rw_hi = int(offs[3]), int(offs[9])
    seg = jnp.asarray(np.kron(np.eye(n_head), np.ones((hs, hs))), BF16)
    n_str = c_rw // (SCAN_HEADS * hs)

    def side_by_side(s):
        s = s.reshape(s.shape[0], 2, n_str, SCAN_HEADS, hs, hs)
        return jnp.swapaxes(s, 3, 4).reshape(s.shape[0], 2, n_str, hs, SCAN_HEADS * hs)

    def from_side_by_side(s):
        s = s.reshape(s.shape[0], 2, n_str, hs, SCAN_HEADS, hs)
        return jnp.swapaxes(s, 3, 4).reshape(s.shape[0], 2, n_head, hs, hs)

    rw_t = router_w.T
    rw_hi_b = rw_t.astype(BF16)
    rw_lo_b = (rw_t - rw_hi_b.astype(F32)).astype(BF16)
    bias_b = jnp.broadcast_to(router_b.astype(F32)[:, None], (n_exp, ROUTE_TILE))
    final_g2 = final_g.reshape(1, d)

    n_ctx_tok = n_ctx * t_ctx
    groups = [
        dict(x=x_prompt.reshape(n_ctx_tok, d), t=t_ctx, cols=t_ctx, rows=(0, n_ctx_tok), s0=None),
        dict(x=x_sample.reshape(n_lat * t_lat, d), t=t_lat, cols=GRID_W, rows=(1, t_lat), s0=state_rwkv),
    ]
    new_states = []
    for l in range(n_layer):
        mod_l = mod[l]
        g1 = norm1_g[l].reshape(1, d)
        win = w_in[l]
        win_r = win[:, rw_lo:rw_hi].astype(BF16)
        win_c = jnp.concatenate([win[:, :rw_lo], win[:, rw_hi:]], axis=1).astype(BF16)
        zero_w = jnp.zeros((lora_w, c_rw), F32)
        w2cat = jnp.concatenate([jnp.concatenate([rwkv_w2[l, 0], zero_w], axis=1),
                                 jnp.concatenate([zero_w, rwkv_w2[l, 1]], axis=1)], axis=0).astype(BF16)
        a2cat = jnp.concatenate([jnp.concatenate([rwkv_a2[l, 0], zero_w], axis=1),
                                 jnp.concatenate([zero_w, rwkv_a2[l, 1]], axis=1)], axis=0).astype(BF16)
        bs_full = jnp.repeat(sgu_bs[l].T, sgu_c // n_grp, axis=1)
        k_a = rwkv_k_a[l].reshape(1, c_rw)
        weights = dict(
            g2=rwkv_g2[l].astype(BF16), ws=sgu_ws[l].astype(BF16), pc=proj_conv[l].astype(BF16),
            ps=proj_sgu[l].astype(BF16), pr=proj_rwkv[l].astype(BF16), wo=w_out[l].astype(BF16),
            w1=exp_w1[l].astype(BF16), w3=exp_w3[l].astype(BF16), w2=exp_w2[l].astype(BF16))
        for grp in groups:
            x, rows = grp["x"], grp["rows"]
            r, k, v, kk, lw, aa, bonus, g = _prep(
                rows, x, mod_l, g1, win_r,
                rwkv_w0[l].reshape(1, 2 * c_rw), w2cat, rwkv_a0[l].reshape(1, 2 * c_rw), a2cat,
                weights["g2"], rwkv_k_k[l].reshape(1, c_rw), k_a, rwkv_r_k[l].reshape(1, c_rw), seg)
            part, grw = _branch(
                rows, grp["cols"], x, mod_l, g1, win_c, conv_w[l], sgu_ln_g[l].reshape(1, sgu_c),
                sgu_ln_b[l].reshape(1, sgu_c), weights["ws"], bs_full, weights["pc"], weights["ps"])
            n_seq = x.shape[0] // grp["t"]
            if grp["s0"] is None:
                s0 = jnp.zeros((n_seq, 2, n_str, hs, SCAN_HEADS * hs), F32)
            else:
                s0 = side_by_side(grp["s0"][:, l].astype(F32))
            yf, yb, s_fin = _scan(grp["t"], r, k, v, kk, lw, aa, k_a, s0)
            if grp["s0"] is None:
                new_states.append(from_side_by_side(s_fin))
            x1, h2, logits_t = _merge(
                rows, x, yf, yb, bonus, g, part, grw, mod_l, rwkv_lnx_w[l].reshape(1, c_rw),
                rwkv_lnx_b[l].reshape(1, c_rw), seg, weights["pr"], weights["wo"],
                norm2_g[l].reshape(1, d), rw_hi_b, rw_lo_b, hs)
            gates = _route(logits_t, bias_b).T
            grp["x"] = _moe(rows, h2, gates, x1, mod_l, weights["w1"], weights["w3"], weights["w2"],
                            final_g2, final_norm=(l == n_layer - 1))

    y_prompt = groups[0]["x"].reshape(n_ctx, t_ctx, d)
    y_sample = groups[1]["x"].reshape(n_lat, t_lat, d)
    new_state = jnp.stack(new_states, axis=1).astype(x_prompt.dtype)
    return (y_prompt, y_sample, new_state)
```

```python
import functools
import math

import jax
import jax.numpy as jnp
import numpy as np
from jax import lax
from jax.experimental import pallas as pl
from jax.experimental.pallas import tpu as pltpu

F32 = jnp.float32
BF16 = jnp.bfloat16

GRID_W = 64
N_GROUPS = 4
GROUP_SIZE = 4
NORM_EPS = 1e-6
GN_EPS = 64e-5
KK_EPS = 1e-12
DECAY_SCALE = math.exp(-0.5)

TOKEN_TILE = 512
MOE_TILE = 512
MOE_BLOCK = 128
MOE_BLOCK_LOG2 = 7
BLK_STRIDE = 16
SCAN_CHUNK = 64
SCAN_HEADS = 4
SCAN_PAR = 2
MOD_COLS = 1536
VMEM_LIMIT = 56 * 1024 * 1024


def _dot(a, b):
    return jnp.dot(a, b, preferred_element_type=F32)


def _dot_nt(a, b):
    return lax.dot_general(a, b, (((1,), (1,)), ((), ())), preferred_element_type=F32)


def _split(x):
    hi = x.astype(BF16)
    lo = (x - hi.astype(F32)).astype(BF16)
    return hi, lo


def _seg_sum(x, seg):
    hi, lo = _split(x)
    return _dot(hi, seg) + _dot(lo, seg)


def _sigmoid(x):
    return 1.0 / (1.0 + jnp.exp(-x))


def _norm_mod(x, gain, shift, scale):
    ms = jnp.mean(x * x, axis=-1, keepdims=True)
    return (x * lax.rsqrt(ms + NORM_EPS) * gain) * (1.0 + scale) + shift


def _const_spec(shape):
    nd = len(shape)
    return pl.BlockSpec(shape, lambda *_: (0,) * nd)


def _mod_kernel(c_ref, w_ref, b_ref, o_ref):
    c = c_ref[...]
    s = c * _sigmoid(c)
    o_ref[0] = _dot(s.astype(BF16), w_ref[0].astype(BF16)) + b_ref[0]


def _modulation(cvec, ada_w, ada_b):
    n_layer, d, cols = ada_w.shape
    rows = cvec.shape[0]
    return pl.pallas_call(
        _mod_kernel,
        grid=(n_layer, cols // MOD_COLS),
        in_specs=[
            pl.BlockSpec((rows, d), lambda l, j: (0, 0)),
            pl.BlockSpec((1, d, MOD_COLS), lambda l, j: (l, 0, j)),
            pl.BlockSpec((1, 1, MOD_COLS), lambda l, j: (l, 0, j)),
        ],
        out_specs=pl.BlockSpec((1, rows, MOD_COLS), lambda l, j: (l, 0, j)),
        out_shape=jax.ShapeDtypeStruct((n_layer, rows, cols), F32),
        compiler_params=pltpu.CompilerParams(
            dimension_semantics=("arbitrary", "arbitrary"), vmem_limit_bytes=VMEM_LIMIT),
        name="mod",
    )(cvec, ada_w, ada_b.reshape(n_layer, 1, cols))


def _prep_kernel(x_ref, mod_ref, g1_ref, win_ref, w0_ref, w2_ref, a0_ref, a2_ref, g2_ref,
                 kk_ref, ka_ref, rk_ref, seg_ref,
                 r_o, k_o, v_o, kk_o, lw_o, aa_o, bonus_o, g_o, *, row0, tiles_per_seq):
    row = row0 + pl.program_id(0) // tiles_per_seq
    x = x_ref[...]
    d = x.shape[1]
    c = r_o.shape[1]
    shift = mod_ref[pl.ds(row, 1), 0:d]
    scale = mod_ref[pl.ds(row, 1), d:2 * d]
    h = _norm_mod(x, g1_ref[...], shift, scale)
    z = _dot(h.astype(BF16), win_ref[...])
    r = z[:, 0:c]
    k = z[:, c:2 * c]
    v = z[:, 2 * c:3 * c]
    lo = 3 * c
    w_lo = z[:, lo:lo + 128]
    a_lo = z[:, lo + 128:lo + 256]
    g_lo = z[:, lo + 256:lo + 384]
    w_logit = w0_ref[...] + _dot(jnp.tanh(w_lo).astype(BF16), w2_ref[...])
    lw = -DECAY_SCALE * _sigmoid(w_logit)
    aa = _sigmoid(a0_ref[...] + _dot(a_lo.astype(BF16), a2_ref[...]))
    g = _dot(_sigmoid(g_lo).astype(BF16), g2_ref[...])
    seg = seg_ref[...]
    kk0 = k * kk_ref[...]
    kk = kk0 / jnp.maximum(jnp.sqrt(_seg_sum(kk0 * kk0, seg)), KK_EPS)
    ka = ka_ref[...]
    k_dirs = k * (1.0 + (aa[:, 0:c] - 1.0) * ka) + k * (1.0 + (aa[:, c:2 * c] - 1.0) * ka)
    rk = _seg_sum(r * k_dirs * rk_ref[...], seg)
    r_o[...] = r
    k_o[...] = k
    v_o[...] = v
    kk_o[...] = kk
    lw_o[0] = lw[:, 0:c]
    lw_o[1] = lw[:, c:2 * c]
    aa_o[0] = aa[:, 0:c]
    aa_o[1] = aa[:, c:2 * c]
    bonus_o[...] = rk * v
    g_o[...] = g


def _prep(rows, x, mod, g1, win_r, w0, w2, a0, a2, g2, k_k, k_a, r_k, seg):
    n, d = x.shape
    c = seg.shape[0]
    tm = TOKEN_TILE
    tok = lambda w: pl.BlockSpec((tm, w), lambda i: (i, 0))
    tok2 = pl.BlockSpec((2, tm, c), lambda i: (0, i, 0))
    consts = [mod, g1, win_r, w0, w2, a0, a2, g2, k_k, k_a, r_k, seg]
    sds = jax.ShapeDtypeStruct
    return pl.pallas_call(
        functools.partial(_prep_kernel, row0=rows[0], tiles_per_seq=rows[1] // tm),
        grid=(n // tm,),
        in_specs=[tok(d)] + [_const_spec(a.shape) for a in consts],
        out_specs=[tok(c), tok(c), tok(c), tok(c), tok2, tok2, tok(c), tok(c)],
        out_shape=[sds((n, c), F32)] * 4 + [sds((2, n, c), F32)] * 2 + [sds((n, c), F32)] * 2,
        compiler_params=pltpu.CompilerParams(
            dimension_semantics=("arbitrary",), vmem_limit_bytes=VMEM_LIMIT),
        name="prep",
    )(x, *consts)


def _branch_kernel(x_ref, mod_ref, g1_ref, win_ref, cw_ref, lng_ref, lnb_ref,
                   ws_ref, bs_ref, pc_ref, ps_ref, part_o, grw_o, *, row0, tiles_per_seq, cols):
    row = row0 + pl.program_id(0) // tiles_per_seq
    x = x_ref[...]
    tm, d = x.shape
    cw_w = cw_ref.shape[1]
    sg_w = lng_ref.shape[1]
    chunk = ws_ref.shape[1]
    n_grp = ws_ref.shape[0]
    shift = mod_ref[pl.ds(row, 1), 0:d]
    scale = mod_ref[pl.ds(row, 1), d:2 * d]
    h = _norm_mod(x, g1_ref[...], shift, scale)
    z = _dot(h.astype(BF16), win_ref[...])
    cb = z[:, 0:cw_w]
    cc = z[:, cw_w:2 * cw_w]
    cx = z[:, 2 * cw_w:3 * cw_w]
    o = 3 * cw_w
    su = z[:, o:o + sg_w]
    sv = z[:, o + sg_w:o + 2 * sg_w]
    gates = _sigmoid(z[:, o + 2 * sg_w:])
    xc = cc * cx
    col = lax.broadcasted_iota(jnp.int32, xc.shape, 0) & (cols - 1)
    prev = jnp.where(col == 0, 0.0, pltpu.roll(xc, 1, 0))
    nxt = jnp.where(col == cols - 1, 0.0, pltpu.roll(xc, tm - 1, 0))
    cw = cw_ref[...]
    y_conv = cb * (cw[0:1] * prev + cw[1:2] * xc + cw[2:3] * nxt)
    mu = jnp.mean(sv, axis=-1, keepdims=True)
    dv = sv - mu
    var = jnp.mean(dv * dv, axis=-1, keepdims=True)
    vn = (dv * lax.rsqrt(var + NORM_EPS) * lng_ref[...] + lnb_ref[...]).astype(BF16)
    lane_grp = lax.broadcasted_iota(jnp.int32, (chunk, sg_w), 1) // (sg_w // n_grp)
    parts = []
    for ci in range(tm // chunk):
        vc = vn[ci * chunk:(ci + 1) * chunk]
        mixed = bs_ref[...]
        for gi in range(n_grp):
            mixed = mixed + jnp.where(lane_grp == gi, _dot(ws_ref[gi], vc), 0.0)
        parts.append(su[ci * chunk:(ci + 1) * chunk] * mixed)
    y_sgu = jnp.concatenate(parts, axis=0)
    part_o[...] = (gates[:, 0:d] * _dot(y_conv.astype(BF16), pc_ref[...])
                   + gates[:, 2 * d:3 * d] * _dot(y_sgu.astype(BF16), ps_ref[...]))
    grw_o[...] = gates[:, d:2 * d]


def _branch(rows, cols, x, mod, g1, win_c, conv_w, ln_g, ln_b, ws, bs_full, proj_conv, proj_sgu):
    n, d = x.shape
    tm = TOKEN_TILE
    tok = pl.BlockSpec((tm, d), lambda i: (i, 0))
    consts = [mod, g1, win_c, conv_w, ln_g, ln_b, ws, bs_full, proj_conv, proj_sgu]
    return pl.pallas_call(
        functools.partial(_branch_kernel, row0=rows[0], tiles_per_seq=rows[1] // tm, cols=cols),
        grid=(n // tm,),
        in_specs=[tok] + [_const_spec(a.shape) for a in consts],
        out_specs=[tok, tok],
        out_shape=[jax.ShapeDtypeStruct((n, d), F32)] * 2,
        compiler_params=pltpu.CompilerParams(
            dimension_semantics=("arbitrary",), vmem_limit_bytes=VMEM_LIMIT),
        name="branch",
    )(x, *consts)


def _scan_streams(streams, bd_f32, bd_bf16):
    n_l = streams[0][0].shape[0]
    n_rep = bd_bf16.shape[0] // n_l
    ns = range(len(streams))

    def bd(x):
        return jnp.concatenate([x.astype(BF16)] * n_rep, axis=0) * bd_bf16

    def mm(x, y):
        return _dot(x.astype(BF16), bd(y))

    r, k, v, kk, lw, aa, ka, z, consts = map(list, zip(*streams))
    tri, strict, incl, levels, eye, end = map(list, zip(*consts))
    b_vec = [kk[i] * aa[i] for i in ns]
    kd = [k[i] * (1.0 + (aa[i] - 1.0) * ka[i]) for i in ns]
    l_hi = [lw[i].astype(BF16) for i in ns]
    l_mid = [(lw[i] - l_hi[i].astype(F32)).astype(BF16) for i in ns]
    l_lo = [(lw[i] - l_hi[i].astype(F32) - l_mid[i].astype(F32)).astype(BF16) for i in ns]
    c = [_dot(tri[i], l_hi[i]) + _dot(tri[i], l_mid[i]) + _dot(tri[i], l_lo[i]) for i in ns]
    c_end = [c[i][end[i]:end[i] + 1] for i in ns]
    e_neg = [jnp.exp(-c[i]) for i in ns]
    e_end = [jnp.exp(c_end[i] - c[i]) for i in ns]
    lhs = [jnp.concatenate([-kk[i] * jnp.exp(c[i] - lw[i]), r[i] * jnp.exp(c[i])], axis=0).astype(BF16)
           for i in ns]
    m_b = [_dot_nt(lhs[i], bd(b_vec[i] * e_neg[i])) for i in ns]
    m_k = [_dot_nt(lhs[i], bd(kd[i] * e_neg[i])) for i in ns]
    m_s = [_dot_nt(lhs[i], z[i].astype(BF16)) for i in ns]
    a_ab = [jnp.where(strict[i], m_b[i][0:n_l], 0.0) for i in ns]
    t_inv = [eye[i] + jnp.where(levels[i][0], a_ab[i], 0.0) for i in ns]
    for q in range(1, len(levels[0])):
        x1 = [mm(t_inv[i], jnp.where(levels[i][q], a_ab[i], 0.0)) for i in ns]
        x2 = [mm(x1[i], t_inv[i]) for i in ns]
        t_inv = [t_inv[i] + x2[i] for i in ns]
    bd_v = [bd(v[i]) for i in ns]
    w = [m_s[i][0:n_l] + _dot(jnp.where(strict[i], m_k[i][0:n_l], 0.0).astype(BF16), bd_v[i]) for i in ns]
    u = [mm(t_inv[i], w[i]) for i in ns]
    y = [m_s[i][n_l:] + _dot(jnp.where(incl[i], m_b[i][n_l:], 0.0).astype(BF16), bd(u[i]))
         + _dot(jnp.where(incl[i], m_k[i][n_l:], 0.0).astype(BF16), bd_v[i]) for i in ns]
    uv_t = [jnp.concatenate([u[i], v[i]], axis=0).T.astype(BF16) for i in ns]
    bk = [jnp.concatenate([b_vec[i] * e_end[i], kd[i] * e_end[i]], axis=0).astype(BF16) for i in ns]
    z_new = [z[i] * jnp.exp(c_end[i]) + _dot(uv_t[i], bk[i]) * bd_f32 for i in ns]
    return list(zip(y, z_new))


def _scan_kernel(rf_ref, kf_ref, vf_ref, kkf_ref, lwf_ref, aaf_ref,
                 rb_ref, kb_ref, vb_ref, kkb_ref, lwb_ref, aab_ref,
                 ka_ref, s0_ref, yf_o, yb_o, sfin_o, z_scr):
    j = pl.program_id(1)
    n_par, n_l, c = rf_ref.shape
    lanes = z_scr.shape[1]
    n_str = c // lanes
    hs = lanes // SCAN_HEADS

    row = lax.broadcasted_iota(jnp.int32, (lanes, lanes), 0)
    colm = lax.broadcasted_iota(jnp.int32, (lanes, lanes), 1)
    bd_f32 = ((row // hs) == (colm // hs)).astype(F32)
    bd_bf16 = bd_f32.astype(BF16)
    slots = [(p, di, hg) for p in range(n_par) for di in range(2) for hg in range(n_str)]

    @pl.when(j == 0)
    def _():
        for zi, (p, di, hg) in enumerate(slots):
            x = s0_ref[p, 0, di, hg]
            z_scr[zi] = jnp.concatenate([x] * SCAN_HEADS, axis=0) * bd_f32

    t = lax.broadcasted_iota(jnp.int32, (n_l, lanes), 0)
    s = lax.broadcasted_iota(jnp.int32, (n_l, lanes), 1) % n_l
    ti = lax.broadcasted_iota(jnp.int32, (n_l, n_l), 0)
    si = lax.broadcasted_iota(jnp.int32, (n_l, n_l), 1)
    eye = (t == s).astype(F32)
    n_lvl = n_l.bit_length() - 1

    def make_consts(reverse):
        if not reverse:
            tri = (ti >= si).astype(BF16)
            strict, incl = t > s, t >= s
            levels = [(((t >> q) & 1) == 1) & ((s >> q) == (t >> q) - 1) for q in range(n_lvl)]
            end = n_l - 1
        else:
            tri = (ti <= si).astype(BF16)
            strict, incl = t < s, t <= s
            levels = [(((t >> q) & 1) == 0) & ((s >> q) == (t >> q) + 1) for q in range(n_lvl)]
            end = 0
        return tri, strict, incl, levels, eye, end

    ka = ka_ref[...]
    dirs = ((rf_ref, kf_ref, vf_ref, kkf_ref, lwf_ref, aaf_ref, yf_o, make_consts(False)),
            (rb_ref, kb_ref, vb_ref, kkb_ref, lwb_ref, aab_ref, yb_o, make_consts(True)))
    streams = []
    for p, di, hg in slots:
        r_ref, k_ref, v_ref, kk_ref, lw_ref, aa_ref, _, consts = dirs[di]
        ls = slice(hg * lanes, (hg + 1) * lanes)
        streams.append((r_ref[p, :, ls], k_ref[p, :, ls], v_ref[p, :, ls], kk_ref[p, :, ls],
                        lw_ref[0, p, :, ls], aa_ref[0, p, :, ls], ka[:, ls], z_scr[len(streams)], consts))
    for zi, ((y, z_new), (p, di, hg)) in enumerate(zip(_scan_streams(streams, bd_f32, bd_bf16), slots)):
        dirs[di][6][p, :, hg * lanes:(hg + 1) * lanes] = y
        z_scr[zi] = z_new

    @pl.when(j == pl.num_programs(1) - 1)
    def _():
        for zi, (p, di, hg) in enumerate(slots):
            zz = z_scr[zi]
            acc = zz[0:hs]
            for hh in range(1, SCAN_HEADS):
                acc = acc + zz[hh * hs:(hh + 1) * hs]
            sfin_o[p, 0, di, hg] = acc


def _scan(t_len, r, k, v, kk, lw, aa, k_a, s0):
    n, c = r.shape
    n_seq = s0.shape[0]
    n_par = SCAN_PAR
    n_l = SCAN_CHUNK
    n_c = t_len // n_l
    hs = s0.shape[3]
    lanes = SCAN_HEADS * hs
    n_str = c // lanes
    part = lambda a: a.reshape(n_par, n // n_par, c)
    part2 = lambda a: a.reshape(2, n_par, n // n_par, c)
    s0p = s0.reshape((n_par, n_seq // n_par) + s0.shape[1:])
    fwd = pl.BlockSpec((n_par, n_l, c), lambda q, j: (0, q * n_c + j, 0))
    bwd = pl.BlockSpec((n_par, n_l, c), lambda q, j: (0, q * n_c + n_c - 1 - j, 0))
    fwd2 = pl.BlockSpec((1, n_par, n_l, c), lambda q, j: (0, 0, q * n_c + j, 0))
    bwd2 = pl.BlockSpec((1, n_par, n_l, c), lambda q, j: (1, 0, q * n_c + n_c - 1 - j, 0))
    st = pl.BlockSpec((n_par, 1, 2, n_str, hs, lanes), lambda q, j: (0, q, 0, 0, 0, 0))
    sds = jax.ShapeDtypeStruct
    args = [part(r), part(k), part(v), part(kk), part2(lw), part2(aa)]
    yf, yb, s_fin = pl.pallas_call(
        _scan_kernel,
        grid=(n_seq // n_par, n_c),
        in_specs=[fwd, fwd, fwd, fwd, fwd2, fwd2, bwd, bwd, bwd, bwd, bwd2, bwd2,
                  _const_spec(k_a.shape), st],
        out_specs=[fwd, bwd, st],
        out_shape=[sds((n_par, n // n_par, c), F32), sds((n_par, n // n_par, c), F32), sds(s0p.shape, F32)],
        scratch_shapes=[pltpu.VMEM((n_par * 2 * n_str, lanes, lanes), F32)],
        compiler_params=pltpu.CompilerParams(
            dimension_semantics=("arbitrary", "arbitrary"), vmem_limit_bytes=VMEM_LIMIT),
        name="scan",
    )(*args, *args, k_a, s0p)
    return yf.reshape(n, c), yb.reshape(n, c), s_fin.reshape(s0.shape)


def _merge_kernel(x_ref, yf_ref, yb_ref, bonus_ref, g_ref, part_ref, grw_ref, mod_ref,
                  lnw_ref, lnb_ref, seg_ref, prw_ref, wout_ref, g2_ref, rwh_ref, rwl_ref,
                  x1_o, h2_o, lg_o, *, head_size, row0, tiles_per_seq):
    row = row0 + pl.program_id(0) // tiles_per_seq
    x = x_ref[...]
    d = x.shape[1]
    seg = seg_ref[...]
    y = yf_ref[...] + yb_ref[...]
    hs_inv = 1.0 / head_size
    mu = _seg_sum(y, seg) * hs_inv
    dy = y - mu
    var = _seg_sum(dy * dy, seg) * hs_inv
    yn = dy * lax.rsqrt(var + GN_EPS) * lnw_ref[...] + lnb_ref[...]
    y_rwkv = (yn + bonus_ref[...]) * g_ref[...]
    merged = part_ref[...] + grw_ref[...] * _dot(y_rwkv.astype(BF16), prw_ref[...])
    mix = _dot(merged.astype(BF16), wout_ref[...])
    gate1 = mod_ref[pl.ds(row, 1), 2 * d:3 * d]
    shift2 = mod_ref[pl.ds(row, 1), 3 * d:4 * d]
    scale2 = mod_ref[pl.ds(row, 1), 4 * d:5 * d]
    x1 = x + gate1 * mix
    h2 = _norm_mod(x1, g2_ref[...], shift2, scale2)
    h_hi, h_lo = _split(h2)
    rwh = rwh_ref[...]
    x1_o[...] = x1
    h2_o[...] = h_hi
    lg_o[...] = _dot_nt(rwh, h_hi) + _dot_nt(rwl_ref[...], h_hi) + _dot_nt(rwh, h_lo)


def _merge(rows, x, yf, yb, bonus, g, part, grw, mod, lnw, lnb, seg, proj_rwkv, w_out, g2n, rw_hi, rw_lo,
           head_size):
    n, d = x.shape
    c = seg.shape[0]
    n_e = rw_hi.shape[0]
    tm = TOKEN_TILE
    tok = lambda w: pl.BlockSpec((tm, w), lambda i: (i, 0))
    consts = [mod, lnw, lnb, seg, proj_rwkv, w_out, g2n, rw_hi, rw_lo]
    sds = jax.ShapeDtypeStruct
    return pl.pallas_call(
        functools.partial(_merge_kernel, head_size=head_size, row0=rows[0], tiles_per_seq=rows[1] // tm),
        grid=(n // tm,),
        in_specs=[tok(d), tok(c), tok(c), tok(c), tok(c), tok(d), tok(d)]
        + [_const_spec(a.shape) for a in consts],
        out_specs=[tok(d), tok(d), pl.BlockSpec((n_e, tm), lambda i: (0, i))],
        out_shape=[sds((n, d), F32), sds((n, d), BF16), sds((n_e, n), F32)],
        compiler_params=pltpu.CompilerParams(
            dimension_semantics=("arbitrary",), vmem_limit_bytes=VMEM_LIMIT),
        name="merge",
    )(x, yf, yb, bonus, g, part, grw, *consts)


def _route_kernel(lg_ref, b_ref, tri_ref, g4_o, pos_o, blk_o):
    scores = _sigmoid(lg_ref[...])
    biased = scores + b_ref[...]
    n_e, n_t = scores.shape
    rows = [biased[e:e + 1] for e in range(n_e)]
    srow = [scores[e:e + 1] for e in range(n_e)]

    def top2_sum(a, b, c, d):
        s1, t1 = jnp.maximum(a, b), jnp.minimum(a, b)
        s2, t2 = jnp.maximum(c, d), jnp.minimum(c, d)
        return jnp.maximum(s1, s2) + jnp.maximum(jnp.minimum(s1, s2), jnp.maximum(t1, t2))

    best = top2_sum(*rows[0:GROUP_SIZE])
    g_sel = jnp.zeros(best.shape, jnp.int32)
    for gi in range(1, N_GROUPS):
        gs = top2_sum(*rows[gi * GROUP_SIZE:(gi + 1) * GROUP_SIZE])
        better = gs > best
        best = jnp.where(better, gs, best)
        g_sel = jnp.where(better, gi, g_sel)

    def pick(src, j):
        out = src[j]
        for gi in range(1, N_GROUPS):
            out = jnp.where(g_sel == gi, src[gi * GROUP_SIZE + j], out)
        return out

    in_b = [pick(rows, j) for j in range(GROUP_SIZE)]
    in_s = [pick(srow, j) for j in range(GROUP_SIZE)]

    def arg_first_max(vals):
        best_v = vals[0]
        best_i = jnp.zeros(best_v.shape, jnp.int32)
        for j in range(1, len(vals)):
            better = vals[j] > best_v
            best_v = jnp.where(better, vals[j], best_v)
            best_i = jnp.where(better, j, best_i)
        return best_i

    i1 = arg_first_max(in_b)
    i2 = arg_first_max([jnp.where(i1 == j, -jnp.inf, in_b[j]) for j in range(GROUP_SIZE)])

    def take(src, idx):
        out = src[0]
        for j in range(1, GROUP_SIZE):
            out = jnp.where(idx == j, src[j], out)
        return out

    g1, g2 = take(in_s, i1), take(in_s, i2)
    den = g1 + g2
    j_idx = lax.broadcasted_iota(jnp.int32, (g4_o.shape[0], n_t), 0)
    g4_o[...] = jnp.where(j_idx == i1, g1 / den, jnp.where(j_idx == i2, g2 / den, 0.0))

    grp = lax.broadcasted_iota(jnp.int32, (n_e, n_t), 0)
    onehot = (grp == g_sel).astype(F32)
    rank = _dot(onehot.astype(BF16), tri_ref[...])
    cnt = jnp.sum(onehot, axis=1, keepdims=True).astype(jnp.int32)
    padded = ((cnt + (MOE_BLOCK - 1)) >> MOE_BLOCK_LOG2) << MOE_BLOCK_LOG2
    start = jnp.zeros((1, 1), jnp.int32)
    pos = jnp.zeros((1, n_t), jnp.int32)
    lane = lax.broadcasted_iota(jnp.int32, (1, blk_o.shape[2]), 1)
    blk_grp = jnp.zeros(lane.shape, jnp.int32)
    for gi in range(N_GROUPS):
        pos = pos + jnp.where(g_sel == gi, start + rank[gi:gi + 1].astype(jnp.int32), 0)
        start = start + padded[gi:gi + 1]
        if gi < N_GROUPS - 1:
            blk_grp = blk_grp + (start <= lane * MOE_BLOCK).astype(jnp.int32)
    pos_o[...] = pos
    n_blk = n_t // MOE_BLOCK + N_GROUPS
    blk_o[0] = jnp.where(lane < n_blk, blk_grp, jnp.where(lane == n_blk, start >> MOE_BLOCK_LOG2, 0))


def _route(logits_t, bias_b, tri):
    n_e, n = logits_t.shape
    rt = MOE_TILE
    sds = jax.ShapeDtypeStruct
    return pl.pallas_call(
        _route_kernel,
        grid=(n // rt,),
        in_specs=[pl.BlockSpec((n_e, rt), lambda i: (0, i)), _const_spec(bias_b.shape), _const_spec(tri.shape)],
        out_specs=[pl.BlockSpec((8, rt), lambda i: (0, i)), pl.BlockSpec((1, rt), lambda i: (0, i)),
                   pl.BlockSpec((1, 1, 128), lambda i: (i, 0, 0))],
        out_shape=[sds((8, n), F32), sds((1, n), jnp.int32), sds((n // rt, 1, 128), jnp.int32)],
        compiler_params=pltpu.CompilerParams(dimension_semantics=("arbitrary",)),
        name="route",
    )(logits_t, bias_b, tri)


def _moe_kernel(blk_ref, h_ref, posr_ref, posc_ref, ghi_ref, glo_ref, x1_ref, mod_ref, w1_ref, w3_ref, w2_ref,
                fg_ref, o_ref, xs_ref, ys_ref, *, final_norm, row0, tiles_per_seq):
    i = pl.program_id(0)
    tm, d = x1_ref.shape
    r_rows = xs_ref.shape[0]
    n_blk = r_rows // MOE_BLOCK
    sel = jnp.where(lax.broadcasted_iota(jnp.int32, (r_rows, tm), 0) == posr_ref[...], 1.0, 0.0).astype(BF16)
    xs_ref[...] = _dot(sel, h_ref[...]).astype(BF16)
    gs = _dot(sel, ghi_ref[...]) + _dot(sel, glo_ref[...])
    base = i * BLK_STRIDE
    n_used = blk_ref[base + n_blk]
    for b in range(n_blk):
        rows = slice(b * MOE_BLOCK, (b + 1) * MOE_BLOCK)

        @pl.when(b < n_used)
        def _():
            first = blk_ref[base + b] * GROUP_SIZE
            xb = xs_ref[rows]
            acc = jnp.zeros((MOE_BLOCK, d), F32)
            for j in range(GROUP_SIZE):
                a = _dot(xb, w1_ref[first + j])
                g = _dot(xb, w3_ref[first + j])
                hid = (a * _sigmoid(a)) * g
                acc = acc + gs[rows, j:j + 1] * _dot(hid.astype(BF16), w2_ref[first + j])
            ys_ref[rows] = acc.astype(BF16)

        @pl.when(b >= n_used)
        def _():
            ys_ref[rows] = jnp.zeros((MOE_BLOCK, d), BF16)

    selt = jnp.where(lax.broadcasted_iota(jnp.int32, (tm, r_rows), 1) == posc_ref[...], 1.0, 0.0).astype(BF16)
    row = row0 + i // tiles_per_seq
    gate2 = mod_ref[pl.ds(row, 1), 5 * d:6 * d]
    x2 = x1_ref[...] + gate2 * _dot(selt, ys_ref[...])
    if final_norm:
        ms = jnp.mean(x2 * x2, axis=-1, keepdims=True)
        x2 = x2 * lax.rsqrt(ms + NORM_EPS) * fg_ref[...]
    o_ref[...] = x2


def _moe(rows, blk, h2, pos_row, pos_col, g_hi, g_lo, x1, mod, w1, w3, w2, final_g, final_norm):
    n, d = x1.shape
    tm = MOE_TILE
    r_rows = tm + N_GROUPS * MOE_BLOCK
    tok = lambda w: pl.BlockSpec((tm, w), lambda i, *_: (i, 0))
    resident = lambda a: pl.BlockSpec(a.shape, lambda i, *_: (0,) * a.ndim, pipeline_mode=pl.Buffered(1))
    grid_spec = pltpu.PrefetchScalarGridSpec(
        num_scalar_prefetch=1,
        grid=(n // tm,),
        in_specs=[tok(d), pl.BlockSpec((1, tm), lambda i, *_: (0, i)), tok(1), tok(g_hi.shape[1]),
                  tok(g_lo.shape[1]), tok(d), _const_spec(mod.shape), resident(w1), resident(w3), resident(w2),
                  _const_spec(final_g.shape)],
        out_specs=tok(d),
        scratch_shapes=[pltpu.VMEM((r_rows, d), BF16), pltpu.VMEM((r_rows, d), BF16)],
    )
    return pl.pallas_call(
        functools.partial(_moe_kernel, final_norm=final_norm, row0=rows[0],
                          tiles_per_seq=max(rows[1] // tm, 1)),
        grid_spec=grid_spec,
        out_shape=jax.ShapeDtypeStruct((n, d), F32),
        compiler_params=pltpu.CompilerParams(
            dimension_semantics=("arbitrary",), vmem_limit_bytes=VMEM_LIMIT),
        name="moe",
    )(blk, h2, pos_row, pos_col, g_hi, g_lo, x1, mod, w1, w3, w2, final_g)


def kernel(x_prompt, x_sample, state_rwkv, c, c_ctx, norm1_g, norm2_g, final_g, ada_w, ada_b, w_in,
           conv_w, rwkv_w0, rwkv_w2, rwkv_a0, rwkv_a2, rwkv_g2, rwkv_k_k, rwkv_k_a, rwkv_r_k,
           rwkv_lnx_w, rwkv_lnx_b, sgu_ln_g, sgu_ln_b, sgu_ws, sgu_bs, proj_conv, proj_rwkv, proj_sgu,
           w_out, router_w, router_b, exp_w1, exp_w3, exp_w2):
    n_ctx, t_ctx, d = x_prompt.shape
    n_lat, t_lat, _ = x_sample.shape
    n_layer = w_in.shape[0]
    _, n_head, hs = rwkv_r_k.shape
    c_rw = n_head * hs
    conv_c = conv_w.shape[2]
    sgu_c = sgu_ln_g.shape[1]
    n_grp, chunk, _ = sgu_ws.shape[1:]
    lora_w = rwkv_w2.shape[2]
    lora_a = rwkv_a2.shape[2]
    lora_g = rwkv_g2.shape[1]
    n_exp = router_w.shape[1]
    assert 2 * lora_w == 128 and 2 * lora_a == 128 and lora_g == 128
    assert TOKEN_TILE % t_ctx == 0 and t_lat % TOKEN_TILE == 0 and TOKEN_TILE % GRID_W == 0
    assert t_lat % MOE_TILE == 0 and (n_ctx * t_ctx) % MOE_TILE == 0 and TOKEN_TILE % chunk == 0
    assert n_exp == N_GROUPS * GROUP_SIZE and c_rw % (SCAN_HEADS * hs) == 0
    assert SCAN_CHUNK == hs and t_ctx % SCAN_CHUNK == 0 and t_lat % SCAN_CHUNK == 0
    assert n_ctx % SCAN_PAR == 0 and n_lat % SCAN_PAR == 0
    assert MOE_TILE // MOE_BLOCK + N_GROUPS < BLK_STRIDE

    mod_rows = 8 * ((1 + n_lat + 7) // 8)
    cvec = jnp.zeros((mod_rows, d), F32).at[0].set(c_ctx).at[1:1 + n_lat].set(c)
    mod = _modulation(cvec, ada_w, ada_b)

    sizes = (conv_c, conv_c, conv_c, c_rw, c_rw, c_rw, 2 * lora_w, 2 * lora_a, lora_g, sgu_c, sgu_c, 3 * d)
    offs = np.concatenate([[0], np.cumsum(sizes)])
    assert offs[-1] == w_in.shape[2]
    rw_lo, rw_hi = int(offs[3]), int(offs[9])
    seg = jnp.asarray(np.kron(np.eye(n_head), np.ones((hs, hs))), BF16)
    n_str = c_rw // (SCAN_HEADS * hs)

    def side_by_side(s):
        s = s.reshape(s.shape[0], 2, n_str, SCAN_HEADS, hs, hs)
        return jnp.swapaxes(s, 3, 4).reshape(s.shape[0], 2, n_str, hs, SCAN_HEADS * hs)

    def from_side_by_side(s):
        s = s.reshape(s.shape[0], 2, n_str, hs, SCAN_HEADS, hs)
        return jnp.swapaxes(s, 3, 4).reshape(s.shape[0], 2, n_head, hs, hs)

    rw_t = router_w.T
    rw_hi_b = rw_t.astype(BF16)
    rw_lo_b = (rw_t - rw_hi_b.astype(F32)).astype(BF16)
    bias_b = jnp.broadcast_to(router_b.astype(F32)[:, None], (n_exp, MOE_TILE))
    tri = jnp.asarray(np.triu(np.ones((MOE_TILE, MOE_TILE)), 1), BF16)
    final_g2 = final_g.reshape(1, d)

    n_ctx_tok = n_ctx * t_ctx
    groups = [
        dict(x=x_prompt.reshape(n_ctx_tok, d), t=t_ctx, cols=t_ctx, rows=(0, n_ctx_tok), s0=None),
        dict(x=x_sample.reshape(n_lat * t_lat, d), t=t_lat, cols=GRID_W, rows=(1, t_lat), s0=state_rwkv),
    ]
    new_states = []
    for l in range(n_layer):
        mod_l = mod[l]
        g1 = norm1_g[l].reshape(1, d)
        win = w_in[l]
        win_r = win[:, rw_lo:rw_hi].astype(BF16)
        win_c = jnp.concatenate([win[:, :rw_lo], win[:, rw_hi:]], axis=1).astype(BF16)
        zero_w = jnp.zeros((lora_w, c_rw), F32)
        w2cat = jnp.concatenate([jnp.concatenate([rwkv_w2[l, 0], zero_w], axis=1),
                                 jnp.concatenate([zero_w, rwkv_w2[l, 1]], axis=1)], axis=0).astype(BF16)
        a2cat = jnp.concatenate([jnp.concatenate([rwkv_a2[l, 0], zero_w], axis=1),
                                 jnp.concatenate([zero_w, rwkv_a2[l, 1]], axis=1)], axis=0).astype(BF16)
        bs_full = jnp.repeat(sgu_bs[l].T, sgu_c // n_grp, axis=1)
        k_a = rwkv_k_a[l].reshape(1, c_rw)
        weights = dict(
            g2=rwkv_g2[l].astype(BF16), ws=sgu_ws[l].astype(BF16), pc=proj_conv[l].astype(BF16),
            ps=proj_sgu[l].astype(BF16), pr=proj_rwkv[l].astype(BF16), wo=w_out[l].astype(BF16),
            w1=exp_w1[l].astype(BF16), w3=exp_w3[l].astype(BF16), w2=exp_w2[l].astype(BF16))
        for grp in groups:
            x, rows = grp["x"], grp["rows"]
            r, k, v, kk, lw, aa, bonus, g = _prep(
                rows, x, mod_l, g1, win_r,
                rwkv_w0[l].reshape(1, 2 * c_rw), w2cat, rwkv_a0[l].reshape(1, 2 * c_rw), a2cat,
                weights["g2"], rwkv_k_k[l].reshape(1, c_rw), k_a, rwkv_r_k[l].reshape(1, c_rw), seg)
            part, grw = _branch(
                rows, grp["cols"], x, mod_l, g1, win_c, conv_w[l], sgu_ln_g[l].reshape(1, sgu_c),
                sgu_ln_b[l].reshape(1, sgu_c), weights["ws"], bs_full, weights["pc"], weights["ps"])
            n_seq = x.shape[0] // grp["t"]
            if grp["s0"] is None:
                s0 = jnp.zeros((n_seq, 2, n_str, hs, SCAN_HEADS * hs), F32)
            else:
                s0 = side_by_side(grp["s0"][:, l].astype(F32))
            yf, yb, s_fin = _scan(grp["t"], r, k, v, kk, lw, aa, k_a, s0)
            if grp["s0"] is None:
                new_states.append(from_side_by_side(s_fin))
            x1, h2, logits_t = _merge(
                rows, x, yf, yb, bonus, g, part, grw, mod_l, rwkv_lnx_w[l].reshape(1, c_rw),
                rwkv_lnx_b[l].reshape(1, c_rw), seg, weights["pr"], weights["wo"],
                norm2_g[l].reshape(1, d), rw_hi_b, rw_lo_b, hs)
            g4, pos, blk = _route(logits_t, bias_b, tri)
            g4 = jnp.pad(g4[:GROUP_SIZE].T, ((0, 0), (0, 128 - GROUP_SIZE)))
            g_hi = g4.astype(BF16)
            g_lo = (g4 - g_hi.astype(F32)).astype(BF16)
            grp["x"] = _moe(rows, blk[:, 0, :BLK_STRIDE].reshape(-1), h2, pos, pos.reshape(-1, 1), g_hi, g_lo,
                            x1, mod_l, weights["w1"], weights["w3"], weights["w2"], final_g2,
                            final_norm=(l == n_layer - 1))

    y_prompt = groups[0]["x"].reshape(n_ctx, t_ctx, d)
    y_sample = groups[1]["x"].reshape(n_lat, t_lat, d)
    new_state = jnp.stack(new_states, axis=1).astype(x_prompt.dtype)
    return (y_prompt, y_sample, new_state)
```

```python
import functools
import math

import jax
import jax.numpy as jnp
import numpy as np
from jax import lax
from jax.experimental import pallas as pl
from jax.experimental.pallas import tpu as pltpu

F32 = jnp.float32
BF16 = jnp.bfloat16

GRID_W = 64
N_GROUPS = 4
GROUP_SIZE = 4
NORM_EPS = 1e-6
GN_EPS = 64e-5
KK_EPS = 1e-12
DECAY_SCALE = math.exp(-0.5)

TOKEN_TILE = 512
MOE_TILE = 512
MOE_BLOCK = 144
BLK_STRIDE = 16
SCAN_CHUNK = 64
SCAN_HEADS = 4
SCAN_PAR = 2
MOD_COLS = 1536
VMEM_LIMIT = 56 * 1024 * 1024


def _dot(a, b):
    return jnp.dot(a, b, preferred_element_type=F32)


def _dot_nt(a, b):
    return lax.dot_general(a, b, (((1,), (1,)), ((), ())), preferred_element_type=F32)


def _split(x):
    hi = x.astype(BF16)
    lo = (x - hi.astype(F32)).astype(BF16)
    return hi, lo


def _seg_sum(x, seg):
    return _dot(x.astype(BF16), seg)


def _sigmoid(x):
    return 1.0 / (1.0 + jnp.exp(-x))


def _norm_mod(x, gain, shift, scale):
    ms = jnp.mean(x * x, axis=-1, keepdims=True)
    return (x * lax.rsqrt(ms + NORM_EPS) * gain) * (1.0 + scale) + shift


def _const_spec(shape):
    nd = len(shape)
    return pl.BlockSpec(shape, lambda *_: (0,) * nd)


def _mod_kernel(c_ref, w_ref, b_ref, o_ref):
    c = c_ref[...]
    s = c * _sigmoid(c)
    o_ref[0] = _dot(s.astype(BF16), w_ref[0].astype(BF16)) + b_ref[0]


def _modulation(cvec, ada_w, ada_b):
    n_layer, d, cols = ada_w.shape
    rows = cvec.shape[0]
    return pl.pallas_call(
        _mod_kernel,
        grid=(n_layer, cols // MOD_COLS),
        in_specs=[
            pl.BlockSpec((rows, d), lambda l, j: (0, 0)),
            pl.BlockSpec((1, d, MOD_COLS), lambda l, j: (l, 0, j)),
            pl.BlockSpec((1, 1, MOD_COLS), lambda l, j: (l, 0, j)),
        ],
        out_specs=pl.BlockSpec((1, rows, MOD_COLS), lambda l, j: (l, 0, j)),
        out_shape=jax.ShapeDtypeStruct((n_layer, rows, cols), F32),
        compiler_params=pltpu.CompilerParams(
            dimension_semantics=("arbitrary", "arbitrary"), vmem_limit_bytes=VMEM_LIMIT),
        name="mod",
    )(cvec, ada_w, ada_b.reshape(n_layer, 1, cols))


def _prep_kernel(x_ref, mod_ref, g1_ref, win_ref, w0_ref, w2_ref, a0_ref, a2_ref, g2_ref,
                 kk_ref, ka_ref, rk_ref, seg_ref,
                 r_o, k_o, v_o, kk_o, lw_o, aa_o, bonus_o, g_o, *, row0, tiles_per_seq):
    row = row0 + pl.program_id(0) // tiles_per_seq
    x = x_ref[...]
    d = x.shape[1]
    c = r_o.shape[1]
    shift = mod_ref[pl.ds(row, 1), 0:d]
    scale = mod_ref[pl.ds(row, 1), d:2 * d]
    h = _norm_mod(x, g1_ref[...], shift, scale)
    z = _dot(h.astype(BF16), win_ref[...])
    r = z[:, 0:c]
    k = z[:, c:2 * c]
    v = z[:, 2 * c:3 * c]
    lo = 3 * c
    w_lo = z[:, lo:lo + 128]
    a_lo = z[:, lo + 128:lo + 256]
    g_lo = z[:, lo + 256:lo + 384]
    w_logit = w0_ref[...] + _dot(jnp.tanh(w_lo).astype(BF16), w2_ref[...])
    lw = -DECAY_SCALE * _sigmoid(w_logit)
    aa = _sigmoid(a0_ref[...] + _dot(a_lo.astype(BF16), a2_ref[...]))
    g = _dot(_sigmoid(g_lo).astype(BF16), g2_ref[...])
    seg = seg_ref[...]
    kk0 = k * kk_ref[...]
    kk = kk0 / jnp.maximum(jnp.sqrt(_seg_sum(kk0 * kk0, seg)), KK_EPS)
    ka = ka_ref[...]
    k_dirs = k * (1.0 + (aa[:, 0:c] - 1.0) * ka) + k * (1.0 + (aa[:, c:2 * c] - 1.0) * ka)
    rk = _seg_sum(r * k_dirs * rk_ref[...], seg)
    r_o[...] = r
    k_o[...] = k
    v_o[...] = v
    kk_o[...] = kk
    lw_o[0] = lw[:, 0:c]
    lw_o[1] = lw[:, c:2 * c]
    aa_o[0] = aa[:, 0:c]
    aa_o[1] = aa[:, c:2 * c]
    bonus_o[...] = rk * v
    g_o[...] = g


def _prep(rows, x, mod, g1, win_r, w0, w2, a0, a2, g2, k_k, k_a, r_k, seg):
    n, d = x.shape
    c = seg.shape[0]
    tm = TOKEN_TILE
    tok = lambda w: pl.BlockSpec((tm, w), lambda i: (i, 0))
    tok2 = pl.BlockSpec((2, tm, c), lambda i: (0, i, 0))
    consts = [mod, g1, win_r, w0, w2, a0, a2, g2, k_k, k_a, r_k, seg]
    sds = jax.ShapeDtypeStruct
    return pl.pallas_call(
        functools.partial(_prep_kernel, row0=rows[0], tiles_per_seq=rows[1] // tm),
        grid=(n // tm,),
        in_specs=[tok(d)] + [_const_spec(a.shape) for a in consts],
        out_specs=[tok(c), tok(c), tok(c), tok(c), tok2, tok2, tok(c), tok(c)],
        out_shape=[sds((n, c), F32)] * 4 + [sds((2, n, c), F32)] * 2 + [sds((n, c), F32)] * 2,
        compiler_params=pltpu.CompilerParams(
            dimension_semantics=("arbitrary",), vmem_limit_bytes=VMEM_LIMIT),
        name="prep",
    )(x, *consts)


def _branch_kernel(x_ref, mod_ref, g1_ref, win_ref, cw_ref, lng_ref, lnb_ref,
                   ws_ref, bs_ref, pc_ref, ps_ref, part_o, grw_o, *, row0, tiles_per_seq, cols):
    row = row0 + pl.program_id(0) // tiles_per_seq
    x = x_ref[...]
    tm, d = x.shape
    cw_w = cw_ref.shape[1]
    sg_w = lng_ref.shape[1]
    chunk = ws_ref.shape[1]
    n_grp = ws_ref.shape[0]
    shift = mod_ref[pl.ds(row, 1), 0:d]
    scale = mod_ref[pl.ds(row, 1), d:2 * d]
    h = _norm_mod(x, g1_ref[...], shift, scale)
    z = _dot(h.astype(BF16), win_ref[...])
    cb = z[:, 0:cw_w]
    cc = z[:, cw_w:2 * cw_w]
    cx = z[:, 2 * cw_w:3 * cw_w]
    o = 3 * cw_w
    su = z[:, o:o + sg_w]
    sv = z[:, o + sg_w:o + 2 * sg_w]
    gates = _sigmoid(z[:, o + 2 * sg_w:])
    xc = cc * cx
    col = lax.broadcasted_iota(jnp.int32, xc.shape, 0) & (cols - 1)
    prev = jnp.where(col == 0, 0.0, pltpu.roll(xc, 1, 0))
    nxt = jnp.where(col == cols - 1, 0.0, pltpu.roll(xc, tm - 1, 0))
    cw = cw_ref[...]
    y_conv = cb * (cw[0:1] * prev + cw[1:2] * xc + cw[2:3] * nxt)
    mu = jnp.mean(sv, axis=-1, keepdims=True)
    dv = sv - mu
    var = jnp.mean(dv * dv, axis=-1, keepdims=True)
    vn = (dv * lax.rsqrt(var + NORM_EPS) * lng_ref[...] + lnb_ref[...]).astype(BF16)
    lane_grp = lax.broadcasted_iota(jnp.int32, (chunk, sg_w), 1) // (sg_w // n_grp)
    parts = []
    for ci in range(tm // chunk):
        vc = vn[ci * chunk:(ci + 1) * chunk]
        mixed = bs_ref[...]
        for gi in range(n_grp):
            mixed = mixed + jnp.where(lane_grp == gi, _dot(ws_ref[gi], vc), 0.0)
        parts.append(su[ci * chunk:(ci + 1) * chunk] * mixed)
    y_sgu = jnp.concatenate(parts, axis=0)
    part_o[...] = (gates[:, 0:d] * _dot(y_conv.astype(BF16), pc_ref[...])
                   + gates[:, 2 * d:3 * d] * _dot(y_sgu.astype(BF16), ps_ref[...]))
    grw_o[...] = gates[:, d:2 * d]


def _branch(rows, cols, x, mod, g1, win_c, conv_w, ln_g, ln_b, ws, bs_full, proj_conv, proj_sgu):
    n, d = x.shape
    tm = TOKEN_TILE
    tok = pl.BlockSpec((tm, d), lambda i: (i, 0))
    consts = [mod, g1, win_c, conv_w, ln_g, ln_b, ws, bs_full, proj_conv, proj_sgu]
    return pl.pallas_call(
        functools.partial(_branch_kernel, row0=rows[0], tiles_per_seq=rows[1] // tm, cols=cols),
        grid=(n // tm,),
        in_specs=[tok] + [_const_spec(a.shape) for a in consts],
        out_specs=[tok, tok],
        out_shape=[jax.ShapeDtypeStruct((n, d), F32)] * 2,
        compiler_params=pltpu.CompilerParams(
            dimension_semantics=("arbitrary",), vmem_limit_bytes=VMEM_LIMIT),
        name="branch",
    )(x, *consts)


def _scan_constants(n_l, hs):
    lanes = SCAN_HEADS * hs
    bd = np.kron(np.eye(SCAN_HEADS), np.ones((hs, hs)))
    t = np.arange(n_l)[:, None]
    s = np.arange(lanes)[None, :] % n_l
    ti, si = np.arange(n_l)[:, None], np.arange(n_l)[None, :]
    n_lvl = n_l.bit_length() - 1
    tri, masks, lvl_bd = [], [], []
    for reverse in (False, True):
        if not reverse:
            level = lambda q: (((t >> q) & 1) == 1) & ((s >> q) == (t >> q) - 1)
            tri.append(ti >= si)
            masks.append([t > s, t >= s, level(0)])
        else:
            level = lambda q: (((t >> q) & 1) == 0) & ((s >> q) == (t >> q) + 1)
            tri.append(ti <= si)
            masks.append([t < s, t <= s, level(0)])
        lvl_bd.append([np.tile(level(q), (SCAN_HEADS, 1)) * bd for q in range(1, n_lvl)])
    as_j = lambda a, dt: jnp.asarray(np.asarray(a, np.float32), dt)
    return (as_j(bd, F32), as_j(bd, BF16), as_j(t == s, F32), as_j(masks, F32), as_j(lvl_bd, BF16),
            as_j(tri, BF16))


def _scan_streams(streams, bd_f32, bd_bf16, eye, masks_ref, lvl_ref, tri_ref):
    n_l = streams[0][0].shape[0]
    n_rep = bd_bf16.shape[0] // n_l
    ns = range(len(streams))

    def tile(xb):
        return jnp.concatenate([xb] * n_rep, axis=0)

    def bd(x):
        return tile(x.astype(BF16)) * bd_bf16

    def mm(x, y):
        return _dot(x.astype(BF16), bd(y))

    r, k, v, kk, lw, aa, ka, z, di = map(list, zip(*streams))
    strict = lambda i: masks_ref[di[i], 0]
    incl = lambda i: masks_ref[di[i], 1]
    end = [0 if d else n_l - 1 for d in di]
    b_vec = [kk[i] * aa[i] for i in ns]
    kd = [k[i] * (1.0 + (aa[i] - 1.0) * ka[i]) for i in ns]
    l_hi = [lw[i].astype(BF16) for i in ns]
    l_mid = [(lw[i] - l_hi[i].astype(F32)).astype(BF16) for i in ns]
    l_lo = [(lw[i] - l_hi[i].astype(F32) - l_mid[i].astype(F32)).astype(BF16) for i in ns]
    c = [_dot(tri_ref[di[i]], l_hi[i]) + _dot(tri_ref[di[i]], l_mid[i]) + _dot(tri_ref[di[i]], l_lo[i])
         for i in ns]
    c_end = [c[i][end[i]:end[i] + 1] for i in ns]
    e_neg = [jnp.exp(-c[i]) for i in ns]
    e_end = [jnp.exp(c_end[i] - c[i]) for i in ns]
    lhs = [jnp.concatenate([-kk[i] * jnp.exp(c[i] - lw[i]), r[i] * jnp.exp(c[i])], axis=0).astype(BF16)
           for i in ns]
    m_b = [_dot_nt(lhs[i], bd(b_vec[i] * e_neg[i])) for i in ns]
    m_k = [_dot_nt(lhs[i], bd(kd[i] * e_neg[i])) for i in ns]
    m_s = [_dot_nt(lhs[i], z[i].astype(BF16)) for i in ns]
    a_ab = [m_b[i][0:n_l] * strict(i) for i in ns]
    t_inv = [eye + a_ab[i] * masks_ref[di[i], 2] for i in ns]
    a_tile = [tile(a_ab[i].astype(BF16)) for i in ns]
    for q in range(lvl_ref.shape[1]):
        x1 = [_dot(t_inv[i].astype(BF16), a_tile[i] * lvl_ref[di[i], q]) for i in ns]
        x2 = [mm(x1[i], t_inv[i]) for i in ns]
        t_inv = [t_inv[i] + x2[i] for i in ns]
    bd_v = [bd(v[i]) for i in ns]
    w = [m_s[i][0:n_l] + _dot((m_k[i][0:n_l] * strict(i)).astype(BF16), bd_v[i]) for i in ns]
    u = [mm(t_inv[i], w[i]) for i in ns]
    y = [m_s[i][n_l:] + _dot((m_b[i][n_l:] * incl(i)).astype(BF16), bd(u[i]))
         + _dot((m_k[i][n_l:] * incl(i)).astype(BF16), bd_v[i]) for i in ns]
    uv_t = [jnp.concatenate([u[i], v[i]], axis=0).T.astype(BF16) for i in ns]
    bk = [jnp.concatenate([b_vec[i] * e_end[i], kd[i] * e_end[i]], axis=0).astype(BF16) for i in ns]
    z_new = [z[i] * jnp.exp(c_end[i]) + _dot(uv_t[i], bk[i]) * bd_f32 for i in ns]
    return list(zip(y, z_new))


def _scan_kernel(rf_ref, kf_ref, vf_ref, kkf_ref, lwf_ref, aaf_ref,
                 rb_ref, kb_ref, vb_ref, kkb_ref, lwb_ref, aab_ref,
                 ka_ref, s0_ref, bdf_ref, bdb_ref, eye_ref, masks_ref, lvl_ref, tri_ref,
                 yf_o, yb_o, sfin_o, z_scr):
    j = pl.program_id(1)
    n_par, n_l, c = rf_ref.shape
    lanes = z_scr.shape[1]
    n_str = c // lanes
    hs = lanes // SCAN_HEADS
    bd_f32 = bdf_ref[...]
    slots = [(p, di, hg) for p in range(n_par) for di in range(2) for hg in range(n_str)]

    @pl.when(j == 0)
    def _():
        for zi, (p, di, hg) in enumerate(slots):
            x = s0_ref[p, 0, di, hg]
            z_scr[zi] = jnp.concatenate([x] * SCAN_HEADS, axis=0) * bd_f32

    ka = ka_ref[...]
    dirs = ((rf_ref, kf_ref, vf_ref, kkf_ref, lwf_ref, aaf_ref, yf_o),
            (rb_ref, kb_ref, vb_ref, kkb_ref, lwb_ref, aab_ref, yb_o))
    streams = []
    for p, di, hg in slots:
        r_ref, k_ref, v_ref, kk_ref, lw_ref, aa_ref, _ = dirs[di]
        ls = slice(hg * lanes, (hg + 1) * lanes)
        streams.append((r_ref[p, :, ls], k_ref[p, :, ls], v_ref[p, :, ls], kk_ref[p, :, ls],
                        lw_ref[0, p, :, ls], aa_ref[0, p, :, ls], ka[:, ls], z_scr[len(streams)], di))
    results = _scan_streams(streams, bd_f32, bdb_ref[...], eye_ref[...], masks_ref, lvl_ref, tri_ref)
    for zi, ((y, z_new), (p, di, hg)) in enumerate(zip(results, slots)):
        dirs[di][6][p, :, hg * lanes:(hg + 1) * lanes] = y
        z_scr[zi] = z_new

    @pl.when(j == pl.num_programs(1) - 1)
    def _():
        for zi, (p, di, hg) in enumerate(slots):
            zz = z_scr[zi]
            acc = zz[0:hs]
            for hh in range(1, SCAN_HEADS):
                acc = acc + zz[hh * hs:(hh + 1) * hs]
            sfin_o[p, 0, di, hg] = acc


def _scan(t_len, r, k, v, kk, lw, aa, k_a, s0, consts):
    n, c = r.shape
    n_seq = s0.shape[0]
    n_par = SCAN_PAR
    n_l = SCAN_CHUNK
    n_c = t_len // n_l
    hs = s0.shape[3]
    lanes = SCAN_HEADS * hs
    n_str = c // lanes
    part = lambda a: a.reshape(n_par, n // n_par, c)
    part2 = lambda a: a.reshape(2, n_par, n // n_par, c)
    s0p = s0.reshape((n_par, n_seq // n_par) + s0.shape[1:])
    fwd = pl.BlockSpec((n_par, n_l, c), lambda q, j: (0, q * n_c + j, 0))
    bwd = pl.BlockSpec((n_par, n_l, c), lambda q, j: (0, q * n_c + n_c - 1 - j, 0))
    fwd2 = pl.BlockSpec((1, n_par, n_l, c), lambda q, j: (0, 0, q * n_c + j, 0))
    bwd2 = pl.BlockSpec((1, n_par, n_l, c), lambda q, j: (1, 0, q * n_c + n_c - 1 - j, 0))
    st = pl.BlockSpec((n_par, 1, 2, n_str, hs, lanes), lambda q, j: (0, q, 0, 0, 0, 0))
    sds = jax.ShapeDtypeStruct
    args = [part(r), part(k), part(v), part(kk), part2(lw), part2(aa)]
    yf, yb, s_fin = pl.pallas_call(
        _scan_kernel,
        grid=(n_seq // n_par, n_c),
        in_specs=[fwd, fwd, fwd, fwd, fwd2, fwd2, bwd, bwd, bwd, bwd, bwd2, bwd2,
                  _const_spec(k_a.shape), st] + [_const_spec(a.shape) for a in consts],
        out_specs=[fwd, bwd, st],
        out_shape=[sds((n_par, n // n_par, c), F32), sds((n_par, n // n_par, c), F32), sds(s0p.shape, F32)],
        scratch_shapes=[pltpu.VMEM((n_par * 2 * n_str, lanes, lanes), F32)],
        compiler_params=pltpu.CompilerParams(
            dimension_semantics=("arbitrary", "arbitrary"), vmem_limit_bytes=VMEM_LIMIT),
        name="scan",
    )(*args, *args, k_a, s0p, *consts)
    return yf.reshape(n, c), yb.reshape(n, c), s_fin.reshape(s0.shape)


def _merge_kernel(x_ref, yf_ref, yb_ref, bonus_ref, g_ref, part_ref, grw_ref, mod_ref,
                  lnw_ref, lnb_ref, seg_ref, prw_ref, wout_ref, g2_ref, rwh_ref, rwl_ref,
                  x1_o, h2_o, lg_o, *, head_size, row0, tiles_per_seq):
    row = row0 + pl.program_id(0) // tiles_per_seq
    x = x_ref[...]
    d = x.shape[1]
    seg = seg_ref[...]
    y = yf_ref[...] + yb_ref[...]
    hs_inv = 1.0 / head_size
    mu = _seg_sum(y, seg) * hs_inv
    dy = y - mu
    var = _seg_sum(dy * dy, seg) * hs_inv
    yn = dy * lax.rsqrt(var + GN_EPS) * lnw_ref[...] + lnb_ref[...]
    y_rwkv = (yn + bonus_ref[...]) * g_ref[...]
    merged = part_ref[...] + grw_ref[...] * _dot(y_rwkv.astype(BF16), prw_ref[...])
    mix = _dot(merged.astype(BF16), wout_ref[...])
    gate1 = mod_ref[pl.ds(row, 1), 2 * d:3 * d]
    shift2 = mod_ref[pl.ds(row, 1), 3 * d:4 * d]
    scale2 = mod_ref[pl.ds(row, 1), 4 * d:5 * d]
    x1 = x + gate1 * mix
    h2 = _norm_mod(x1, g2_ref[...], shift2, scale2)
    h_hi, h_lo = _split(h2)
    rwh = rwh_ref[...]
    x1_o[...] = x1
    h2_o[...] = h_hi
    lg_o[...] = _dot_nt(rwh, h_hi) + _dot_nt(rwl_ref[...], h_hi) + _dot_nt(rwh, h_lo)


def _merge(rows, x, yf, yb, bonus, g, part, grw, mod, lnw, lnb, seg, proj_rwkv, w_out, g2n, rw_hi, rw_lo,
           head_size):
    n, d = x.shape
    c = seg.shape[0]
    n_e = rw_hi.shape[0]
    tm = TOKEN_TILE
    tok = lambda w: pl.BlockSpec((tm, w), lambda i: (i, 0))
    consts = [mod, lnw, lnb, seg, proj_rwkv, w_out, g2n, rw_hi, rw_lo]
    sds = jax.ShapeDtypeStruct
    return pl.pallas_call(
        functools.partial(_merge_kernel, head_size=head_size, row0=rows[0], tiles_per_seq=rows[1] // tm),
        grid=(n // tm,),
        in_specs=[tok(d), tok(c), tok(c), tok(c), tok(c), tok(d), tok(d)]
        + [_const_spec(a.shape) for a in consts],
        out_specs=[tok(d), tok(d), pl.BlockSpec((n_e, tm), lambda i: (0, i))],
        out_shape=[sds((n, d), F32), sds((n, d), BF16), sds((n_e, n), F32)],
        compiler_params=pltpu.CompilerParams(
            dimension_semantics=("arbitrary",), vmem_limit_bytes=VMEM_LIMIT),
        name="merge",
    )(x, yf, yb, bonus, g, part, grw, *consts)


def _route_kernel(lg_ref, b_ref, tri_ref, g4_o, pos_o, blk_o):
    scores = _sigmoid(lg_ref[...])
    biased = scores + b_ref[...]
    n_e, n_t = scores.shape
    rows = [biased[e:e + 1] for e in range(n_e)]
    srow = [scores[e:e + 1] for e in range(n_e)]

    def top2_sum(a, b, c, d):
        s1, t1 = jnp.maximum(a, b), jnp.minimum(a, b)
        s2, t2 = jnp.maximum(c, d), jnp.minimum(c, d)
        return jnp.maximum(s1, s2) + jnp.maximum(jnp.minimum(s1, s2), jnp.maximum(t1, t2))

    best = top2_sum(*rows[0:GROUP_SIZE])
    g_sel = jnp.zeros(best.shape, jnp.int32)
    for gi in range(1, N_GROUPS):
        gs = top2_sum(*rows[gi * GROUP_SIZE:(gi + 1) * GROUP_SIZE])
        better = gs > best
        best = jnp.where(better, gs, best)
        g_sel = jnp.where(better, gi, g_sel)

    def pick(src, j):
        out = src[j]
        for gi in range(1, N_GROUPS):
            out = jnp.where(g_sel == gi, src[gi * GROUP_SIZE + j], out)
        return out

    in_b = [pick(rows, j) for j in range(GROUP_SIZE)]
    in_s = [pick(srow, j) for j in range(GROUP_SIZE)]

    def arg_first_max(vals):
        best_v = vals[0]
        best_i = jnp.zeros(best_v.shape, jnp.int32)
        for j in range(1, len(vals)):
            better = vals[j] > best_v
            best_v = jnp.where(better, vals[j], best_v)
            best_i = jnp.where(better, j, best_i)
        return best_i

    i1 = arg_first_max(in_b)
    i2 = arg_first_max([jnp.where(i1 == j, -jnp.inf, in_b[j]) for j in range(GROUP_SIZE)])

    def take(src, idx):
        out = src[0]
        for j in range(1, GROUP_SIZE):
            out = jnp.where(idx == j, src[j], out)
        return out

    g1, g2 = take(in_s, i1), take(in_s, i2)
    den = g1 + g2
    j_idx = lax.broadcasted_iota(jnp.int32, (g4_o.shape[0], n_t), 0)
    g4_o[...] = jnp.where(j_idx == i1, g1 / den, jnp.where(j_idx == i2, g2 / den, 0.0))

    grp = lax.broadcasted_iota(jnp.int32, (n_e, n_t), 0)
    onehot = (grp == g_sel).astype(F32)
    rank = _dot(onehot.astype(BF16), tri_ref[...])
    cnt = jnp.sum(onehot, axis=1, keepdims=True).astype(jnp.int32)
    blocks = jnp.zeros(cnt.shape, jnp.int32)
    for kb in range(-(-n_t // MOE_BLOCK)):
        blocks = blocks + (cnt > kb * MOE_BLOCK).astype(jnp.int32)
    used = jnp.zeros((1, 1), jnp.int32)
    pos = jnp.zeros((1, n_t), jnp.int32)
    lane = lax.broadcasted_iota(jnp.int32, (1, blk_o.shape[2]), 1)
    blk_grp = jnp.zeros(lane.shape, jnp.int32)
    for gi in range(N_GROUPS):
        pos = pos + jnp.where(g_sel == gi, used * MOE_BLOCK + rank[gi:gi + 1].astype(jnp.int32), 0)
        used = used + blocks[gi:gi + 1]
        if gi < N_GROUPS - 1:
            blk_grp = blk_grp + (used <= lane).astype(jnp.int32)
    pos_o[...] = pos
    n_blk = _moe_blocks(n_t)
    blk_o[0] = jnp.where(lane < n_blk, blk_grp, jnp.where(lane == n_blk, used, 0))


def _moe_blocks(tile):
    return -(-(tile + N_GROUPS * (MOE_BLOCK - 1)) // MOE_BLOCK)


def _route(logits_t, bias_b, tri):
    n_e, n = logits_t.shape
    rt = MOE_TILE
    sds = jax.ShapeDtypeStruct
    return pl.pallas_call(
        _route_kernel,
        grid=(n // rt,),
        in_specs=[pl.BlockSpec((n_e, rt), lambda i: (0, i)), _const_spec(bias_b.shape), _const_spec(tri.shape)],
        out_specs=[pl.BlockSpec((8, rt), lambda i: (0, i)), pl.BlockSpec((1, rt), lambda i: (0, i)),
                   pl.BlockSpec((1, 1, 128), lambda i: (i, 0, 0))],
        out_shape=[sds((8, n), F32), sds((1, n), jnp.int32), sds((n // rt, 1, 128), jnp.int32)],
        compiler_params=pltpu.CompilerParams(dimension_semantics=("arbitrary",)),
        name="route",
    )(logits_t, bias_b, tri)


def _moe_kernel(blk_ref, h_ref, posr_ref, posc_ref, ghi_ref, glo_ref, x1_ref, mod_ref, w1_ref, w3_ref, w2_ref,
                fg_ref, o_ref, xs_ref, ys_ref, *, final_norm, row0, tiles_per_seq):
    i = pl.program_id(0)
    tm, d = x1_ref.shape
    r_rows = xs_ref.shape[0]
    n_blk = r_rows // MOE_BLOCK
    sel = jnp.where(lax.broadcasted_iota(jnp.int32, (r_rows, tm), 0) == posr_ref[...], 1.0, 0.0).astype(BF16)
    xs_ref[...] = _dot(sel, h_ref[...]).astype(BF16)
    gs = _dot(sel, ghi_ref[...]) + _dot(sel, glo_ref[...])
    base = i * BLK_STRIDE
    n_used = blk_ref[base + n_blk]
    for b in range(n_blk):
        rows = slice(b * MOE_BLOCK, (b + 1) * MOE_BLOCK)

        @pl.when(b < n_used)
        def _():
            first = blk_ref[base + b] * GROUP_SIZE
            xb = xs_ref[rows]
            acc = jnp.zeros((MOE_BLOCK, d), F32)
            for j in range(GROUP_SIZE):
                a = _dot(xb, w1_ref[first + j])
                g = _dot(xb, w3_ref[first + j])
                hid = (a * _sigmoid(a)) * g
                acc = acc + gs[rows, j:j + 1] * _dot(hid.astype(BF16), w2_ref[first + j])
            ys_ref[rows] = acc.astype(BF16)

        @pl.when(b >= n_used)
        def _():
            ys_ref[rows] = jnp.zeros((MOE_BLOCK, d), BF16)

    selt = jnp.where(lax.broadcasted_iota(jnp.int32, (tm, r_rows), 1) == posc_ref[...], 1.0, 0.0).astype(BF16)
    row = row0 + i // tiles_per_seq
    gate2 = mod_ref[pl.ds(row, 1), 5 * d:6 * d]
    x2 = x1_ref[...] + gate2 * _dot(selt, ys_ref[...])
    if final_norm:
        ms = jnp.mean(x2 * x2, axis=-1, keepdims=True)
        x2 = x2 * lax.rsqrt(ms + NORM_EPS) * fg_ref[...]
    o_ref[...] = x2


def _moe(rows, blk, h2, pos_row, pos_col, g_hi, g_lo, x1, mod, w1, w3, w2, final_g, final_norm):
    n, d = x1.shape
    tm = MOE_TILE
    r_rows = _moe_blocks(tm) * MOE_BLOCK
    tok = lambda w: pl.BlockSpec((tm, w), lambda i, *_: (i, 0))
    resident = lambda a: pl.BlockSpec(a.shape, lambda i, *_: (0,) * a.ndim, pipeline_mode=pl.Buffered(1))
    grid_spec = pltpu.PrefetchScalarGridSpec(
        num_scalar_prefetch=1,
        grid=(n // tm,),
        in_specs=[tok(d), pl.BlockSpec((1, tm), lambda i, *_: (0, i)), tok(1), tok(g_hi.shape[1]),
                  tok(g_lo.shape[1]), tok(d), _const_spec(mod.shape), resident(w1), resident(w3), resident(w2),
                  _const_spec(final_g.shape)],
        out_specs=tok(d),
        scratch_shapes=[pltpu.VMEM((r_rows, d), BF16), pltpu.VMEM((r_rows, d), BF16)],
    )
    return pl.pallas_call(
        functools.partial(_moe_kernel, final_norm=final_norm, row0=rows[0],
                          tiles_per_seq=max(rows[1] // tm, 1)),
        grid_spec=grid_spec,
        out_shape=jax.ShapeDtypeStruct((n, d), F32),
        compiler_params=pltpu.CompilerParams(
            dimension_semantics=("arbitrary",), vmem_limit_bytes=VMEM_LIMIT),
        name="moe",
    )(blk, h2, pos_row, pos_col, g_hi, g_lo, x1, mod, w1, w3, w2, final_g)


def kernel(x_prompt, x_sample, state_rwkv, c, c_ctx, norm1_g, norm2_g, final_g, ada_w, ada_b, w_in,
           conv_w, rwkv_w0, rwkv_w2, rwkv_a0, rwkv_a2, rwkv_g2, rwkv_k_k, rwkv_k_a, rwkv_r_k,
           rwkv_lnx_w, rwkv_lnx_b, sgu_ln_g, sgu_ln_b, sgu_ws, sgu_bs, proj_conv, proj_rwkv, proj_sgu,
           w_out, router_w, router_b, exp_w1, exp_w3, exp_w2):
    n_ctx, t_ctx, d = x_prompt.shape
    n_lat, t_lat, _ = x_sample.shape
    n_layer = w_in.shape[0]
    _, n_head, hs = rwkv_r_k.shape
    c_rw = n_head * hs
    conv_c = conv_w.shape[2]
    sgu_c = sgu_ln_g.shape[1]
    n_grp, chunk, _ = sgu_ws.shape[1:]
    lora_w = rwkv_w2.shape[2]
    lora_a = rwkv_a2.shape[2]
    lora_g = rwkv_g2.shape[1]
    n_exp = router_w.shape[1]
    assert 2 * lora_w == 128 and 2 * lora_a == 128 and lora_g == 128
    assert TOKEN_TILE % t_ctx == 0 and t_lat % TOKEN_TILE == 0 and TOKEN_TILE % GRID_W == 0
    assert t_lat % MOE_TILE == 0 and (n_ctx * t_ctx) % MOE_TILE == 0 and TOKEN_TILE % chunk == 0
    assert n_exp == N_GROUPS * GROUP_SIZE and c_rw % (SCAN_HEADS * hs) == 0
    assert SCAN_CHUNK == hs and t_ctx % SCAN_CHUNK == 0 and t_lat % SCAN_CHUNK == 0
    assert n_ctx % SCAN_PAR == 0 and n_lat % SCAN_PAR == 0
    assert _moe_blocks(MOE_TILE) < BLK_STRIDE and MOE_BLOCK % 16 == 0

    mod_rows = 8 * ((1 + n_lat + 7) // 8)
    cvec = jnp.zeros((mod_rows, d), F32).at[0].set(c_ctx).at[1:1 + n_lat].set(c)
    mod = _modulation(cvec, ada_w, ada_b)

    sizes = (conv_c, conv_c, conv_c, c_rw, c_rw, c_rw, 2 * lora_w, 2 * lora_a, lora_g, sgu_c, sgu_c, 3 * d)
    offs = np.concatenate([[0], np.cumsum(sizes)])
    assert offs[-1] == w_in.shape[2]
    rw_lo, rw_hi = int(offs[3]), int(offs[9])
    seg = jnp.asarray(np.kron(np.eye(n_head), np.ones((hs, hs))), BF16)
    n_str = c_rw // (SCAN_HEADS * hs)

    def side_by_side(s):
        s = s.reshape(s.shape[0], 2, n_str, SCAN_HEADS, hs, hs)
        return jnp.swapaxes(s, 3, 4).reshape(s.shape[0], 2, n_str, hs, SCAN_HEADS * hs)

    def from_side_by_side(s):
        s = s.reshape(s.shape[0], 2, n_str, hs, SCAN_HEADS, hs)
        return jnp.swapaxes(s, 3, 4).reshape(s.shape[0], 2, n_head, hs, hs)

    rw_t = router_w.T
    rw_hi_b = rw_t.astype(BF16)
    rw_lo_b = (rw_t - rw_hi_b.astype(F32)).astype(BF16)
    bias_b = jnp.broadcast_to(router_b.astype(F32)[:, None], (n_exp, MOE_TILE))
    tri = jnp.asarray(np.triu(np.ones((MOE_TILE, MOE_TILE)), 1), BF16)
    final_g2 = final_g.reshape(1, d)
    scan_consts = _scan_constants(SCAN_CHUNK, hs)

    n_ctx_tok = n_ctx * t_ctx
    groups = [
        dict(x=x_prompt.reshape(n_ctx_tok, d), t=t_ctx, cols=t_ctx, rows=(0, n_ctx_tok), s0=None),
        dict(x=x_sample.reshape(n_lat * t_lat, d), t=t_lat, cols=GRID_W, rows=(1, t_lat), s0=state_rwkv),
    ]
    new_states = []
    for l in range(n_layer):
        mod_l = mod[l]
        g1 = norm1_g[l].reshape(1, d)
        win = w_in[l]
        win_r = win[:, rw_lo:rw_hi].astype(BF16)
        win_c = jnp.concatenate([win[:, :rw_lo], win[:, rw_hi:]], axis=1).astype(BF16)
        zero_w = jnp.zeros((lora_w, c_rw), F32)
        w2cat = jnp.concatenate([jnp.concatenate([rwkv_w2[l, 0], zero_w], axis=1),
                                 jnp.concatenate([zero_w, rwkv_w2[l, 1]], axis=1)], axis=0).astype(BF16)
        a2cat = jnp.concatenate([jnp.concatenate([rwkv_a2[l, 0], zero_w], axis=1),
                                 jnp.concatenate([zero_w, rwkv_a2[l, 1]], axis=1)], axis=0).astype(BF16)
        bs_full = jnp.repeat(sgu_bs[l].T, sgu_c // n_grp, axis=1)
        k_a = rwkv_k_a[l].reshape(1, c_rw)
        weights = dict(
            g2=rwkv_g2[l].astype(BF16), ws=sgu_ws[l].astype(BF16), pc=proj_conv[l].astype(BF16),
            ps=proj_sgu[l].astype(BF16), pr=proj_rwkv[l].astype(BF16), wo=w_out[l].astype(BF16),
            w1=exp_w1[l].astype(BF16), w3=exp_w3[l].astype(BF16), w2=exp_w2[l].astype(BF16))
        for grp in groups:
            x, rows = grp["x"], grp["rows"]
            r, k, v, kk, lw, aa, bonus, g = _prep(
                rows, x, mod_l, g1, win_r,
                rwkv_w0[l].reshape(1, 2 * c_rw), w2cat, rwkv_a0[l].reshape(1, 2 * c_rw), a2cat,
                weights["g2"], rwkv_k_k[l].reshape(1, c_rw), k_a, rwkv_r_k[l].reshape(1, c_rw), seg)
            part, grw = _branch(
                rows, grp["cols"], x, mod_l, g1, win_c, conv_w[l], sgu_ln_g[l].reshape(1, sgu_c),
                sgu_ln_b[l].reshape(1, sgu_c), weights["ws"], bs_full, weights["pc"], weights["ps"])
            n_seq = x.shape[0] // grp["t"]
            if grp["s0"] is None:
                s0 = jnp.zeros((n_seq, 2, n_str, hs, SCAN_HEADS * hs), F32)
            else:
                s0 = side_by_side(grp["s0"][:, l].astype(F32))
            yf, yb, s_fin = _scan(grp["t"], r, k, v, kk, lw, aa, k_a, s0, scan_consts)
            if grp["s0"] is None:
                new_states.append(from_side_by_side(s_fin))
            x1, h2, logits_t = _merge(
                rows, x, yf, yb, bonus, g, part, grw, mod_l, rwkv_lnx_w[l].reshape(1, c_rw),
                rwkv_lnx_b[l].reshape(1, c_rw), seg, weights["pr"], weights["wo"],
                norm2_g[l].reshape(1, d), rw_hi_b, rw_lo_b, hs)
            g4, pos, blk = _route(logits_t, bias_b, tri)
            g4 = jnp.pad(g4[:GROUP_SIZE].T, ((0, 0), (0, 128 - GROUP_SIZE)))
            g_hi = g4.astype(BF16)
            g_lo = (g4 - g_hi.astype(F32)).astype(BF16)
            grp["x"] = _moe(rows, blk[:, 0, :BLK_STRIDE].reshape(-1), h2, pos, pos.reshape(-1, 1), g_hi, g_lo,
                            x1, mod_l, weights["w1"], weights["w3"], weights["w2"], final_g2,
                            final_norm=(l == n_layer - 1))

    y_prompt = groups[0]["x"].reshape(n_ctx, t_ctx, d)
    y_sample = groups[1]["x"].reshape(n_lat, t_lat, d)
    new_state = jnp.stack(new_states, axis=1).astype(x_prompt.dtype)
    return (y_prompt, y_sample, new_state)
```

```python
import functools
import math

import jax
import jax.numpy as jnp
import numpy as np
from jax import lax
from jax.experimental import pallas as pl
from jax.experimental.pallas import tpu as pltpu

F32 = jnp.float32
BF16 = jnp.bfloat16

GRID_W = 64
N_GROUPS = 4
GROUP_SIZE = 4
NORM_EPS = 1e-6
GN_EPS = 64e-5
KK_EPS = 1e-12
DECAY_SCALE = math.exp(-0.5)

TOKEN_TILE = 512
MOE_TILE = 512
MOE_BLOCK = 144
BLK_STRIDE = 16
SCAN_CHUNK = 64
SCAN_HEADS = 4
SCAN_PAR = 2
SCAN_SUB = 2
SCAN_STAGGER = 4
MOD_COLS = 1536
VMEM_LIMIT = 56 * 1024 * 1024


def _dot(a, b):
    return jnp.dot(a, b, preferred_element_type=F32)


def _dot_nt(a, b):
    return lax.dot_general(a, b, (((1,), (1,)), ((), ())), preferred_element_type=F32)


def _split(x):
    hi = x.astype(BF16)
    lo = (x - hi.astype(F32)).astype(BF16)
    return hi, lo


def _seg_sum(x, seg):
    return _dot(x.astype(BF16), seg)


def _sigmoid(x):
    return 1.0 / (1.0 + jnp.exp(-x))


def _norm_mod(x, gain, shift, scale):
    ms = jnp.mean(x * x, axis=-1, keepdims=True)
    return (x * lax.rsqrt(ms + NORM_EPS) * gain) * (1.0 + scale) + shift


def _const_spec(shape):
    nd = len(shape)
    return pl.BlockSpec(shape, lambda *_: (0,) * nd)


def _mod_kernel(c_ref, w_ref, b_ref, o_ref):
    c = c_ref[...]
    s = c * _sigmoid(c)
    o_ref[0] = _dot(s.astype(BF16), w_ref[0].astype(BF16)) + b_ref[0]


def _modulation(cvec, ada_w, ada_b):
    n_layer, d, cols = ada_w.shape
    rows = cvec.shape[0]
    return pl.pallas_call(
        _mod_kernel,
        grid=(n_layer, cols // MOD_COLS),
        in_specs=[
            pl.BlockSpec((rows, d), lambda l, j: (0, 0)),
            pl.BlockSpec((1, d, MOD_COLS), lambda l, j: (l, 0, j)),
            pl.BlockSpec((1, 1, MOD_COLS), lambda l, j: (l, 0, j)),
        ],
        out_specs=pl.BlockSpec((1, rows, MOD_COLS), lambda l, j: (l, 0, j)),
        out_shape=jax.ShapeDtypeStruct((n_layer, rows, cols), F32),
        compiler_params=pltpu.CompilerParams(
            dimension_semantics=("arbitrary", "arbitrary"), vmem_limit_bytes=VMEM_LIMIT),
        name="mod",
    )(cvec, ada_w, ada_b.reshape(n_layer, 1, cols))


def _prep_kernel(x_ref, mod_ref, g1_ref, win_ref, w0_ref, w2_ref, a0_ref, a2_ref, g2_ref,
                 kk_ref, ka_ref, rk_ref, seg_ref,
                 r_o, k_o, v_o, kk_o, lw_o, aa_o, bonus_o, g_o, *, row0, tiles_per_seq):
    row = row0 + pl.program_id(0) // tiles_per_seq
    x = x_ref[...]
    d = x.shape[1]
    c = r_o.shape[1]
    shift = mod_ref[pl.ds(row, 1), 0:d]
    scale = mod_ref[pl.ds(row, 1), d:2 * d]
    h = _norm_mod(x, g1_ref[...], shift, scale)
    z = _dot(h.astype(BF16), win_ref[...])
    r = z[:, 0:c]
    k = z[:, c:2 * c]
    v = z[:, 2 * c:3 * c]
    lo = 3 * c
    w_lo = z[:, lo:lo + 128]
    a_lo = z[:, lo + 128:lo + 256]
    g_lo = z[:, lo + 256:lo + 384]
    w_logit = w0_ref[...] + _dot(jnp.tanh(w_lo).astype(BF16), w2_ref[...])
    lw = -DECAY_SCALE * _sigmoid(w_logit)
    aa = _sigmoid(a0_ref[...] + _dot(a_lo.astype(BF16), a2_ref[...]))
    g = _dot(_sigmoid(g_lo).astype(BF16), g2_ref[...])
    seg = seg_ref[...]
    kk0 = k * kk_ref[...]
    kk = kk0 / jnp.maximum(jnp.sqrt(_seg_sum(kk0 * kk0, seg)), KK_EPS)
    ka = ka_ref[...]
    k_dirs = k * (1.0 + (aa[:, 0:c] - 1.0) * ka) + k * (1.0 + (aa[:, c:2 * c] - 1.0) * ka)
    rk = _seg_sum(r * k_dirs * rk_ref[...], seg)
    r_o[...] = r
    k_o[...] = k
    v_o[...] = v
    kk_o[...] = kk
    lw_o[0] = lw[:, 0:c]
    lw_o[1] = lw[:, c:2 * c]
    aa_o[0] = aa[:, 0:c]
    aa_o[1] = aa[:, c:2 * c]
    bonus_o[...] = rk * v
    g_o[...] = g


def _prep(rows, x, mod, g1, win_r, w0, w2, a0, a2, g2, k_k, k_a, r_k, seg):
    n, d = x.shape
    c = seg.shape[0]
    tm = TOKEN_TILE
    tok = lambda w: pl.BlockSpec((tm, w), lambda i: (i, 0))
    tok2 = pl.BlockSpec((2, tm, c), lambda i: (0, i, 0))
    consts = [mod, g1, win_r, w0, w2, a0, a2, g2, k_k, k_a, r_k, seg]
    sds = jax.ShapeDtypeStruct
    return pl.pallas_call(
        functools.partial(_prep_kernel, row0=rows[0], tiles_per_seq=rows[1] // tm),
        grid=(n // tm,),
        in_specs=[tok(d)] + [_const_spec(a.shape) for a in consts],
        out_specs=[tok(c), tok(c), tok(c), tok(c), tok2, tok2, tok(c), tok(c)],
        out_shape=[sds((n, c), F32)] * 4 + [sds((2, n, c), F32)] * 2 + [sds((n, c), F32)] * 2,
        compiler_params=pltpu.CompilerParams(
            dimension_semantics=("arbitrary",), vmem_limit_bytes=VMEM_LIMIT),
        name="prep",
    )(x, *consts)


def _branch_kernel(x_ref, mod_ref, g1_ref, win_ref, cw_ref, lng_ref, lnb_ref,
                   ws_ref, bs_ref, pc_ref, ps_ref, part_o, grw_o, *, row0, tiles_per_seq, cols):
    row = row0 + pl.program_id(0) // tiles_per_seq
    x = x_ref[...]
    tm, d = x.shape
    cw_w = cw_ref.shape[1]
    sg_w = lng_ref.shape[1]
    chunk = ws_ref.shape[1]
    n_grp = ws_ref.shape[0]
    shift = mod_ref[pl.ds(row, 1), 0:d]
    scale = mod_ref[pl.ds(row, 1), d:2 * d]
    h = _norm_mod(x, g1_ref[...], shift, scale)
    z = _dot(h.astype(BF16), win_ref[...])
    cb = z[:, 0:cw_w]
    cc = z[:, cw_w:2 * cw_w]
    cx = z[:, 2 * cw_w:3 * cw_w]
    o = 3 * cw_w
    su = z[:, o:o + sg_w]
    sv = z[:, o + sg_w:o + 2 * sg_w]
    gates = _sigmoid(z[:, o + 2 * sg_w:])
    xc = cc * cx
    col = lax.broadcasted_iota(jnp.int32, xc.shape, 0) & (cols - 1)
    prev = jnp.where(col == 0, 0.0, pltpu.roll(xc, 1, 0))
    nxt = jnp.where(col == cols - 1, 0.0, pltpu.roll(xc, tm - 1, 0))
    cw = cw_ref[...]
    y_conv = cb * (cw[0:1] * prev + cw[1:2] * xc + cw[2:3] * nxt)
    mu = jnp.mean(sv, axis=-1, keepdims=True)
    dv = sv - mu
    var = jnp.mean(dv * dv, axis=-1, keepdims=True)
    vn = (dv * lax.rsqrt(var + NORM_EPS) * lng_ref[...] + lnb_ref[...]).astype(BF16)
    lane_grp = lax.broadcasted_iota(jnp.int32, (chunk, sg_w), 1) // (sg_w // n_grp)
    parts = []
    for ci in range(tm // chunk):
        vc = vn[ci * chunk:(ci + 1) * chunk]
        mixed = bs_ref[...]
        for gi in range(n_grp):
            mixed = mixed + jnp.where(lane_grp == gi, _dot(ws_ref[gi], vc), 0.0)
        parts.append(su[ci * chunk:(ci + 1) * chunk] * mixed)
    y_sgu = jnp.concatenate(parts, axis=0)
    part_o[...] = (gates[:, 0:d] * _dot(y_conv.astype(BF16), pc_ref[...])
                   + gates[:, 2 * d:3 * d] * _dot(y_sgu.astype(BF16), ps_ref[...]))
    grw_o[...] = gates[:, d:2 * d]


def _branch(rows, cols, x, mod, g1, win_c, conv_w, ln_g, ln_b, ws, bs_full, proj_conv, proj_sgu):
    n, d = x.shape
    tm = TOKEN_TILE
    tok = pl.BlockSpec((tm, d), lambda i: (i, 0))
    consts = [mod, g1, win_c, conv_w, ln_g, ln_b, ws, bs_full, proj_conv, proj_sgu]
    return pl.pallas_call(
        functools.partial(_branch_kernel, row0=rows[0], tiles_per_seq=rows[1] // tm, cols=cols),
        grid=(n // tm,),
        in_specs=[tok] + [_const_spec(a.shape) for a in consts],
        out_specs=[tok, tok],
        out_shape=[jax.ShapeDtypeStruct((n, d), F32)] * 2,
        compiler_params=pltpu.CompilerParams(
            dimension_semantics=("arbitrary",), vmem_limit_bytes=VMEM_LIMIT),
        name="branch",
    )(x, *consts)


LANE_TILE = 128


def _scan_constants(n_l, hs):
    lanes = SCAN_HEADS * hs
    t = np.arange(n_l)[:, None]
    s = np.arange(lanes)[None, :] % n_l
    ti, si = np.arange(n_l)[:, None], np.arange(n_l)[None, :]
    n_lvl = n_l.bit_length() - 1
    tri, masks, lvl = [], [], []
    for reverse in (False, True):
        if not reverse:
            level = lambda q: (((t >> q) & 1) == 1) & ((s >> q) == (t >> q) - 1)
            tri.append(ti >= si)
            masks.append(np.concatenate([t > s, t >= s, level(0)], axis=0))
        else:
            level = lambda q: (((t >> q) & 1) == 0) & ((s >> q) == (t >> q) + 1)
            tri.append(ti <= si)
            masks.append(np.concatenate([t < s, t <= s, level(0)], axis=0))
        lvl.append([level(q) for q in range(1, n_lvl)])
    as_j = lambda a, dt: jnp.asarray(np.asarray(a, np.float32), dt)
    return as_j(t == s, F32), as_j(masks, F32), as_j(lvl, BF16), as_j(tri, BF16)


class _HeadBlocks:
    def __init__(self, hs, lanes):
        self.hs, self.lanes = hs, lanes
        self.n_tiles = lanes // LANE_TILE
        self.per_tile = LANE_TILE // hs
        self.heads = lanes // hs
        lane_head = lax.broadcasted_iota(jnp.int32, (hs, LANE_TILE), 1) // hs
        self.half_f32 = [(lane_head == j).astype(F32) for j in range(self.per_tile)]
        self.half_bf16 = [m.astype(BF16) for m in self.half_f32]

    def expand(self, compact):
        zero = jnp.zeros((self.hs, LANE_TILE), compact.dtype)
        rows = []
        for h in range(self.heads):
            tiles = [zero] * self.n_tiles
            tiles[h // self.per_tile] = compact[h * self.hs:(h + 1) * self.hs]
            rows.append(jnp.concatenate(tiles, axis=1))
        return jnp.concatenate(rows, axis=0)

    def compact(self, x, masks):
        return jnp.concatenate(
            [x[:, (h // self.per_tile) * LANE_TILE:(h // self.per_tile + 1) * LANE_TILE] * masks[h % self.per_tile]
             for h in range(self.heads)], axis=0)

    def diag(self, full, masks):
        return jnp.concatenate(
            [full[h * self.hs:(h + 1) * self.hs,
                  (h // self.per_tile) * LANE_TILE:(h // self.per_tile + 1) * LANE_TILE] * masks[h % self.per_tile]
             for h in range(self.heads)], axis=0)

    def side_by_side(self, compact):
        tiles = []
        for t in range(self.n_tiles):
            acc = compact[t * self.per_tile * self.hs:(t * self.per_tile + 1) * self.hs]
            for j in range(1, self.per_tile):
                h = t * self.per_tile + j
                acc = acc + compact[h * self.hs:(h + 1) * self.hs]
            tiles.append(acc)
        return jnp.concatenate(tiles, axis=1)

    def bd(self, x):
        return self.expand(self.compact(x.astype(BF16), self.half_bf16))


def _scan_prepare(streams, hb, eye, masks_ref, lvl_ref, tri_ref, out):
    n_l = streams[0][0].shape[0]
    ns = range(len(streams))
    bd = hb.bd
    r, k, v, kk, lw, aa, ka, di = map(list, zip(*streams))
    tri = lambda i: tri_ref[di[i]]
    tril_mask = lambda i: masks_ref[di[i], 0:2 * n_l]
    end = [0 if d else n_l - 1 for d in di]
    l_hi = [lw[i].astype(BF16) for i in ns]
    l_mid = [(lw[i] - l_hi[i].astype(F32)).astype(BF16) for i in ns]
    l_lo = [(lw[i] - l_hi[i].astype(F32) - l_mid[i].astype(F32)).astype(BF16) for i in ns]
    c = [_dot(tri(i), l_hi[i]) + _dot(tri(i), l_mid[i]) + _dot(tri(i), l_lo[i]) for i in ns]
    yield
    b_vec = [kk[i] * aa[i] for i in ns]
    kd = [k[i] * (1.0 + (aa[i] - 1.0) * ka[i]) for i in ns]
    c_end = [c[i][end[i]:end[i] + 1] for i in ns]
    e_neg = [jnp.exp(-c[i]) for i in ns]
    lhs = [jnp.concatenate([-kk[i] * jnp.exp(c[i] - lw[i]), r[i] * jnp.exp(c[i])], axis=0).astype(BF16)
           for i in ns]
    yield
    m_b = [_dot_nt(lhs[i], bd(b_vec[i] * e_neg[i])) * tril_mask(i) for i in ns]
    yield
    m_k = [(_dot_nt(lhs[i], bd(kd[i] * e_neg[i])) * tril_mask(i)).astype(BF16) for i in ns]
    yield
    a_ab = [m_b[i][0:n_l] for i in ns]
    t_inv = [eye + a_ab[i] * masks_ref[di[i], 2 * n_l:3 * n_l] for i in ns]
    a_bf = [a_ab[i].astype(BF16) for i in ns]
    for q in range(lvl_ref.shape[1]):
        x1 = [_dot(t_inv[i].astype(BF16), bd(a_bf[i] * lvl_ref[di[i], q])) for i in ns]
        yield
        x2 = [_dot(x1[i].astype(BF16), bd(t_inv[i])) for i in ns]
        t_inv = [t_inv[i] + x2[i] for i in ns]
        yield
    v_k = [_dot(m_k[i], bd(v[i])) for i in ns]
    e_end = [jnp.exp(c_end[i] - c[i]) for i in ns]
    bk_t = [jnp.concatenate([b_vec[i] * e_end[i], kd[i] * e_end[i]], axis=0).T.astype(BF16) for i in ns]
    decay_col = [jnp.broadcast_to(jnp.exp(c_end[i]), (LANE_TILE, hb.lanes)).T for i in ns]
    out.extend((lhs[i], v_k[i], t_inv[i], m_b[i][n_l:].astype(BF16), v[i], bk_t[i], decay_col[i]) for i in ns)


def _scan_apply(pre, z, hb, out):
    lhs, v_k, t_inv, a_rb, v, bk_t, decay_col = map(list, zip(*pre))
    n_l = t_inv[0].shape[0]
    ns = range(len(pre))
    v_part = [_dot(lhs[i], hb.expand(z[i].astype(BF16))) + v_k[i] for i in ns]
    yield
    u = [_dot(t_inv[i].astype(BF16), hb.bd(v_part[i][0:n_l])) for i in ns]
    yield
    y = [v_part[i][n_l:] + _dot(a_rb[i], hb.bd(u[i])) for i in ns]
    yield
    uv = [jnp.concatenate([u[i], v[i]], axis=0).astype(BF16) for i in ns]
    z_new = [z[i] * decay_col[i] + hb.diag(_dot(bk_t[i], uv[i]), hb.half_f32) for i in ns]
    out.extend(zip(y, z_new))


def _scan_kernel(rf_ref, kf_ref, vf_ref, kkf_ref, lwf_ref, aaf_ref,
                 rb_ref, kb_ref, vb_ref, kkb_ref, lwb_ref, aab_ref,
                 ka_ref, s0_ref, eye_ref, masks_ref, lvl_ref, tri_ref,
                 yf_o, yb_o, sfin_o, z_scr):
    j = pl.program_id(1)
    n_par, n_rows, c = rf_ref.shape
    n_l = eye_ref.shape[0]
    lanes = z_scr.shape[1]
    n_str = c // lanes
    hb = _HeadBlocks(lanes // SCAN_HEADS, lanes)
    slots = [(p, di, hg) for p in range(n_par) for di in range(2) for hg in range(n_str)]

    @pl.when(j == 0)
    def _():
        for zi, (p, di, hg) in enumerate(slots):
            z_scr[zi] = hb.compact(s0_ref[p, 0, di, hg], hb.half_f32)

    ka = ka_ref[...]
    dirs = ((rf_ref, kf_ref, vf_ref, kkf_ref, lwf_ref, aaf_ref, yf_o),
            (rb_ref, kb_ref, vb_ref, kkb_ref, lwb_ref, aab_ref, yb_o))
    n_sub = n_rows // n_l
    streams = []
    for sub in range(n_sub):
        for p, di, hg in slots:
            r_ref, k_ref, v_ref, kk_ref, lw_ref, aa_ref, _ = dirs[di]
            rs = slice((n_sub - 1 - sub if di else sub) * n_l, (n_sub - sub if di else sub + 1) * n_l)
            ls = slice(hg * lanes, (hg + 1) * lanes)
            streams.append((r_ref[p, rs, ls], k_ref[p, rs, ls], v_ref[p, rs, ls], kk_ref[p, rs, ls],
                            lw_ref[0, p, rs, ls], aa_ref[0, p, rs, ls], ka[:, ls], di))
    n_slot = len(slots)
    eye = eye_ref[...]
    pre = [[] for _ in range(n_sub)]
    res = [[] for _ in range(n_sub)]
    z0 = [z_scr[zi] for zi in range(n_slot)]

    gens = [_scan_prepare(streams[s * n_slot:(s + 1) * n_slot], hb, eye, masks_ref, lvl_ref, tri_ref, pre[s])
            for s in range(n_sub)]
    for _ in range(SCAN_STAGGER):
        next(gens[0])
    for s in range(n_sub):
        nxt = gens[s + 1] if s + 1 < n_sub else None
        for _ in gens[s]:
            if nxt is not None:
                next(nxt, None)
        z_in = z0 if s == 0 else [o[1] for o in res[s - 1]]
        for _ in _scan_apply(pre[s], z_in, hb, res[s]):
            if nxt is not None:
                next(nxt, None)
        for (y, _), (p, di, hg) in zip(res[s], slots):
            rs = slice((n_sub - 1 - s if di else s) * n_l, (n_sub - s if di else s + 1) * n_l)
            dirs[di][6][p, rs, hg * lanes:(hg + 1) * lanes] = y
    z = [o[1] for o in res[n_sub - 1]]
    for zi in range(len(slots)):
        z_scr[zi] = z[zi]

    @pl.when(j == pl.num_programs(1) - 1)
    def _():
        for zi, (p, di, hg) in enumerate(slots):
            sfin_o[p, 0, di, hg] = hb.side_by_side(z_scr[zi])


def _scan(t_len, r, k, v, kk, lw, aa, k_a, s0, consts):
    n, c = r.shape
    n_seq = s0.shape[0]
    n_par = SCAN_PAR
    n_l = SCAN_CHUNK * SCAN_SUB
    n_c = t_len // n_l
    hs = s0.shape[3]
    lanes = SCAN_HEADS * hs
    n_str = c // lanes
    part = lambda a: a.reshape(n_par, n // n_par, c)
    part2 = lambda a: a.reshape(2, n_par, n // n_par, c)
    s0p = s0.reshape((n_par, n_seq // n_par) + s0.shape[1:])
    fwd = pl.BlockSpec((n_par, n_l, c), lambda q, j: (0, q * n_c + j, 0))
    bwd = pl.BlockSpec((n_par, n_l, c), lambda q, j: (0, q * n_c + n_c - 1 - j, 0))
    fwd2 = pl.BlockSpec((1, n_par, n_l, c), lambda q, j: (0, 0, q * n_c + j, 0))
    bwd2 = pl.BlockSpec((1, n_par, n_l, c), lambda q, j: (1, 0, q * n_c + n_c - 1 - j, 0))
    st = pl.BlockSpec((n_par, 1, 2, n_str, hs, lanes), lambda q, j: (0, q, 0, 0, 0, 0))
    sds = jax.ShapeDtypeStruct
    args = [part(r), part(k), part(v), part(kk), part2(lw), part2(aa)]
    yf, yb, s_fin = pl.pallas_call(
        _scan_kernel,
        grid=(n_seq // n_par, n_c),
        in_specs=[fwd, fwd, fwd, fwd, fwd2, fwd2, bwd, bwd, bwd, bwd, bwd2, bwd2,
                  _const_spec(k_a.shape), st] + [_const_spec(a.shape) for a in consts],
        out_specs=[fwd, bwd, st],
        out_shape=[sds((n_par, n // n_par, c), F32), sds((n_par, n // n_par, c), F32), sds(s0p.shape, F32)],
        scratch_shapes=[pltpu.VMEM((n_par * 2 * n_str, lanes, LANE_TILE), F32)],
        compiler_params=pltpu.CompilerParams(
            dimension_semantics=("arbitrary", "arbitrary"), vmem_limit_bytes=VMEM_LIMIT),
        name="scan",
    )(*args, *args, k_a, s0p, *consts)
    return yf.reshape(n, c), yb.reshape(n, c), s_fin.reshape(s0.shape)


def _merge_kernel(x_ref, yf_ref, yb_ref, bonus_ref, g_ref, part_ref, grw_ref, mod_ref,
                  lnw_ref, lnb_ref, seg_ref, prw_ref, wout_ref, g2_ref, rwh_ref, rwl_ref,
                  x1_o, h2_o, lg_o, *, head_size, row0, tiles_per_seq):
    row = row0 + pl.program_id(0) // tiles_per_seq
    x = x_ref[...]
    d = x.shape[1]
    seg = seg_ref[...]
    y = yf_ref[...] + yb_ref[...]
    hs_inv = 1.0 / head_size
    mu = _seg_sum(y, seg) * hs_inv
    dy = y - mu
    var = _seg_sum(dy * dy, seg) * hs_inv
    yn = dy * lax.rsqrt(var + GN_EPS) * lnw_ref[...] + lnb_ref[...]
    y_rwkv = (yn + bonus_ref[...]) * g_ref[...]
    merged = part_ref[...] + grw_ref[...] * _dot(y_rwkv.astype(BF16), prw_ref[...])
    mix = _dot(merged.astype(BF16), wout_ref[...])
    gate1 = mod_ref[pl.ds(row, 1), 2 * d:3 * d]
    shift2 = mod_ref[pl.ds(row, 1), 3 * d:4 * d]
    scale2 = mod_ref[pl.ds(row, 1), 4 * d:5 * d]
    x1 = x + gate1 * mix
    h2 = _norm_mod(x1, g2_ref[...], shift2, scale2)
    h_hi, h_lo = _split(h2)
    rwh = rwh_ref[...]
    x1_o[...] = x1
    h2_o[...] = h_hi
    lg_o[...] = _dot_nt(rwh, h_hi) + _dot_nt(rwl_ref[...], h_hi) + _dot_nt(rwh, h_lo)


def _merge(rows, x, yf, yb, bonus, g, part, grw, mod, lnw, lnb, seg, proj_rwkv, w_out, g2n, rw_hi, rw_lo,
           head_size):
    n, d = x.shape
    c = seg.shape[0]
    n_e = rw_hi.shape[0]
    tm = TOKEN_TILE
    tok = lambda w: pl.BlockSpec((tm, w), lambda i: (i, 0))
    consts = [mod, lnw, lnb, seg, proj_rwkv, w_out, g2n, rw_hi, rw_lo]
    sds = jax.ShapeDtypeStruct
    return pl.pallas_call(
        functools.partial(_merge_kernel, head_size=head_size, row0=rows[0], tiles_per_seq=rows[1] // tm),
        grid=(n // tm,),
        in_specs=[tok(d), tok(c), tok(c), tok(c), tok(c), tok(d), tok(d)]
        + [_const_spec(a.shape) for a in consts],
        out_specs=[tok(d), tok(d), pl.BlockSpec((n_e, tm), lambda i: (0, i))],
        out_shape=[sds((n, d), F32), sds((n, d), BF16), sds((n_e, n), F32)],
        compiler_params=pltpu.CompilerParams(
            dimension_semantics=("arbitrary",), vmem_limit_bytes=VMEM_LIMIT),
        name="merge",
    )(x, yf, yb, bonus, g, part, grw, *consts)


def _route_kernel(lg_ref, b_ref, tri_ref, g4_o, pos_o, blk_o):
    scores = _sigmoid(lg_ref[...])
    biased = scores + b_ref[...]
    n_e, n_t = scores.shape
    rows = [biased[e:e + 1] for e in range(n_e)]
    srow = [scores[e:e + 1] for e in range(n_e)]

    def top2_sum(a, b, c, d):
        s1, t1 = jnp.maximum(a, b), jnp.minimum(a, b)
        s2, t2 = jnp.maximum(c, d), jnp.minimum(c, d)
        return jnp.maximum(s1, s2) + jnp.maximum(jnp.minimum(s1, s2), jnp.maximum(t1, t2))

    best = top2_sum(*rows[0:GROUP_SIZE])
    g_sel = jnp.zeros(best.shape, jnp.int32)
    for gi in range(1, N_GROUPS):
        gs = top2_sum(*rows[gi * GROUP_SIZE:(gi + 1) * GROUP_SIZE])
        better = gs > best
        best = jnp.where(better, gs, best)
        g_sel = jnp.where(better, gi, g_sel)

    def pick(src, j):
        out = src[j]
        for gi in range(1, N_GROUPS):
            out = jnp.where(g_sel == gi, src[gi * GROUP_SIZE + j], out)
        return out

    in_b = [pick(rows, j) for j in range(GROUP_SIZE)]
    in_s = [pick(srow, j) for j in range(GROUP_SIZE)]

    def arg_first_max(vals):
        best_v = vals[0]
        best_i = jnp.zeros(best_v.shape, jnp.int32)
        for j in range(1, len(vals)):
            better = vals[j] > best_v
            best_v = jnp.where(better, vals[j], best_v)
            best_i = jnp.where(better, j, best_i)
        return best_i

    i1 = arg_first_max(in_b)
    i2 = arg_first_max([jnp.where(i1 == j, -jnp.inf, in_b[j]) for j in range(GROUP_SIZE)])

    def take(src, idx):
        out = src[0]
        for j in range(1, GROUP_SIZE):
            out = jnp.where(idx == j, src[j], out)
        return out

    g1, g2 = take(in_s, i1), take(in_s, i2)
    den = g1 + g2
    j_idx = lax.broadcasted_iota(jnp.int32, (g4_o.shape[0], n_t), 0)
    g4_o[...] = jnp.where(j_idx == i1, g1 / den, jnp.where(j_idx == i2, g2 / den, 0.0))

    grp = lax.broadcasted_iota(jnp.int32, (n_e, n_t), 0)
    onehot = (grp == g_sel).astype(F32)
    rank = _dot(onehot.astype(BF16), tri_ref[...])
    cnt = jnp.sum(onehot, axis=1, keepdims=True).astype(jnp.int32)
    blocks = jnp.zeros(cnt.shape, jnp.int32)
    for kb in range(-(-n_t // MOE_BLOCK)):
        blocks = blocks + (cnt > kb * MOE_BLOCK).astype(jnp.int32)
    used = jnp.zeros((1, 1), jnp.int32)
    pos = jnp.zeros((1, n_t), jnp.int32)
    lane = lax.broadcasted_iota(jnp.int32, (1, blk_o.shape[2]), 1)
    blk_grp = jnp.zeros(lane.shape, jnp.int32)
    for gi in range(N_GROUPS):
        pos = pos + jnp.where(g_sel == gi, used * MOE_BLOCK + rank[gi:gi + 1].astype(jnp.int32), 0)
        used = used + blocks[gi:gi + 1]
        if gi < N_GROUPS - 1:
            blk_grp = blk_grp + (used <= lane).astype(jnp.int32)
    pos_o[...] = pos
    n_blk = _moe_blocks(n_t)
    blk_o[0] = jnp.where(lane < n_blk, blk_grp, jnp.where(lane == n_blk, used, 0))


def _moe_blocks(tile):
    return -(-(tile + N_GROUPS * (MOE_BLOCK - 1)) // MOE_BLOCK)


def _route(logits_t, bias_b, tri):
    n_e, n = logits_t.shape
    rt = MOE_TILE
    sds = jax.ShapeDtypeStruct
    return pl.pallas_call(
        _route_kernel,
        grid=(n // rt,),
        in_specs=[pl.BlockSpec((n_e, rt), lambda i: (0, i)), _const_spec(bias_b.shape), _const_spec(tri.shape)],
        out_specs=[pl.BlockSpec((8, rt), lambda i: (0, i)), pl.BlockSpec((1, rt), lambda i: (0, i)),
                   pl.BlockSpec((1, 1, 128), lambda i: (i, 0, 0))],
        out_shape=[sds((8, n), F32), sds((1, n), jnp.int32), sds((n // rt, 1, 128), jnp.int32)],
        compiler_params=pltpu.CompilerParams(dimension_semantics=("arbitrary",)),
        name="route",
    )(logits_t, bias_b, tri)


def _moe_kernel(blk_ref, h_ref, posr_ref, posc_ref, ghi_ref, glo_ref, x1_ref, mod_ref, w1_ref, w3_ref, w2_ref,
                fg_ref, o_ref, xs_ref, ys_ref, *, final_norm, row0, tiles_per_seq):
    i = pl.program_id(0)
    tm, d = x1_ref.shape
    r_rows = xs_ref.shape[0]
    n_blk = r_rows // MOE_BLOCK
    sel = jnp.where(lax.broadcasted_iota(jnp.int32, (r_rows, tm), 0) == posr_ref[...], 1.0, 0.0).astype(BF16)
    xs_ref[...] = _dot(sel, h_ref[...]).astype(BF16)
    gs = _dot(sel, ghi_ref[...]) + _dot(sel, glo_ref[...])
    base = i * BLK_STRIDE
    n_used = blk_ref[base + n_blk]
    for b in range(n_blk):
        rows = slice(b * MOE_BLOCK, (b + 1) * MOE_BLOCK)

        @pl.when(b < n_used)
        def _():
            first = blk_ref[base + b] * GROUP_SIZE
            xb = xs_ref[rows]
            acc = jnp.zeros((MOE_BLOCK, d), F32)
            for j in range(GROUP_SIZE):
                a = _dot(xb, w1_ref[first + j])
                g = _dot(xb, w3_ref[first + j])
                hid = (a * _sigmoid(a)) * g
                acc = acc + gs[rows, j:j + 1] * _dot(hid.astype(BF16), w2_ref[first + j])
            ys_ref[rows] = acc.astype(BF16)

        @pl.when(b >= n_used)
        def _():
            ys_ref[rows] = jnp.zeros((MOE_BLOCK, d), BF16)

    selt = jnp.where(lax.broadcasted_iota(jnp.int32, (tm, r_rows), 1) == posc_ref[...], 1.0, 0.0).astype(BF16)
    row = row0 + i // tiles_per_seq
    gate2 = mod_ref[pl.ds(row, 1), 5 * d:6 * d]
    x2 = x1_ref[...] + gate2 * _dot(selt, ys_ref[...])
    if final_norm:
        ms = jnp.mean(x2 * x2, axis=-1, keepdims=True)
        x2 = x2 * lax.rsqrt(ms + NORM_EPS) * fg_ref[...]
    o_ref[...] = x2


def _moe(rows, blk, h2, pos_row, pos_col, g_hi, g_lo, x1, mod, w1, w3, w2, final_g, final_norm):
    n, d = x1.shape
    tm = MOE_TILE
    r_rows = _moe_blocks(tm) * MOE_BLOCK
    tok = lambda w: pl.BlockSpec((tm, w), lambda i, *_: (i, 0))
    resident = lambda a: pl.BlockSpec(a.shape, lambda i, *_: (0,) * a.ndim, pipeline_mode=pl.Buffered(1))
    grid_spec = pltpu.PrefetchScalarGridSpec(
        num_scalar_prefetch=1,
        grid=(n // tm,),
        in_specs=[tok(d), pl.BlockSpec((1, tm), lambda i, *_: (0, i)), tok(1), tok(g_hi.shape[1]),
                  tok(g_lo.shape[1]), tok(d), _const_spec(mod.shape), resident(w1), resident(w3), resident(w2),
                  _const_spec(final_g.shape)],
        out_specs=tok(d),
        scratch_shapes=[pltpu.VMEM((r_rows, d), BF16), pltpu.VMEM((r_rows, d), BF16)],
    )
    return pl.pallas_call(
        functools.partial(_moe_kernel, final_norm=final_norm, row0=rows[0],
                          tiles_per_seq=max(rows[1] // tm, 1)),
        grid_spec=grid_spec,
        out_shape=jax.ShapeDtypeStruct((n, d), F32),
        compiler_params=pltpu.CompilerParams(
            dimension_semantics=("arbitrary",), vmem_limit_bytes=VMEM_LIMIT),
        name="moe",
    )(blk, h2, pos_row, pos_col, g_hi, g_lo, x1, mod, w1, w3, w2, final_g)


def kernel(x_prompt, x_sample, state_rwkv, c, c_ctx, norm1_g, norm2_g, final_g, ada_w, ada_b, w_in,
           conv_w, rwkv_w0, rwkv_w2, rwkv_a0, rwkv_a2, rwkv_g2, rwkv_k_k, rwkv_k_a, rwkv_r_k,
           rwkv_lnx_w, rwkv_lnx_b, sgu_ln_g, sgu_ln_b, sgu_ws, sgu_bs, proj_conv, proj_rwkv, proj_sgu,
           w_out, router_w, router_b, exp_w1, exp_w3, exp_w2):
    n_ctx, t_ctx, d = x_prompt.shape
    n_lat, t_lat, _ = x_sample.shape
    n_layer = w_in.shape[0]
    _, n_head, hs = rwkv_r_k.shape
    c_rw = n_head * hs
    conv_c = conv_w.shape[2]
    sgu_c = sgu_ln_g.shape[1]
    n_grp, chunk, _ = sgu_ws.shape[1:]
    lora_w = rwkv_w2.shape[2]
    lora_a = rwkv_a2.shape[2]
    lora_g = rwkv_g2.shape[1]
    n_exp = router_w.shape[1]
    assert 2 * lora_w == 128 and 2 * lora_a == 128 and lora_g == 128
    assert TOKEN_TILE % t_ctx == 0 and t_lat % TOKEN_TILE == 0 and TOKEN_TILE % GRID_W == 0
    assert t_lat % MOE_TILE == 0 and (n_ctx * t_ctx) % MOE_TILE == 0 and TOKEN_TILE % chunk == 0
    assert n_exp == N_GROUPS * GROUP_SIZE and c_rw % (SCAN_HEADS * hs) == 0
    assert SCAN_CHUNK == hs and t_ctx % (SCAN_CHUNK * SCAN_SUB) == 0 and t_lat % (SCAN_CHUNK * SCAN_SUB) == 0
    assert n_ctx % SCAN_PAR == 0 and n_lat % SCAN_PAR == 0
    assert _moe_blocks(MOE_TILE) < BLK_STRIDE and MOE_BLOCK % 16 == 0

    mod_rows = 8 * ((1 + n_lat + 7) // 8)
    cvec = jnp.zeros((mod_rows, d), F32).at[0].set(c_ctx).at[1:1 + n_lat].set(c)
    mod = _modulation(cvec, ada_w, ada_b)

    sizes = (conv_c, conv_c, conv_c, c_rw, c_rw, c_rw, 2 * lora_w, 2 * lora_a, lora_g, sgu_c, sgu_c, 3 * d)
    offs = np.concatenate([[0], np.cumsum(sizes)])
    assert offs[-1] == w_in.shape[2]
    rw_lo, rw_hi = int(offs[3]), int(offs[9])
    seg = jnp.asarray(np.kron(np.eye(n_head), np.ones((hs, hs))), BF16)
    n_str = c_rw // (SCAN_HEADS * hs)

    def side_by_side(s):
        s = jnp.swapaxes(s, -1, -2).reshape(s.shape[0], 2, n_str, SCAN_HEADS, hs, hs)
        return jnp.swapaxes(s, 3, 4).reshape(s.shape[0], 2, n_str, hs, SCAN_HEADS * hs)

    def from_side_by_side(s):
        s = s.reshape(s.shape[0], 2, n_str, hs, SCAN_HEADS, hs)
        return jnp.swapaxes(jnp.swapaxes(s, 3, 4).reshape(s.shape[0], 2, n_head, hs, hs), -1, -2)

    rw_t = router_w.T
    rw_hi_b = rw_t.astype(BF16)
    rw_lo_b = (rw_t - rw_hi_b.astype(F32)).astype(BF16)
    bias_b = jnp.broadcast_to(router_b.astype(F32)[:, None], (n_exp, MOE_TILE))
    tri = jnp.asarray(np.triu(np.ones((MOE_TILE, MOE_TILE)), 1), BF16)
    final_g2 = final_g.reshape(1, d)
    scan_consts = _scan_constants(SCAN_CHUNK, hs)

    n_ctx_tok = n_ctx * t_ctx
    groups = [
        dict(x=x_prompt.reshape(n_ctx_tok, d), t=t_ctx, cols=t_ctx, rows=(0, n_ctx_tok), s0=None),
        dict(x=x_sample.reshape(n_lat * t_lat, d), t=t_lat, cols=GRID_W, rows=(1, t_lat), s0=state_rwkv),
    ]
    new_states = []
    for l in range(n_layer):
        mod_l = mod[l]
        g1 = norm1_g[l].reshape(1, d)
        win = w_in[l]
        win_r = win[:, rw_lo:rw_hi].astype(BF16)
        win_c = jnp.concatenate([win[:, :rw_lo], win[:, rw_hi:]], axis=1).astype(BF16)
        zero_w = jnp.zeros((lora_w, c_rw), F32)
        w2cat = jnp.concatenate([jnp.concatenate([rwkv_w2[l, 0], zero_w], axis=1),
                                 jnp.concatenate([zero_w, rwkv_w2[l, 1]], axis=1)], axis=0).astype(BF16)
        a2cat = jnp.concatenate([jnp.concatenate([rwkv_a2[l, 0], zero_w], axis=1),
                                 jnp.concatenate([zero_w, rwkv_a2[l, 1]], axis=1)], axis=0).astype(BF16)
        bs_full = jnp.repeat(sgu_bs[l].T, sgu_c // n_grp, axis=1)
        k_a = rwkv_k_a[l].reshape(1, c_rw)
        weights = dict(
            g2=rwkv_g2[l].astype(BF16), ws=sgu_ws[l].astype(BF16), pc=proj_conv[l].astype(BF16),
            ps=proj_sgu[l].astype(BF16), pr=proj_rwkv[l].astype(BF16), wo=w_out[l].astype(BF16),
            w1=exp_w1[l].astype(BF16), w3=exp_w3[l].astype(BF16), w2=exp_w2[l].astype(BF16))
        for grp in groups:
            x, rows = grp["x"], grp["rows"]
            r, k, v, kk, lw, aa, bonus, g = _prep(
                rows, x, mod_l, g1, win_r,
                rwkv_w0[l].reshape(1, 2 * c_rw), w2cat, rwkv_a0[l].reshape(1, 2 * c_rw), a2cat,
                weights["g2"], rwkv_k_k[l].reshape(1, c_rw), k_a, rwkv_r_k[l].reshape(1, c_rw), seg)
            part, grw = _branch(
                rows, grp["cols"], x, mod_l, g1, win_c, conv_w[l], sgu_ln_g[l].reshape(1, sgu_c),
                sgu_ln_b[l].reshape(1, sgu_c), weights["ws"], bs_full, weights["pc"], weights["ps"])
            n_seq = x.shape[0] // grp["t"]
            if grp["s0"] is None:
                s0 = jnp.zeros((n_seq, 2, n_str, hs, SCAN_HEADS * hs), F32)
            else:
                s0 = side_by_side(grp["s0"][:, l].astype(F32))
            yf, yb, s_fin = _scan(grp["t"], r, k, v, kk, lw, aa, k_a, s0, scan_consts)
            if grp["s0"] is None:
                new_states.append(from_side_by_side(s_fin))
            x1, h2, logits_t = _merge(
                rows, x, yf, yb, bonus, g, part, grw, mod_l, rwkv_lnx_w[l].reshape(1, c_rw),
                rwkv_lnx_b[l].reshape(1, c_rw), seg, weights["pr"], weights["wo"],
                norm2_g[l].reshape(1, d), rw_hi_b, rw_lo_b, hs)
            g4, pos, blk = _route(logits_t, bias_b, tri)
            g4 = jnp.pad(g4[:GROUP_SIZE].T, ((0, 0), (0, 128 - GROUP_SIZE)))
            g_hi = g4.astype(BF16)
            g_lo = (g4 - g_hi.astype(F32)).astype(BF16)
            grp["x"] = _moe(rows, blk[:, 0, :BLK_STRIDE].reshape(-1), h2, pos, pos.reshape(-1, 1), g_hi, g_lo,
                            x1, mod_l, weights["w1"], weights["w3"], weights["w2"], final_g2,
                            final_norm=(l == n_layer - 1))

    y_prompt = groups[0]["x"].reshape(n_ctx, t_ctx, d)
    y_sample = groups[1]["x"].reshape(n_lat, t_lat, d)
    new_state = jnp.stack(new_states, axis=1).astype(x_prompt.dtype)
    return (y_prompt, y_sample, new_state)
```

```python
import functools
import math

import jax
import jax.numpy as jnp
import numpy as np
from jax import lax
from jax.experimental import pallas as pl
from jax.experimental.pallas import tpu as pltpu

F32 = jnp.float32
BF16 = jnp.bfloat16

GRID_W = 64
N_GROUPS = 4
GROUP_SIZE = 4
NORM_EPS = 1e-6
GN_EPS = 64e-5
KK_EPS = 1e-12
DECAY_SCALE = math.exp(-0.5)

TOKEN_TILE = 512
MOE_TILE = 512
MOE_BLOCK = 128
BLK_STRIDE = 16
SCAN_CHUNK = 64
SCAN_HEADS = 4
SCAN_PAR = 2
SCAN_SUB = 4
SCAN_STAGGER = 4
MOD_COLS = 1536
VMEM_LIMIT = 56 * 1024 * 1024


def _dot(a, b):
    return jnp.dot(a, b, preferred_element_type=F32)


def _dot_nt(a, b):
    return lax.dot_general(a, b, (((1,), (1,)), ((), ())), preferred_element_type=F32)


def _split(x):
    hi = x.astype(BF16)
    lo = (x - hi.astype(F32)).astype(BF16)
    return hi, lo


def _seg_sum(x, seg):
    return _dot(x.astype(BF16), seg)


def _sigmoid(x):
    return 1.0 / (1.0 + jnp.exp(-x))


def _norm_mod(x, gain, shift, scale):
    ms = jnp.mean(x * x, axis=-1, keepdims=True)
    return (x * lax.rsqrt(ms + NORM_EPS) * gain) * (1.0 + scale) + shift


def _const_spec(shape):
    nd = len(shape)
    return pl.BlockSpec(shape, lambda *_: (0,) * nd)


def _mod_kernel(c_ref, w_ref, b_ref, o_ref):
    c = c_ref[...]
    s = c * _sigmoid(c)
    o_ref[0] = _dot(s.astype(BF16), w_ref[0].astype(BF16)) + b_ref[0]


def _modulation(cvec, ada_w, ada_b):
    n_layer, d, cols = ada_w.shape
    rows = cvec.shape[0]
    return pl.pallas_call(
        _mod_kernel,
        grid=(n_layer, cols // MOD_COLS),
        in_specs=[
            pl.BlockSpec((rows, d), lambda l, j: (0, 0)),
            pl.BlockSpec((1, d, MOD_COLS), lambda l, j: (l, 0, j)),
            pl.BlockSpec((1, 1, MOD_COLS), lambda l, j: (l, 0, j)),
        ],
        out_specs=pl.BlockSpec((1, rows, MOD_COLS), lambda l, j: (l, 0, j)),
        out_shape=jax.ShapeDtypeStruct((n_layer, rows, cols), F32),
        compiler_params=pltpu.CompilerParams(
            dimension_semantics=("arbitrary", "arbitrary"), vmem_limit_bytes=VMEM_LIMIT),
        name="mod",
    )(cvec, ada_w, ada_b.reshape(n_layer, 1, cols))


def _prep_kernel(x_ref, mod_ref, g1_ref, win_ref, w0_ref, w2_ref, a0_ref, a2_ref, g2_ref,
                 kk_ref, ka_ref, rk_ref, seg_ref,
                 r_o, k_o, v_o, kk_o, lw_o, aa_o, bonus_o, g_o, *, row0, tiles_per_seq):
    row = row0 + pl.program_id(0) // tiles_per_seq
    x = x_ref[...]
    d = x.shape[1]
    c = r_o.shape[1]
    shift = mod_ref[pl.ds(row, 1), 0:d]
    scale = mod_ref[pl.ds(row, 1), d:2 * d]
    h = _norm_mod(x, g1_ref[...], shift, scale)
    z = _dot(h.astype(BF16), win_ref[...])
    r = z[:, 0:c]
    k = z[:, c:2 * c]
    v = z[:, 2 * c:3 * c]
    lo = 3 * c
    w_lo = z[:, lo:lo + 128]
    a_lo = z[:, lo + 128:lo + 256]
    g_lo = z[:, lo + 256:lo + 384]
    w_logit = w0_ref[...] + _dot(jnp.tanh(w_lo).astype(BF16), w2_ref[...])
    lw = -DECAY_SCALE * _sigmoid(w_logit)
    aa = _sigmoid(a0_ref[...] + _dot(a_lo.astype(BF16), a2_ref[...]))
    g = _dot(_sigmoid(g_lo).astype(BF16), g2_ref[...])
    seg = seg_ref[...]
    kk0 = k * kk_ref[...]
    kk = kk0 / jnp.maximum(jnp.sqrt(_seg_sum(kk0 * kk0, seg)), KK_EPS)
    ka = ka_ref[...]
    k_dirs = k * (1.0 + (aa[:, 0:c] - 1.0) * ka) + k * (1.0 + (aa[:, c:2 * c] - 1.0) * ka)
    rk = _seg_sum(r * k_dirs * rk_ref[...], seg)
    r_o[...] = r
    k_o[...] = k
    v_o[...] = v
    kk_o[...] = kk
    lw_o[0] = lw[:, 0:c]
    lw_o[1] = lw[:, c:2 * c]
    aa_o[0] = aa[:, 0:c]
    aa_o[1] = aa[:, c:2 * c]
    bonus_o[...] = rk * v
    g_o[...] = g


def _prep(rows, x, mod, g1, win_r, w0, w2, a0, a2, g2, k_k, k_a, r_k, seg):
    n, d = x.shape
    c = seg.shape[0]
    tm = TOKEN_TILE
    tok = lambda w: pl.BlockSpec((tm, w), lambda i: (i, 0))
    tok2 = pl.BlockSpec((2, tm, c), lambda i: (0, i, 0))
    consts = [mod, g1, win_r, w0, w2, a0, a2, g2, k_k, k_a, r_k, seg]
    sds = jax.ShapeDtypeStruct
    return pl.pallas_call(
        functools.partial(_prep_kernel, row0=rows[0], tiles_per_seq=rows[1] // tm),
        grid=(n // tm,),
        in_specs=[tok(d)] + [_const_spec(a.shape) for a in consts],
        out_specs=[tok(c), tok(c), tok(c), tok(c), tok2, tok2, tok(c), tok(c)],
        out_shape=[sds((n, c), F32)] * 4 + [sds((2, n, c), F32)] * 2 + [sds((n, c), F32)] * 2,
        compiler_params=pltpu.CompilerParams(
            dimension_semantics=("arbitrary",), vmem_limit_bytes=VMEM_LIMIT),
        name="prep",
    )(x, *consts)


def _branch_kernel(x_ref, mod_ref, g1_ref, win_ref, cw_ref, lng_ref, lnb_ref,
                   ws_ref, bs_ref, pc_ref, ps_ref, part_o, grw_o, *, row0, tiles_per_seq, cols):
    row = row0 + pl.program_id(0) // tiles_per_seq
    x = x_ref[...]
    tm, d = x.shape
    cw_w = cw_ref.shape[1]
    sg_w = lng_ref.shape[1]
    chunk = ws_ref.shape[1]
    n_grp = ws_ref.shape[0]
    shift = mod_ref[pl.ds(row, 1), 0:d]
    scale = mod_ref[pl.ds(row, 1), d:2 * d]
    h = _norm_mod(x, g1_ref[...], shift, scale)
    z = _dot(h.astype(BF16), win_ref[...])
    cb = z[:, 0:cw_w]
    cc = z[:, cw_w:2 * cw_w]
    cx = z[:, 2 * cw_w:3 * cw_w]
    o = 3 * cw_w
    su = z[:, o:o + sg_w]
    sv = z[:, o + sg_w:o + 2 * sg_w]
    gates = _sigmoid(z[:, o + 2 * sg_w:])
    xc = cc * cx
    col = lax.broadcasted_iota(jnp.int32, xc.shape, 0) & (cols - 1)
    prev = jnp.where(col == 0, 0.0, pltpu.roll(xc, 1, 0))
    nxt = jnp.where(col == cols - 1, 0.0, pltpu.roll(xc, tm - 1, 0))
    cw = cw_ref[...]
    y_conv = cb * (cw[0:1] * prev + cw[1:2] * xc + cw[2:3] * nxt)
    mu = jnp.mean(sv, axis=-1, keepdims=True)
    dv = sv - mu
    var = jnp.mean(dv * dv, axis=-1, keepdims=True)
    vn = (dv * lax.rsqrt(var + NORM_EPS) * lng_ref[...] + lnb_ref[...]).astype(BF16)
    lane_grp = lax.broadcasted_iota(jnp.int32, (chunk, sg_w), 1) // (sg_w // n_grp)
    parts = []
    for ci in range(tm // chunk):
        vc = vn[ci * chunk:(ci + 1) * chunk]
        mixed = bs_ref[...]
        for gi in range(n_grp):
            mixed = mixed + jnp.where(lane_grp == gi, _dot(ws_ref[gi], vc), 0.0)
        parts.append(su[ci * chunk:(ci + 1) * chunk] * mixed)
    y_sgu = jnp.concatenate(parts, axis=0)
    part_o[...] = (gates[:, 0:d] * _dot(y_conv.astype(BF16), pc_ref[...])
                   + gates[:, 2 * d:3 * d] * _dot(y_sgu.astype(BF16), ps_ref[...]))
    grw_o[...] = gates[:, d:2 * d]


def _branch(rows, cols, x, mod, g1, win_c, conv_w, ln_g, ln_b, ws, bs_full, proj_conv, proj_sgu):
    n, d = x.shape
    tm = TOKEN_TILE
    tok = pl.BlockSpec((tm, d), lambda i: (i, 0))
    consts = [mod, g1, win_c, conv_w, ln_g, ln_b, ws, bs_full, proj_conv, proj_sgu]
    return pl.pallas_call(
        functools.partial(_branch_kernel, row0=rows[0], tiles_per_seq=rows[1] // tm, cols=cols),
        grid=(n // tm,),
        in_specs=[tok] + [_const_spec(a.shape) for a in consts],
        out_specs=[tok, tok],
        out_shape=[jax.ShapeDtypeStruct((n, d), F32)] * 2,
        compiler_params=pltpu.CompilerParams(
            dimension_semantics=("arbitrary",), vmem_limit_bytes=VMEM_LIMIT),
        name="branch",
    )(x, *consts)


LANE_TILE = 128


def _scan_constants(n_l, hs):
    lanes = SCAN_HEADS * hs
    t = np.arange(n_l)[:, None]
    s = np.arange(lanes)[None, :] % n_l
    ti, si = np.arange(n_l)[:, None], np.arange(n_l)[None, :]
    n_lvl = n_l.bit_length() - 1
    tri, masks, lvl = [], [], []
    for reverse in (False, True):
        if not reverse:
            level = lambda q: (((t >> q) & 1) == 1) & ((s >> q) == (t >> q) - 1)
            tri.append(ti >= si)
            masks.append(np.concatenate([t > s, t >= s, level(0)], axis=0))
        else:
            level = lambda q: (((t >> q) & 1) == 0) & ((s >> q) == (t >> q) + 1)
            tri.append(ti <= si)
            masks.append(np.concatenate([t < s, t <= s, level(0)], axis=0))
        lvl.append([level(q) for q in range(1, n_lvl)])
    as_j = lambda a, dt: jnp.asarray(np.asarray(a, np.float32), dt)
    return as_j(t == s, F32), as_j(masks, F32), as_j(lvl, BF16), as_j(tri, BF16)


class _HeadBlocks:
    def __init__(self, hs, lanes):
        self.hs, self.lanes = hs, lanes
        self.n_tiles = lanes // LANE_TILE
        self.per_tile = LANE_TILE // hs
        self.heads = lanes // hs
        lane_head = lax.broadcasted_iota(jnp.int32, (hs, LANE_TILE), 1) // hs
        self.half_f32 = [(lane_head == j).astype(F32) for j in range(self.per_tile)]
        self.half_bf16 = [m.astype(BF16) for m in self.half_f32]

    def expand(self, compact):
        zero = jnp.zeros((self.hs, LANE_TILE), compact.dtype)
        rows = []
        for h in range(self.heads):
            tiles = [zero] * self.n_tiles
            tiles[h // self.per_tile] = compact[h * self.hs:(h + 1) * self.hs]
            rows.append(jnp.concatenate(tiles, axis=1))
        return jnp.concatenate(rows, axis=0)

    def compact(self, x, masks):
        return jnp.concatenate(
            [x[:, (h // self.per_tile) * LANE_TILE:(h // self.per_tile + 1) * LANE_TILE] * masks[h % self.per_tile]
             for h in range(self.heads)], axis=0)

    def diag(self, full, masks):
        return jnp.concatenate(
            [full[h * self.hs:(h + 1) * self.hs,
                  (h // self.per_tile) * LANE_TILE:(h // self.per_tile + 1) * LANE_TILE] * masks[h % self.per_tile]
             for h in range(self.heads)], axis=0)

    def side_by_side(self, compact):
        tiles = []
        for t in range(self.n_tiles):
            acc = compact[t * self.per_tile * self.hs:(t * self.per_tile + 1) * self.hs]
            for j in range(1, self.per_tile):
                h = t * self.per_tile + j
                acc = acc + compact[h * self.hs:(h + 1) * self.hs]
            tiles.append(acc)
        return jnp.concatenate(tiles, axis=1)

    def bd(self, x):
        return self.expand(self.compact(x.astype(BF16), self.half_bf16))


def _scan_prepare(streams, hb, eye, masks_ref, lvl_ref, tri_ref, out):
    n_l = streams[0][0].shape[0]
    ns = range(len(streams))
    bd = hb.bd
    r, k, v, kk, lw, aa, ka, di = map(list, zip(*streams))
    tri = lambda i: tri_ref[di[i]]
    tril_mask = lambda i: masks_ref[di[i], 0:2 * n_l]
    end = [0 if d else n_l - 1 for d in di]
    l_hi = [lw[i].astype(BF16) for i in ns]
    l_mid = [(lw[i] - l_hi[i].astype(F32)).astype(BF16) for i in ns]
    l_lo = [(lw[i] - l_hi[i].astype(F32) - l_mid[i].astype(F32)).astype(BF16) for i in ns]
    c = [_dot(tri(i), l_hi[i]) + _dot(tri(i), l_mid[i]) + _dot(tri(i), l_lo[i]) for i in ns]
    yield
    b_vec = [kk[i] * aa[i] for i in ns]
    kd = [k[i] * (1.0 + (aa[i] - 1.0) * ka[i]) for i in ns]
    c_end = [c[i][end[i]:end[i] + 1] for i in ns]
    e_neg = [jnp.exp(-c[i]) for i in ns]
    lhs = [jnp.concatenate([-kk[i] * jnp.exp(c[i] - lw[i]), r[i] * jnp.exp(c[i])], axis=0).astype(BF16)
           for i in ns]
    yield
    m_b = [_dot_nt(lhs[i], bd(b_vec[i] * e_neg[i])) * tril_mask(i) for i in ns]
    yield
    m_k = [(_dot_nt(lhs[i], bd(kd[i] * e_neg[i])) * tril_mask(i)).astype(BF16) for i in ns]
    yield
    a_ab = [m_b[i][0:n_l] for i in ns]
    t_inv = [eye + a_ab[i] * masks_ref[di[i], 2 * n_l:3 * n_l] for i in ns]
    a_bf = [a_ab[i].astype(BF16) for i in ns]
    for q in range(lvl_ref.shape[1]):
        x1 = [_dot(t_inv[i].astype(BF16), bd(a_bf[i] * lvl_ref[di[i], q])) for i in ns]
        yield
        x2 = [_dot(x1[i].astype(BF16), bd(t_inv[i])) for i in ns]
        t_inv = [t_inv[i] + x2[i] for i in ns]
        yield
    v_k = [_dot(m_k[i], bd(v[i])) for i in ns]
    e_end = [jnp.exp(c_end[i] - c[i]) for i in ns]
    bk_t = [jnp.concatenate([b_vec[i] * e_end[i], kd[i] * e_end[i]], axis=0).T.astype(BF16) for i in ns]
    decay_col = [jnp.broadcast_to(jnp.exp(c_end[i]), (LANE_TILE, hb.lanes)).T for i in ns]
    out.extend((lhs[i], v_k[i], t_inv[i], m_b[i][n_l:].astype(BF16), v[i], bk_t[i], decay_col[i]) for i in ns)


def _scan_apply(pre, z, hb, out):
    lhs, v_k, t_inv, a_rb, v, bk_t, decay_col = map(list, zip(*pre))
    n_l = t_inv[0].shape[0]
    ns = range(len(pre))
    v_part = [_dot(lhs[i], hb.expand(z[i].astype(BF16))) + v_k[i] for i in ns]
    yield
    u = [_dot(t_inv[i].astype(BF16), hb.bd(v_part[i][0:n_l])) for i in ns]
    yield
    y = [v_part[i][n_l:] + _dot(a_rb[i], hb.bd(u[i])) for i in ns]
    yield
    uv = [jnp.concatenate([u[i], v[i]], axis=0).astype(BF16) for i in ns]
    z_new = [z[i] * decay_col[i] + hb.diag(_dot(bk_t[i], uv[i]), hb.half_f32) for i in ns]
    out.extend(zip(y, z_new))


def _scan_kernel(rf_ref, kf_ref, vf_ref, kkf_ref, lwf_ref, aaf_ref,
                 rb_ref, kb_ref, vb_ref, kkb_ref, lwb_ref, aab_ref,
                 ka_ref, s0_ref, eye_ref, masks_ref, lvl_ref, tri_ref,
                 yf_o, yb_o, sfin_o, z_scr):
    j = pl.program_id(1)
    n_par, n_rows, c = rf_ref.shape
    n_l = eye_ref.shape[0]
    lanes = z_scr.shape[1]
    n_str = c // lanes
    hb = _HeadBlocks(lanes // SCAN_HEADS, lanes)
    slots = [(p, di, hg) for p in range(n_par) for di in range(2) for hg in range(n_str)]

    @pl.when(j == 0)
    def _():
        for zi, (p, di, hg) in enumerate(slots):
            z_scr[zi] = hb.compact(s0_ref[p, 0, di, hg], hb.half_f32)

    ka = ka_ref[...]
    dirs = ((rf_ref, kf_ref, vf_ref, kkf_ref, lwf_ref, aaf_ref, yf_o),
            (rb_ref, kb_ref, vb_ref, kkb_ref, lwb_ref, aab_ref, yb_o))
    n_sub = n_rows // n_l
    streams = []
    for sub in range(n_sub):
        for p, di, hg in slots:
            r_ref, k_ref, v_ref, kk_ref, lw_ref, aa_ref, _ = dirs[di]
            rs = slice((n_sub - 1 - sub if di else sub) * n_l, (n_sub - sub if di else sub + 1) * n_l)
            ls = slice(hg * lanes, (hg + 1) * lanes)
            streams.append((r_ref[p, rs, ls], k_ref[p, rs, ls], v_ref[p, rs, ls], kk_ref[p, rs, ls],
                            lw_ref[0, p, rs, ls], aa_ref[0, p, rs, ls], ka[:, ls], di))
    n_slot = len(slots)
    eye = eye_ref[...]
    pre = [[] for _ in range(n_sub)]
    res = [[] for _ in range(n_sub)]
    z0 = [z_scr[zi] for zi in range(n_slot)]

    gens = [_scan_prepare(streams[s * n_slot:(s + 1) * n_slot], hb, eye, masks_ref, lvl_ref, tri_ref, pre[s])
            for s in range(n_sub)]
    for _ in range(SCAN_STAGGER):
        next(gens[0])
    for s in range(n_sub):
        nxt = gens[s + 1] if s + 1 < n_sub else None
        for _ in gens[s]:
            if nxt is not None:
                next(nxt, None)
        z_in = z0 if s == 0 else [o[1] for o in res[s - 1]]
        for _ in _scan_apply(pre[s], z_in, hb, res[s]):
            if nxt is not None:
                next(nxt, None)
        for (y, _), (p, di, hg) in zip(res[s], slots):
            rs = slice((n_sub - 1 - s if di else s) * n_l, (n_sub - s if di else s + 1) * n_l)
            dirs[di][6][p, rs, hg * lanes:(hg + 1) * lanes] = y
    z = [o[1] for o in res[n_sub - 1]]
    for zi in range(len(slots)):
        z_scr[zi] = z[zi]

    @pl.when(j == pl.num_programs(1) - 1)
    def _():
        for zi, (p, di, hg) in enumerate(slots):
            sfin_o[p, 0, di, hg] = hb.side_by_side(z_scr[zi])


def _scan(t_len, r, k, v, kk, lw, aa, k_a, s0, consts):
    n, c = r.shape
    n_seq = s0.shape[0]
    n_par = SCAN_PAR
    n_l = SCAN_CHUNK * SCAN_SUB
    n_c = t_len // n_l
    hs = s0.shape[3]
    lanes = SCAN_HEADS * hs
    n_str = c // lanes
    part = lambda a: a.reshape(n_par, n // n_par, c)
    part2 = lambda a: a.reshape(2, n_par, n // n_par, c)
    s0p = s0.reshape((n_par, n_seq // n_par) + s0.shape[1:])
    fwd = pl.BlockSpec((n_par, n_l, c), lambda q, j: (0, q * n_c + j, 0))
    bwd = pl.BlockSpec((n_par, n_l, c), lambda q, j: (0, q * n_c + n_c - 1 - j, 0))
    fwd2 = pl.BlockSpec((1, n_par, n_l, c), lambda q, j: (0, 0, q * n_c + j, 0))
    bwd2 = pl.BlockSpec((1, n_par, n_l, c), lambda q, j: (1, 0, q * n_c + n_c - 1 - j, 0))
    st = pl.BlockSpec((n_par, 1, 2, n_str, hs, lanes), lambda q, j: (0, q, 0, 0, 0, 0))
    sds = jax.ShapeDtypeStruct
    args = [part(r), part(k), part(v), part(kk), part2(lw), part2(aa)]
    yf, yb, s_fin = pl.pallas_call(
        _scan_kernel,
        grid=(n_seq // n_par, n_c),
        in_specs=[fwd, fwd, fwd, fwd, fwd2, fwd2, bwd, bwd, bwd, bwd, bwd2, bwd2,
                  _const_spec(k_a.shape), st] + [_const_spec(a.shape) for a in consts],
        out_specs=[fwd, bwd, st],
        out_shape=[sds((n_par, n // n_par, c), F32), sds((n_par, n // n_par, c), F32), sds(s0p.shape, F32)],
        scratch_shapes=[pltpu.VMEM((n_par * 2 * n_str, lanes, LANE_TILE), F32)],
        compiler_params=pltpu.CompilerParams(
            dimension_semantics=("arbitrary", "arbitrary"), vmem_limit_bytes=VMEM_LIMIT),
        name="scan",
    )(*args, *args, k_a, s0p, *consts)
    return yf.reshape(n, c), yb.reshape(n, c), s_fin.reshape(s0.shape)


def _merge_kernel(x_ref, yf_ref, yb_ref, bonus_ref, g_ref, part_ref, grw_ref, mod_ref,
                  lnw_ref, lnb_ref, seg_ref, prw_ref, wout_ref, g2_ref, rwh_ref, rwl_ref,
                  x1_o, h2_o, lg_o, *, head_size, row0, tiles_per_seq):
    row = row0 + pl.program_id(0) // tiles_per_seq
    x = x_ref[...]
    d = x.shape[1]
    seg = seg_ref[...]
    y = yf_ref[...] + yb_ref[...]
    hs_inv = 1.0 / head_size
    mu = _seg_sum(y, seg) * hs_inv
    dy = y - mu
    var = _seg_sum(dy * dy, seg) * hs_inv
    yn = dy * lax.rsqrt(var + GN_EPS) * lnw_ref[...] + lnb_ref[...]
    y_rwkv = (yn + bonus_ref[...]) * g_ref[...]
    merged = part_ref[...] + grw_ref[...] * _dot(y_rwkv.astype(BF16), prw_ref[...])
    mix = _dot(merged.astype(BF16), wout_ref[...])
    gate1 = mod_ref[pl.ds(row, 1), 2 * d:3 * d]
    shift2 = mod_ref[pl.ds(row, 1), 3 * d:4 * d]
    scale2 = mod_ref[pl.ds(row, 1), 4 * d:5 * d]
    x1 = x + gate1 * mix
    h2 = _norm_mod(x1, g2_ref[...], shift2, scale2)
    h_hi, h_lo = _split(h2)
    rwh = rwh_ref[...]
    x1_o[...] = x1
    h2_o[...] = h_hi
    lg_o[...] = _dot_nt(rwh, h_hi) + _dot_nt(rwl_ref[...], h_hi) + _dot_nt(rwh, h_lo)


def _merge(rows, x, yf, yb, bonus, g, part, grw, mod, lnw, lnb, seg, proj_rwkv, w_out, g2n, rw_hi, rw_lo,
           head_size):
    n, d = x.shape
    c = seg.shape[0]
    n_e = rw_hi.shape[0]
    tm = TOKEN_TILE
    tok = lambda w: pl.BlockSpec((tm, w), lambda i: (i, 0))
    consts = [mod, lnw, lnb, seg, proj_rwkv, w_out, g2n, rw_hi, rw_lo]
    sds = jax.ShapeDtypeStruct
    return pl.pallas_call(
        functools.partial(_merge_kernel, head_size=head_size, row0=rows[0], tiles_per_seq=rows[1] // tm),
        grid=(n // tm,),
        in_specs=[tok(d), tok(c), tok(c), tok(c), tok(c), tok(d), tok(d)]
        + [_const_spec(a.shape) for a in consts],
        out_specs=[tok(d), tok(d), pl.BlockSpec((n_e, tm), lambda i: (0, i))],
        out_shape=[sds((n, d), F32), sds((n, d), BF16), sds((n_e, n), F32)],
        compiler_params=pltpu.CompilerParams(
            dimension_semantics=("arbitrary",), vmem_limit_bytes=VMEM_LIMIT),
        name="merge",
    )(x, yf, yb, bonus, g, part, grw, *consts)


def _route_kernel(lg_ref, b_ref, tri_ref, g4_o, pos_o, blk_o):
    scores = _sigmoid(lg_ref[...])
    biased = scores + b_ref[...]
    n_e, n_t = scores.shape
    rows = [biased[e:e + 1] for e in range(n_e)]
    srow = [scores[e:e + 1] for e in range(n_e)]

    def top2_sum(a, b, c, d):
        s1, t1 = jnp.maximum(a, b), jnp.minimum(a, b)
        s2, t2 = jnp.maximum(c, d), jnp.minimum(c, d)
        return jnp.maximum(s1, s2) + jnp.maximum(jnp.minimum(s1, s2), jnp.maximum(t1, t2))

    best = top2_sum(*rows[0:GROUP_SIZE])
    g_sel = jnp.zeros(best.shape, jnp.int32)
    for gi in range(1, N_GROUPS):
        gs = top2_sum(*rows[gi * GROUP_SIZE:(gi + 1) * GROUP_SIZE])
        better = gs > best
        best = jnp.where(better, gs, best)
        g_sel = jnp.where(better, gi, g_sel)

    def pick(src, j):
        out = src[j]
        for gi in range(1, N_GROUPS):
            out = jnp.where(g_sel == gi, src[gi * GROUP_SIZE + j], out)
        return out

    in_b = [pick(rows, j) for j in range(GROUP_SIZE)]
    in_s = [pick(srow, j) for j in range(GROUP_SIZE)]

    def arg_first_max(vals):
        best_v = vals[0]
        best_i = jnp.zeros(best_v.shape, jnp.int32)
        for j in range(1, len(vals)):
            better = vals[j] > best_v
            best_v = jnp.where(better, vals[j], best_v)
            best_i = jnp.where(better, j, best_i)
        return best_i

    i1 = arg_first_max(in_b)
    i2 = arg_first_max([jnp.where(i1 == j, -jnp.inf, in_b[j]) for j in range(GROUP_SIZE)])

    def take(src, idx):
        out = src[0]
        for j in range(1, GROUP_SIZE):
            out = jnp.where(idx == j, src[j], out)
        return out

    g1, g2 = take(in_s, i1), take(in_s, i2)
    den = g1 + g2
    j_idx = lax.broadcasted_iota(jnp.int32, (g4_o.shape[0], n_t), 0)
    g4_o[...] = jnp.where(j_idx == i1, g1 / den, jnp.where(j_idx == i2, g2 / den, 0.0))

    grp = lax.broadcasted_iota(jnp.int32, (n_e, n_t), 0)
    onehot = (grp == g_sel).astype(F32)
    rank = _dot(onehot.astype(BF16), tri_ref[...])
    cnt = jnp.sum(onehot, axis=1, keepdims=True).astype(jnp.int32)
    blocks = jnp.zeros(cnt.shape, jnp.int32)
    for kb in range(-(-n_t // MOE_BLOCK)):
        blocks = blocks + (cnt > kb * MOE_BLOCK).astype(jnp.int32)
    used = jnp.zeros((1, 1), jnp.int32)
    pos = jnp.zeros((1, n_t), jnp.int32)
    lane = lax.broadcasted_iota(jnp.int32, (1, blk_o.shape[2]), 1)
    blk_grp = jnp.zeros(lane.shape, jnp.int32)
    for gi in range(N_GROUPS):
        pos = pos + jnp.where(g_sel == gi, used * MOE_BLOCK + rank[gi:gi + 1].astype(jnp.int32), 0)
        used = used + blocks[gi:gi + 1]
        if gi < N_GROUPS - 1:
            blk_grp = blk_grp + (used <= lane).astype(jnp.int32)
    pos_o[...] = pos
    n_blk = _moe_blocks(n_t)
    blk_o[0] = jnp.where(lane < n_blk, blk_grp, jnp.where(lane == n_blk, used, 0))


def _moe_blocks(tile):
    return -(-(tile + N_GROUPS * (MOE_BLOCK - 1)) // MOE_BLOCK)


def _route(logits_t, bias_b, tri):
    n_e, n = logits_t.shape
    rt = MOE_TILE
    sds = jax.ShapeDtypeStruct
    return pl.pallas_call(
        _route_kernel,
        grid=(n // rt,),
        in_specs=[pl.BlockSpec((n_e, rt), lambda i: (0, i)), _const_spec(bias_b.shape), _const_spec(tri.shape)],
        out_specs=[pl.BlockSpec((8, rt), lambda i: (0, i)), pl.BlockSpec((1, rt), lambda i: (0, i)),
                   pl.BlockSpec((1, 1, 128), lambda i: (i, 0, 0))],
        out_shape=[sds((8, n), F32), sds((1, n), jnp.int32), sds((n // rt, 1, 128), jnp.int32)],
        compiler_params=pltpu.CompilerParams(dimension_semantics=("arbitrary",)),
        name="route",
    )(logits_t, bias_b, tri)


def _moe_kernel(blk_ref, h_ref, posr_ref, posc_ref, ghi_ref, glo_ref, x1_ref, mod_ref, w1_ref, w3_ref, w2_ref,
                fg_ref, o_ref, xs_ref, ys_ref, *, final_norm, row0, tiles_per_seq):
    i = pl.program_id(0)
    tm, d = x1_ref.shape
    r_rows = xs_ref.shape[0]
    n_blk = r_rows // MOE_BLOCK
    sel = jnp.where(lax.broadcasted_iota(jnp.int32, (r_rows, tm), 0) == posr_ref[...], 1.0, 0.0).astype(BF16)
    gathered = _dot(sel, jnp.concatenate([h_ref[...], ghi_ref[...], glo_ref[...]], axis=1))
    xs_ref[...] = gathered[:, 0:d].astype(BF16)
    gs = gathered[:, d:d + LANE_TILE] + gathered[:, d + LANE_TILE:]
    base = i * BLK_STRIDE
    n_used = blk_ref[base + n_blk]
    for b in range(n_blk):
        rows = slice(b * MOE_BLOCK, (b + 1) * MOE_BLOCK)

        @pl.when(b < n_used)
        def _():
            first = blk_ref[base + b] * GROUP_SIZE
            xb = xs_ref[rows]
            experts = range(GROUP_SIZE)
            a = [_dot(xb, w1_ref[first + j]) for j in experts]
            g = [_dot(xb, w3_ref[first + j]) for j in experts]
            hid = [((a[j] * _sigmoid(a[j])) * g[j]).astype(BF16) for j in experts]
            out = [gs[rows, j:j + 1] * _dot(hid[j], w2_ref[first + j]) for j in experts]
            ys_ref[rows] = ((out[0] + out[1]) + (out[2] + out[3])).astype(BF16)

        @pl.when(b >= n_used)
        def _():
            ys_ref[rows] = jnp.zeros((MOE_BLOCK, d), BF16)

    selt = jnp.where(lax.broadcasted_iota(jnp.int32, (tm, r_rows), 1) == posc_ref[...], 1.0, 0.0).astype(BF16)
    row = row0 + i // tiles_per_seq
    gate2 = mod_ref[pl.ds(row, 1), 5 * d:6 * d]
    x2 = x1_ref[...] + gate2 * _dot(selt, ys_ref[...])
    if final_norm:
        ms = jnp.mean(x2 * x2, axis=-1, keepdims=True)
        x2 = x2 * lax.rsqrt(ms + NORM_EPS) * fg_ref[...]
    o_ref[...] = x2


def _moe(rows, blk, h2, pos_row, pos_col, g_hi, g_lo, x1, mod, w1, w3, w2, final_g, final_norm):
    n, d = x1.shape
    tm = MOE_TILE
    r_rows = _moe_blocks(tm) * MOE_BLOCK
    tok = lambda w: pl.BlockSpec((tm, w), lambda i, *_: (i, 0))
    resident = lambda a: pl.BlockSpec(a.shape, lambda i, *_: (0,) * a.ndim, pipeline_mode=pl.Buffered(1))
    grid_spec = pltpu.PrefetchScalarGridSpec(
        num_scalar_prefetch=1,
        grid=(n // tm,),
        in_specs=[tok(d), pl.BlockSpec((1, tm), lambda i, *_: (0, i)), tok(1), tok(g_hi.shape[1]),
                  tok(g_lo.shape[1]), tok(d), _const_spec(mod.shape), resident(w1), resident(w3), resident(w2),
                  _const_spec(final_g.shape)],
        out_specs=tok(d),
        scratch_shapes=[pltpu.VMEM((r_rows, d), BF16), pltpu.VMEM((r_rows, d), BF16)],
    )
    return pl.pallas_call(
        functools.partial(_moe_kernel, final_norm=final_norm, row0=rows[0],
                          tiles_per_seq=max(rows[1] // tm, 1)),
        grid_spec=grid_spec,
        out_shape=jax.ShapeDtypeStruct((n, d), F32),
        compiler_params=pltpu.CompilerParams(
            dimension_semantics=("arbitrary",), vmem_limit_bytes=VMEM_LIMIT),
        name="moe",
    )(blk, h2, pos_row, pos_col, g_hi, g_lo, x1, mod, w1, w3, w2, final_g)


def kernel(x_prompt, x_sample, state_rwkv, c, c_ctx, norm1_g, norm2_g, final_g, ada_w, ada_b, w_in,
           conv_w, rwkv_w0, rwkv_w2, rwkv_a0, rwkv_a2, rwkv_g2, rwkv_k_k, rwkv_k_a, rwkv_r_k,
           rwkv_lnx_w, rwkv_lnx_b, sgu_ln_g, sgu_ln_b, sgu_ws, sgu_bs, proj_conv, proj_rwkv, proj_sgu,
           w_out, router_w, router_b, exp_w1, exp_w3, exp_w2):
    n_ctx, t_ctx, d = x_prompt.shape
    n_lat, t_lat, _ = x_sample.shape
    n_layer = w_in.shape[0]
    _, n_head, hs = rwkv_r_k.shape
    c_rw = n_head * hs
    conv_c = conv_w.shape[2]
    sgu_c = sgu_ln_g.shape[1]
    n_grp, chunk, _ = sgu_ws.shape[1:]
    lora_w = rwkv_w2.shape[2]
    lora_a = rwkv_a2.shape[2]
    lora_g = rwkv_g2.shape[1]
    n_exp = router_w.shape[1]
    assert 2 * lora_w == 128 and 2 * lora_a == 128 and lora_g == 128
    assert TOKEN_TILE % t_ctx == 0 and t_lat % TOKEN_TILE == 0 and TOKEN_TILE % GRID_W == 0
    assert t_lat % MOE_TILE == 0 and (n_ctx * t_ctx) % MOE_TILE == 0 and TOKEN_TILE % chunk == 0
    assert n_exp == N_GROUPS * GROUP_SIZE and c_rw % (SCAN_HEADS * hs) == 0
    assert SCAN_CHUNK == hs and t_ctx % (SCAN_CHUNK * SCAN_SUB) == 0 and t_lat % (SCAN_CHUNK * SCAN_SUB) == 0
    assert n_ctx % SCAN_PAR == 0 and n_lat % SCAN_PAR == 0
    assert _moe_blocks(MOE_TILE) < BLK_STRIDE and MOE_BLOCK % 16 == 0

    mod_rows = 8 * ((1 + n_lat + 7) // 8)
    cvec = jnp.zeros((mod_rows, d), F32).at[0].set(c_ctx).at[1:1 + n_lat].set(c)
    mod = _modulation(cvec, ada_w, ada_b)

    sizes = (conv_c, conv_c, conv_c, c_rw, c_rw, c_rw, 2 * lora_w, 2 * lora_a, lora_g, sgu_c, sgu_c, 3 * d)
    offs = np.concatenate([[0], np.cumsum(sizes)])
    assert offs[-1] == w_in.shape[2]
    rw_lo, rw_hi = int(offs[3]), int(offs[9])
    seg = jnp.asarray(np.kron(np.eye(n_head), np.ones((hs, hs))), BF16)
    n_str = c_rw // (SCAN_HEADS * hs)

    def side_by_side(s):
        s = jnp.swapaxes(s, -1, -2).reshape(s.shape[0], 2, n_str, SCAN_HEADS, hs, hs)
        return jnp.swapaxes(s, 3, 4).reshape(s.shape[0], 2, n_str, hs, SCAN_HEADS * hs)

    def from_side_by_side(s):
        s = s.reshape(s.shape[0], 2, n_str, hs, SCAN_HEADS, hs)
        return jnp.swapaxes(jnp.swapaxes(s, 3, 4).reshape(s.shape[0], 2, n_head, hs, hs), -1, -2)

    rw_t = router_w.T
    rw_hi_b = rw_t.astype(BF16)
    rw_lo_b = (rw_t - rw_hi_b.astype(F32)).astype(BF16)
    bias_b = jnp.broadcast_to(router_b.astype(F32)[:, None], (n_exp, MOE_TILE))
    tri = jnp.asarray(np.triu(np.ones((MOE_TILE, MOE_TILE)), 1), BF16)
    final_g2 = final_g.reshape(1, d)
    scan_consts = _scan_constants(SCAN_CHUNK, hs)

    n_ctx_tok = n_ctx * t_ctx
    groups = [
        dict(x=x_prompt.reshape(n_ctx_tok, d), t=t_ctx, cols=t_ctx, rows=(0, n_ctx_tok), s0=None),
        dict(x=x_sample.reshape(n_lat * t_lat, d), t=t_lat, cols=GRID_W, rows=(1, t_lat), s0=state_rwkv),
    ]
    new_states = []
    for l in range(n_layer):
        mod_l = mod[l]
        g1 = norm1_g[l].reshape(1, d)
        win = w_in[l]
        win_r = win[:, rw_lo:rw_hi].astype(BF16)
        win_c = jnp.concatenate([win[:, :rw_lo], win[:, rw_hi:]], axis=1).astype(BF16)
        zero_w = jnp.zeros((lora_w, c_rw), F32)
        w2cat = jnp.concatenate([jnp.concatenate([rwkv_w2[l, 0], zero_w], axis=1),
                                 jnp.concatenate([zero_w, rwkv_w2[l, 1]], axis=1)], axis=0).astype(BF16)
        a2cat = jnp.concatenate([jnp.concatenate([rwkv_a2[l, 0], zero_w], axis=1),
                                 jnp.concatenate([zero_w, rwkv_a2[l, 1]], axis=1)], axis=0).astype(BF16)
        bs_full = jnp.repeat(sgu_bs[l].T, sgu_c // n_grp, axis=1)
        k_a = rwkv_k_a[l].reshape(1, c_rw)
        weights = dict(
            g2=rwkv_g2[l].astype(BF16), ws=sgu_ws[l].astype(BF16), pc=proj_conv[l].astype(BF16),
            ps=proj_sgu[l].astype(BF16), pr=proj_rwkv[l].astype(BF16), wo=w_out[l].astype(BF16),
            w1=exp_w1[l].astype(BF16), w3=exp_w3[l].astype(BF16), w2=exp_w2[l].astype(BF16))
        for grp in groups:
            x, rows = grp["x"], grp["rows"]
            r, k, v, kk, lw, aa, bonus, g = _prep(
                rows, x, mod_l, g1, win_r,
                rwkv_w0[l].reshape(1, 2 * c_rw), w2cat, rwkv_a0[l].reshape(1, 2 * c_rw), a2cat,
                weights["g2"], rwkv_k_k[l].reshape(1, c_rw), k_a, rwkv_r_k[l].reshape(1, c_rw), seg)
            part, grw = _branch(
                rows, grp["cols"], x, mod_l, g1, win_c, conv_w[l], sgu_ln_g[l].reshape(1, sgu_c),
                sgu_ln_b[l].reshape(1, sgu_c), weights["ws"], bs_full, weights["pc"], weights["ps"])
            n_seq = x.shape[0] // grp["t"]
            if grp["s0"] is None:
                s0 = jnp.zeros((n_seq, 2, n_str, hs, SCAN_HEADS * hs), F32)
            else:
                s0 = side_by_side(grp["s0"][:, l].astype(F32))
            yf, yb, s_fin = _scan(grp["t"], r, k, v, kk, lw, aa, k_a, s0, scan_consts)
            if grp["s0"] is None:
                new_states.append(from_side_by_side(s_fin))
            x1, h2, logits_t = _merge(
                rows, x, yf, yb, bonus, g, part, grw, mod_l, rwkv_lnx_w[l].reshape(1, c_rw),
                rwkv_lnx_b[l].reshape(1, c_rw), seg, weights["pr"], weights["wo"],
                norm2_g[l].reshape(1, d), rw_hi_b, rw_lo_b, hs)
            g4, pos, blk = _route(logits_t, bias_b, tri)
            g4 = jnp.pad(g4[:GROUP_SIZE].T, ((0, 0), (0, 128 - GROUP_SIZE)))
            g_hi = g4.astype(BF16)
            g_lo = (g4 - g_hi.astype(F32)).astype(BF16)
            grp["x"] = _moe(rows, blk[:, 0, :BLK_STRIDE].reshape(-1), h2, pos, pos.reshape(-1, 1), g_hi, g_lo,
                            x1, mod_l, weights["w1"], weights["w3"], weights["w2"], final_g2,
                            final_norm=(l == n_layer - 1))

    y_prompt = groups[0]["x"].reshape(n_ctx, t_ctx, d)
    y_sample = groups[1]["x"].reshape(n_lat, t_lat, d)
    new_state = jnp.stack(new_states, axis=1).astype(x_prompt.dtype)
    return (y_prompt, y_sample, new_state)
```

```python
import functools
import math

import jax
import jax.numpy as jnp
import numpy as np
from jax import lax
from jax.experimental import pallas as pl
from jax.experimental.pallas import tpu as pltpu

F32 = jnp.float32
BF16 = jnp.bfloat16

GRID_W = 64
N_GROUPS = 4
GROUP_SIZE = 4
NORM_EPS = 1e-6
GN_EPS = 64e-5
KK_EPS = 1e-12
DECAY_SCALE = math.exp(-0.5)

TOKEN_TILE = 512
MOE_TILE = 512
MOE_BLOCK = 128
BLK_STRIDE = 16
SCAN_CHUNK = 64
SCAN_HEADS = 4
SCAN_PAR = 2
SCAN_SUB = 4
SCAN_STAGGER = 4
MOD_COLS = 1536
VMEM_LIMIT = 56 * 1024 * 1024


def _dot(a, b):
    return jnp.dot(a, b, preferred_element_type=F32)


def _dot_nt(a, b):
    return lax.dot_general(a, b, (((1,), (1,)), ((), ())), preferred_element_type=F32)


def _split(x):
    hi = x.astype(BF16)
    lo = (x - hi.astype(F32)).astype(BF16)
    return hi, lo


def _seg_sum(x, seg):
    return _dot(x.astype(BF16), seg)


def _sigmoid(x):
    return 1.0 / (1.0 + jnp.exp(-x))


def _norm_mod(x, gain, shift, scale):
    ms = jnp.mean(x * x, axis=-1, keepdims=True)
    return (x * lax.rsqrt(ms + NORM_EPS) * gain) * (1.0 + scale) + shift


def _const_spec(shape):
    nd = len(shape)
    return pl.BlockSpec(shape, lambda *_: (0,) * nd)


def _mod_kernel(c_ref, w_ref, b_ref, o_ref):
    c = c_ref[...]
    s = c * _sigmoid(c)
    o_ref[0] = _dot(s.astype(BF16), w_ref[0].astype(BF16)) + b_ref[0]


def _modulation(cvec, ada_w, ada_b):
    n_layer, d, cols = ada_w.shape
    rows = cvec.shape[0]
    return pl.pallas_call(
        _mod_kernel,
        grid=(n_layer, cols // MOD_COLS),
        in_specs=[
            pl.BlockSpec((rows, d), lambda l, j: (0, 0)),
            pl.BlockSpec((1, d, MOD_COLS), lambda l, j: (l, 0, j)),
            pl.BlockSpec((1, 1, MOD_COLS), lambda l, j: (l, 0, j)),
        ],
        out_specs=pl.BlockSpec((1, rows, MOD_COLS), lambda l, j: (l, 0, j)),
        out_shape=jax.ShapeDtypeStruct((n_layer, rows, cols), F32),
        compiler_params=pltpu.CompilerParams(
            dimension_semantics=("arbitrary", "arbitrary"), vmem_limit_bytes=VMEM_LIMIT),
        name="mod",
    )(cvec, ada_w, ada_b.reshape(n_layer, 1, cols))


def _prep_kernel(x_ref, mod_ref, g1_ref, win_ref, w0_ref, w2_ref, a0_ref, a2_ref, g2_ref,
                 kk_ref, ka_ref, rk_ref, seg_ref,
                 r_o, k_o, v_o, kk_o, lw_o, aa_o, bonus_o, g_o, *, row0, tiles_per_seq):
    row = row0 + pl.program_id(0) // tiles_per_seq
    x = x_ref[...]
    d = x.shape[1]
    c = r_o.shape[1]
    shift = mod_ref[pl.ds(row, 1), 0:d]
    scale = mod_ref[pl.ds(row, 1), d:2 * d]
    h = _norm_mod(x, g1_ref[...], shift, scale)
    z = _dot(h.astype(BF16), win_ref[...])
    r = z[:, 0:c]
    k = z[:, c:2 * c]
    v = z[:, 2 * c:3 * c]
    lo = 3 * c
    w_lo = z[:, lo:lo + 128]
    a_lo = z[:, lo + 128:lo + 256]
    g_lo = z[:, lo + 256:lo + 384]
    w_logit = w0_ref[...] + _dot(jnp.tanh(w_lo).astype(BF16), w2_ref[...])
    lw = -DECAY_SCALE * _sigmoid(w_logit)
    aa = _sigmoid(a0_ref[...] + _dot(a_lo.astype(BF16), a2_ref[...]))
    g = _dot(_sigmoid(g_lo).astype(BF16), g2_ref[...])
    seg = seg_ref[...]
    kk0 = k * kk_ref[...]
    kk = kk0 / jnp.maximum(jnp.sqrt(_seg_sum(kk0 * kk0, seg)), KK_EPS)
    ka = ka_ref[...]
    k_dirs = k * (1.0 + (aa[:, 0:c] - 1.0) * ka) + k * (1.0 + (aa[:, c:2 * c] - 1.0) * ka)
    rk = _seg_sum(r * k_dirs * rk_ref[...], seg)
    r_o[...] = r
    k_o[...] = k
    v_o[...] = v.astype(BF16)
    kk_o[...] = kk
    lw_o[0] = lw[:, 0:c]
    lw_o[1] = lw[:, c:2 * c]
    aa_o[0] = aa[:, 0:c]
    aa_o[1] = aa[:, c:2 * c]
    bonus_o[...] = (rk * v).astype(BF16)
    g_o[...] = g.astype(BF16)


def _prep(rows, x, mod, g1, win_r, w0, w2, a0, a2, g2, k_k, k_a, r_k, seg):
    n, d = x.shape
    c = seg.shape[0]
    tm = TOKEN_TILE
    tok = lambda w: pl.BlockSpec((tm, w), lambda i: (i, 0))
    tok2 = pl.BlockSpec((2, tm, c), lambda i: (0, i, 0))
    consts = [mod, g1, win_r, w0, w2, a0, a2, g2, k_k, k_a, r_k, seg]
    sds = jax.ShapeDtypeStruct
    return pl.pallas_call(
        functools.partial(_prep_kernel, row0=rows[0], tiles_per_seq=rows[1] // tm),
        grid=(n // tm,),
        in_specs=[tok(d)] + [_const_spec(a.shape) for a in consts],
        out_specs=[tok(c), tok(c), tok(c), tok(c), tok2, tok2, tok(c), tok(c)],
        out_shape=[sds((n, c), F32), sds((n, c), F32), sds((n, c), BF16), sds((n, c), F32)]
        + [sds((2, n, c), F32)] * 2 + [sds((n, c), BF16)] * 2,
        compiler_params=pltpu.CompilerParams(
            dimension_semantics=("arbitrary",), vmem_limit_bytes=VMEM_LIMIT),
        name="prep",
    )(x, *consts)


def _branch_kernel(x_ref, mod_ref, g1_ref, win_ref, cw_ref, lng_ref, lnb_ref,
                   ws_ref, bs_ref, pc_ref, ps_ref, part_o, grw_o, *, row0, tiles_per_seq, cols):
    row = row0 + pl.program_id(0) // tiles_per_seq
    x = x_ref[...]
    tm, d = x.shape
    cw_w = cw_ref.shape[1]
    sg_w = lng_ref.shape[1]
    chunk = ws_ref.shape[1]
    n_grp = ws_ref.shape[0]
    shift = mod_ref[pl.ds(row, 1), 0:d]
    scale = mod_ref[pl.ds(row, 1), d:2 * d]
    h = _norm_mod(x, g1_ref[...], shift, scale)
    z = _dot(h.astype(BF16), win_ref[...])
    cb = z[:, 0:cw_w]
    cc = z[:, cw_w:2 * cw_w]
    cx = z[:, 2 * cw_w:3 * cw_w]
    o = 3 * cw_w
    su = z[:, o:o + sg_w]
    sv = z[:, o + sg_w:o + 2 * sg_w]
    gates = _sigmoid(z[:, o + 2 * sg_w:])
    xc = cc * cx
    col = lax.broadcasted_iota(jnp.int32, xc.shape, 0) & (cols - 1)
    prev = jnp.where(col == 0, 0.0, pltpu.roll(xc, 1, 0))
    nxt = jnp.where(col == cols - 1, 0.0, pltpu.roll(xc, tm - 1, 0))
    cw = cw_ref[...]
    y_conv = cb * (cw[0:1] * prev + cw[1:2] * xc + cw[2:3] * nxt)
    mu = jnp.mean(sv, axis=-1, keepdims=True)
    dv = sv - mu
    var = jnp.mean(dv * dv, axis=-1, keepdims=True)
    vn = (dv * lax.rsqrt(var + NORM_EPS) * lng_ref[...] + lnb_ref[...]).astype(BF16)
    lane_grp = lax.broadcasted_iota(jnp.int32, (chunk, sg_w), 1) // (sg_w // n_grp)
    parts = []
    for ci in range(tm // chunk):
        vc = vn[ci * chunk:(ci + 1) * chunk]
        mixed = bs_ref[...]
        for gi in range(n_grp):
            mixed = mixed + jnp.where(lane_grp == gi, _dot(ws_ref[gi], vc), 0.0)
        parts.append(su[ci * chunk:(ci + 1) * chunk] * mixed)
    y_sgu = jnp.concatenate(parts, axis=0)
    part_o[...] = (gates[:, 0:d] * _dot(y_conv.astype(BF16), pc_ref[...])
                   + gates[:, 2 * d:3 * d] * _dot(y_sgu.astype(BF16), ps_ref[...])).astype(BF16)
    grw_o[...] = gates[:, d:2 * d].astype(BF16)


def _branch(rows, cols, x, mod, g1, win_c, conv_w, ln_g, ln_b, ws, bs_full, proj_conv, proj_sgu):
    n, d = x.shape
    tm = TOKEN_TILE
    tok = pl.BlockSpec((tm, d), lambda i: (i, 0))
    consts = [mod, g1, win_c, conv_w, ln_g, ln_b, ws, bs_full, proj_conv, proj_sgu]
    return pl.pallas_call(
        functools.partial(_branch_kernel, row0=rows[0], tiles_per_seq=rows[1] // tm, cols=cols),
        grid=(n // tm,),
        in_specs=[tok] + [_const_spec(a.shape) for a in consts],
        out_specs=[tok, tok],
        out_shape=[jax.ShapeDtypeStruct((n, d), BF16)] * 2,
        compiler_params=pltpu.CompilerParams(
            dimension_semantics=("arbitrary",), vmem_limit_bytes=VMEM_LIMIT),
        name="branch",
    )(x, *consts)


LANE_TILE = 128


def _scan_constants(n_l, hs):
    lanes = SCAN_HEADS * hs
    t = np.arange(n_l)[:, None]
    s = np.arange(lanes)[None, :] % n_l
    ti, si = np.arange(n_l)[:, None], np.arange(n_l)[None, :]
    n_lvl = n_l.bit_length() - 1
    tri, masks, lvl = [], [], []
    for reverse in (False, True):
        if not reverse:
            level = lambda q: (((t >> q) & 1) == 1) & ((s >> q) == (t >> q) - 1)
            tri.append(ti >= si)
            masks.append(np.concatenate([t > s, t >= s, level(0)], axis=0))
        else:
            level = lambda q: (((t >> q) & 1) == 0) & ((s >> q) == (t >> q) + 1)
            tri.append(ti <= si)
            masks.append(np.concatenate([t < s, t <= s, level(0)], axis=0))
        lvl.append([level(q) for q in range(1, n_lvl)])
    as_j = lambda a, dt: jnp.asarray(np.asarray(a, np.float32), dt)
    return as_j(t == s, F32), as_j(masks, F32), as_j(lvl, BF16), as_j(tri, BF16)


class _HeadBlocks:
    def __init__(self, hs, lanes):
        self.hs, self.lanes = hs, lanes
        self.n_tiles = lanes // LANE_TILE
        self.per_tile = LANE_TILE // hs
        self.heads = lanes // hs
        lane_head = lax.broadcasted_iota(jnp.int32, (hs, LANE_TILE), 1) // hs
        self.half_f32 = [(lane_head == j).astype(F32) for j in range(self.per_tile)]
        self.half_bf16 = [m.astype(BF16) for m in self.half_f32]
        row_head = lax.broadcasted_iota(jnp.int32, (LANE_TILE, LANE_TILE), 0) // hs
        col_head = lax.broadcasted_iota(jnp.int32, (LANE_TILE, LANE_TILE), 1) // hs
        self.tile_mask = (row_head == col_head).astype(F32)

    def expand(self, compact):
        zero = jnp.zeros((self.hs, LANE_TILE), compact.dtype)
        rows = []
        for h in range(self.heads):
            tiles = [zero] * self.n_tiles
            tiles[h // self.per_tile] = compact[h * self.hs:(h + 1) * self.hs]
            rows.append(jnp.concatenate(tiles, axis=1))
        return jnp.concatenate(rows, axis=0)

    def compact(self, x, masks):
        return jnp.concatenate(
            [x[:, (h // self.per_tile) * LANE_TILE:(h // self.per_tile + 1) * LANE_TILE] * masks[h % self.per_tile]
             for h in range(self.heads)], axis=0)

    def diag(self, full, masks):
        return jnp.concatenate(
            [full[h * self.hs:(h + 1) * self.hs,
                  (h // self.per_tile) * LANE_TILE:(h // self.per_tile + 1) * LANE_TILE] * masks[h % self.per_tile]
             for h in range(self.heads)], axis=0)

    def side_by_side(self, compact):
        tiles = []
        for t in range(self.n_tiles):
            acc = compact[t * self.per_tile * self.hs:(t * self.per_tile + 1) * self.hs]
            for j in range(1, self.per_tile):
                h = t * self.per_tile + j
                acc = acc + compact[h * self.hs:(h + 1) * self.hs]
            tiles.append(acc)
        return jnp.concatenate(tiles, axis=1)

    def bd(self, x):
        return self.expand(self.compact(x.astype(BF16), self.half_bf16))

    def bd_t(self, x):
        tiles = []
        for t in range(self.n_tiles):
            square = jnp.concatenate([x[:, t * LANE_TILE:(t + 1) * LANE_TILE]] * self.per_tile, axis=0)
            tiles.append((square.T * self.tile_mask).astype(BF16))
        return self.expand(jnp.concatenate(tiles, axis=0))


def _scan_prepare(streams, hb, eye, masks_ref, lvl_ref, tri_ref, out):
    n_l = streams[0][0].shape[0]
    ns = range(len(streams))
    bd = hb.bd
    r, k, v, kk, lw, aa, ka, di = map(list, zip(*streams))
    tri = lambda i: tri_ref[di[i]]
    tril_mask = lambda i: masks_ref[di[i], 0:2 * n_l]
    end = [0 if d else n_l - 1 for d in di]
    l_hi = [lw[i].astype(BF16) for i in ns]
    l_mid = [(lw[i] - l_hi[i].astype(F32)).astype(BF16) for i in ns]
    l_lo = [(lw[i] - l_hi[i].astype(F32) - l_mid[i].astype(F32)).astype(BF16) for i in ns]
    c = [_dot(tri(i), l_hi[i]) + _dot(tri(i), l_mid[i]) + _dot(tri(i), l_lo[i]) for i in ns]
    yield
    b_vec = [kk[i] * aa[i] for i in ns]
    kd = [k[i] * (1.0 + (aa[i] - 1.0) * ka[i]) for i in ns]
    c_end = [c[i][end[i]:end[i] + 1] for i in ns]
    e_neg = [jnp.exp(-c[i]) for i in ns]
    lhs = [jnp.concatenate([-kk[i] * jnp.exp(c[i] - lw[i]), r[i] * jnp.exp(c[i])], axis=0).astype(BF16)
           for i in ns]
    yield
    m_b = [_dot(lhs[i], hb.bd_t(b_vec[i] * e_neg[i])) * tril_mask(i) for i in ns]
    yield
    m_k = [(_dot(lhs[i], hb.bd_t(kd[i] * e_neg[i])) * tril_mask(i)).astype(BF16) for i in ns]
    yield
    a_ab = [m_b[i][0:n_l] for i in ns]
    t_inv = [eye + a_ab[i] * masks_ref[di[i], 2 * n_l:3 * n_l] for i in ns]
    a_bf = [a_ab[i].astype(BF16) for i in ns]
    for q in range(lvl_ref.shape[1]):
        x1 = [_dot(t_inv[i].astype(BF16), bd(a_bf[i] * lvl_ref[di[i], q])) for i in ns]
        yield
        x2 = [_dot(x1[i].astype(BF16), bd(t_inv[i])) for i in ns]
        t_inv = [t_inv[i] + x2[i] for i in ns]
        yield
    v_k = [_dot(m_k[i], bd(v[i])) for i in ns]
    e_end = [jnp.exp(c_end[i] - c[i]) for i in ns]
    bk_t = [jnp.concatenate([b_vec[i] * e_end[i], kd[i] * e_end[i]], axis=0).T.astype(BF16) for i in ns]
    decay_col = [jnp.broadcast_to(jnp.exp(c_end[i]), (LANE_TILE, hb.lanes)).T for i in ns]
    out.extend((lhs[i], v_k[i], t_inv[i], m_b[i][n_l:].astype(BF16), v[i], bk_t[i], decay_col[i]) for i in ns)


def _scan_apply(pre, z, hb, out):
    lhs, v_k, t_inv, a_rb, v, bk_t, decay_col = map(list, zip(*pre))
    n_l = t_inv[0].shape[0]
    ns = range(len(pre))
    v_part = [_dot(lhs[i], hb.expand(z[i].astype(BF16))) + v_k[i] for i in ns]
    yield
    u = [_dot(t_inv[i].astype(BF16), hb.bd(v_part[i][0:n_l])) for i in ns]
    yield
    y = [v_part[i][n_l:] + _dot(a_rb[i], hb.bd(u[i])) for i in ns]
    yield
    uv = [jnp.concatenate([u[i].astype(BF16), v[i].astype(BF16)], axis=0) for i in ns]
    z_new = [z[i] * decay_col[i] + hb.diag(_dot(bk_t[i], uv[i]), hb.half_f32) for i in ns]
    out.extend(zip(y, z_new))


def _scan_kernel(rf_ref, kf_ref, vf_ref, kkf_ref, lwf_ref, aaf_ref,
                 rb_ref, kb_ref, vb_ref, kkb_ref, lwb_ref, aab_ref,
                 ka_ref, s0_ref, eye_ref, masks_ref, lvl_ref, tri_ref,
                 yf_o, yb_o, sfin_o, z_scr):
    j = pl.program_id(1)
    n_par, n_rows, c = rf_ref.shape
    n_l = eye_ref.shape[0]
    lanes = z_scr.shape[1]
    n_str = c // lanes
    hb = _HeadBlocks(lanes // SCAN_HEADS, lanes)
    slots = [(p, di, hg) for p in range(n_par) for di in range(2) for hg in range(n_str)]

    @pl.when(j == 0)
    def _():
        for zi, (p, di, hg) in enumerate(slots):
            z_scr[zi] = hb.compact(s0_ref[p, 0, di, hg], hb.half_f32)

    ka = ka_ref[...]
    dirs = ((rf_ref, kf_ref, vf_ref, kkf_ref, lwf_ref, aaf_ref, yf_o),
            (rb_ref, kb_ref, vb_ref, kkb_ref, lwb_ref, aab_ref, yb_o))
    n_sub = n_rows // n_l
    streams = []
    for sub in range(n_sub):
        for p, di, hg in slots:
            r_ref, k_ref, v_ref, kk_ref, lw_ref, aa_ref, _ = dirs[di]
            rs = slice((n_sub - 1 - sub if di else sub) * n_l, (n_sub - sub if di else sub + 1) * n_l)
            ls = slice(hg * lanes, (hg + 1) * lanes)
            streams.append((r_ref[p, rs, ls], k_ref[p, rs, ls], v_ref[p, rs, ls], kk_ref[p, rs, ls],
                            lw_ref[0, p, rs, ls], aa_ref[0, p, rs, ls], ka[:, ls], di))
    n_slot = len(slots)
    eye = eye_ref[...]
    pre = [[] for _ in range(n_sub)]
    res = [[] for _ in range(n_sub)]
    z0 = [z_scr[zi] for zi in range(n_slot)]

    gens = [_scan_prepare(streams[s * n_slot:(s + 1) * n_slot], hb, eye, masks_ref, lvl_ref, tri_ref, pre[s])
            for s in range(n_sub)]
    for _ in range(SCAN_STAGGER):
        next(gens[0])
    for s in range(n_sub):
        nxt = gens[s + 1] if s + 1 < n_sub else None
        for _ in gens[s]:
            if nxt is not None:
                next(nxt, None)
        z_in = z0 if s == 0 else [o[1] for o in res[s - 1]]
        for _ in _scan_apply(pre[s], z_in, hb, res[s]):
            if nxt is not None:
                next(nxt, None)
        for (y, _), (p, di, hg) in zip(res[s], slots):
            rs = slice((n_sub - 1 - s if di else s) * n_l, (n_sub - s if di else s + 1) * n_l)
            dirs[di][6][p, rs, hg * lanes:(hg + 1) * lanes] = y
    z = [o[1] for o in res[n_sub - 1]]
    for zi in range(len(slots)):
        z_scr[zi] = z[zi]

    @pl.when(j == pl.num_programs(1) - 1)
    def _():
        for zi, (p, di, hg) in enumerate(slots):
            sfin_o[p, 0, di, hg] = hb.side_by_side(z_scr[zi])


def _scan(t_len, r, k, v, kk, lw, aa, k_a, s0, consts):
    n, c = r.shape
    n_seq = s0.shape[0]
    n_par = SCAN_PAR
    n_l = SCAN_CHUNK * SCAN_SUB
    n_c = t_len // n_l
    hs = s0.shape[3]
    lanes = SCAN_HEADS * hs
    n_str = c // lanes
    part = lambda a: a.reshape(n_par, n // n_par, c)
    part2 = lambda a: a.reshape(2, n_par, n // n_par, c)
    s0p = s0.reshape((n_par, n_seq // n_par) + s0.shape[1:])
    fwd = pl.BlockSpec((n_par, n_l, c), lambda q, j: (0, q * n_c + j, 0))
    bwd = pl.BlockSpec((n_par, n_l, c), lambda q, j: (0, q * n_c + n_c - 1 - j, 0))
    fwd2 = pl.BlockSpec((1, n_par, n_l, c), lambda q, j: (0, 0, q * n_c + j, 0))
    bwd2 = pl.BlockSpec((1, n_par, n_l, c), lambda q, j: (1, 0, q * n_c + n_c - 1 - j, 0))
    st = pl.BlockSpec((n_par, 1, 2, n_str, hs, lanes), lambda q, j: (0, q, 0, 0, 0, 0))
    sds = jax.ShapeDtypeStruct
    args = [part(r), part(k), part(v), part(kk), part2(lw), part2(aa)]
    yf, yb, s_fin = pl.pallas_call(
        _scan_kernel,
        grid=(n_seq // n_par, n_c),
        in_specs=[fwd, fwd, fwd, fwd, fwd2, fwd2, bwd, bwd, bwd, bwd, bwd2, bwd2,
                  _const_spec(k_a.shape), st] + [_const_spec(a.shape) for a in consts],
        out_specs=[fwd, bwd, st],
        out_shape=[sds((n_par, n // n_par, c), F32), sds((n_par, n // n_par, c), F32), sds(s0p.shape, F32)],
        scratch_shapes=[pltpu.VMEM((n_par * 2 * n_str, lanes, LANE_TILE), F32)],
        compiler_params=pltpu.CompilerParams(
            dimension_semantics=("arbitrary", "arbitrary"), vmem_limit_bytes=VMEM_LIMIT),
        name="scan",
    )(*args, *args, k_a, s0p, *consts)
    return yf.reshape(n, c), yb.reshape(n, c), s_fin.reshape(s0.shape)


def _merge_kernel(x_ref, yf_ref, yb_ref, bonus_ref, g_ref, part_ref, grw_ref, mod_ref,
                  lnw_ref, lnb_ref, seg_ref, prw_ref, wout_ref, g2_ref, rwh_ref, rwl_ref,
                  x1_o, h2_o, lg_o, *, head_size, row0, tiles_per_seq):
    row = row0 + pl.program_id(0) // tiles_per_seq
    x = x_ref[...]
    d = x.shape[1]
    seg = seg_ref[...]
    y = yf_ref[...] + yb_ref[...]
    hs_inv = 1.0 / head_size
    mu = _seg_sum(y, seg) * hs_inv
    dy = y - mu
    var = _seg_sum(dy * dy, seg) * hs_inv
    yn = dy * lax.rsqrt(var + GN_EPS) * lnw_ref[...] + lnb_ref[...]
    y_rwkv = (yn + bonus_ref[...]) * g_ref[...]
    merged = part_ref[...] + grw_ref[...] * _dot(y_rwkv.astype(BF16), prw_ref[...])
    mix = _dot(merged.astype(BF16), wout_ref[...])
    gate1 = mod_ref[pl.ds(row, 1), 2 * d:3 * d]
    shift2 = mod_ref[pl.ds(row, 1), 3 * d:4 * d]
    scale2 = mod_ref[pl.ds(row, 1), 4 * d:5 * d]
    x1 = x + gate1 * mix
    h2 = _norm_mod(x1, g2_ref[...], shift2, scale2)
    h_hi, h_lo = _split(h2)
    rwh = rwh_ref[...]
    n_e = rwh.shape[0]
    both = _dot_nt(jnp.concatenate([rwh, rwl_ref[...]], axis=0), h_hi)
    x1_o[...] = x1
    h2_o[...] = h_hi
    lg_o[...] = both[0:n_e] + both[n_e:] + _dot_nt(rwh, h_lo)


def _merge(rows, x, yf, yb, bonus, g, part, grw, mod, lnw, lnb, seg, proj_rwkv, w_out, g2n, rw_hi, rw_lo,
           head_size):
    n, d = x.shape
    c = seg.shape[0]
    n_e = rw_hi.shape[0]
    tm = TOKEN_TILE
    tok = lambda w: pl.BlockSpec((tm, w), lambda i: (i, 0))
    consts = [mod, lnw, lnb, seg, proj_rwkv, w_out, g2n, rw_hi, rw_lo]
    sds = jax.ShapeDtypeStruct
    return pl.pallas_call(
        functools.partial(_merge_kernel, head_size=head_size, row0=rows[0], tiles_per_seq=rows[1] // tm),
        grid=(n // tm,),
        in_specs=[tok(d), tok(c), tok(c), tok(c), tok(c), tok(d), tok(d)]
        + [_const_spec(a.shape) for a in consts],
        out_specs=[tok(d), tok(d), pl.BlockSpec((n_e, tm), lambda i: (0, i))],
        out_shape=[sds((n, d), F32), sds((n, d), BF16), sds((n_e, n), F32)],
        compiler_params=pltpu.CompilerParams(
            dimension_semantics=("arbitrary",), vmem_limit_bytes=VMEM_LIMIT),
        name="merge",
    )(x, yf, yb, bonus, g, part, grw, *consts)


def _route_kernel(lg_ref, b_ref, tri_ref, g4_o, pos_o, blk_o):
    scores = _sigmoid(lg_ref[...])
    biased = scores + b_ref[...]
    n_e, n_t = scores.shape
    rows = [biased[e:e + 1] for e in range(n_e)]
    srow = [scores[e:e + 1] for e in range(n_e)]

    def top2_sum(a, b, c, d):
        s1, t1 = jnp.maximum(a, b), jnp.minimum(a, b)
        s2, t2 = jnp.maximum(c, d), jnp.minimum(c, d)
        return jnp.maximum(s1, s2) + jnp.maximum(jnp.minimum(s1, s2), jnp.maximum(t1, t2))

    best = top2_sum(*rows[0:GROUP_SIZE])
    g_sel = jnp.zeros(best.shape, jnp.int32)
    for gi in range(1, N_GROUPS):
        gs = top2_sum(*rows[gi * GROUP_SIZE:(gi + 1) * GROUP_SIZE])
        better = gs > best
        best = jnp.where(better, gs, best)
        g_sel = jnp.where(better, gi, g_sel)

    def pick(src, j):
        out = src[j]
        for gi in range(1, N_GROUPS):
            out = jnp.where(g_sel == gi, src[gi * GROUP_SIZE + j], out)
        return out

    in_b = [pick(rows, j) for j in range(GROUP_SIZE)]
    in_s = [pick(srow, j) for j in range(GROUP_SIZE)]

    def arg_first_max(vals):
        best_v = vals[0]
        best_i = jnp.zeros(best_v.shape, jnp.int32)
        for j in range(1, len(vals)):
            better = vals[j] > best_v
            best_v = jnp.where(better, vals[j], best_v)
            best_i = jnp.where(better, j, best_i)
        return best_i

    i1 = arg_first_max(in_b)
    i2 = arg_first_max([jnp.where(i1 == j, -jnp.inf, in_b[j]) for j in range(GROUP_SIZE)])

    def take(src, idx):
        out = src[0]
        for j in range(1, GROUP_SIZE):
            out = jnp.where(idx == j, src[j], out)
        return out

    g1, g2 = take(in_s, i1), take(in_s, i2)
    den = g1 + g2
    j_idx = lax.broadcasted_iota(jnp.int32, (g4_o.shape[0], n_t), 0)
    g4_o[...] = jnp.where(j_idx == i1, g1 / den, jnp.where(j_idx == i2, g2 / den, 0.0))

    grp = lax.broadcasted_iota(jnp.int32, (n_e, n_t), 0)
    onehot = (grp == g_sel).astype(F32)
    rank = _dot(onehot.astype(BF16), tri_ref[...])
    cnt = jnp.sum(onehot, axis=1, keepdims=True).astype(jnp.int32)
    blocks = jnp.zeros(cnt.shape, jnp.int32)
    for kb in range(-(-n_t // MOE_BLOCK)):
        blocks = blocks + (cnt > kb * MOE_BLOCK).astype(jnp.int32)
    used = jnp.zeros((1, 1), jnp.int32)
    pos = jnp.zeros((1, n_t), jnp.int32)
    lane = lax.broadcasted_iota(jnp.int32, (1, blk_o.shape[2]), 1)
    blk_grp = jnp.zeros(lane.shape, jnp.int32)
    for gi in range(N_GROUPS):
        pos = pos + jnp.where(g_sel == gi, used * MOE_BLOCK + rank[gi:gi + 1].astype(jnp.int32), 0)
        used = used + blocks[gi:gi + 1]
        if gi < N_GROUPS - 1:
            blk_grp = blk_grp + (used <= lane).astype(jnp.int32)
    pos_o[...] = pos
    n_blk = _moe_blocks(n_t)
    blk_o[0] = jnp.where(lane < n_blk, blk_grp, jnp.where(lane == n_blk, used, 0))


def _moe_blocks(tile):
    return -(-(tile + N_GROUPS * (MOE_BLOCK - 1)) // MOE_BLOCK)


def _route(logits_t, bias_b, tri):
    n_e, n = logits_t.shape
    rt = MOE_TILE
    sds = jax.ShapeDtypeStruct
    return pl.pallas_call(
        _route_kernel,
        grid=(n // rt,),
        in_specs=[pl.BlockSpec((n_e, rt), lambda i: (0, i)), _const_spec(bias_b.shape), _const_spec(tri.shape)],
        out_specs=[pl.BlockSpec((8, rt), lambda i: (0, i)), pl.BlockSpec((1, rt), lambda i: (0, i)),
                   pl.BlockSpec((1, 1, 128), lambda i: (i, 0, 0))],
        out_shape=[sds((8, n), F32), sds((1, n), jnp.int32), sds((n // rt, 1, 128), jnp.int32)],
        compiler_params=pltpu.CompilerParams(dimension_semantics=("arbitrary",)),
        name="route",
    )(logits_t, bias_b, tri)


def _moe_kernel(blk_ref, h_ref, posr_ref, posc_ref, ghi_ref, glo_ref, x1_ref, mod_ref, w1_ref, w3_ref, w2_ref,
                fg_ref, o_ref, xs_ref, ys_ref, *, final_norm, row0, tiles_per_seq):
    i = pl.program_id(0)
    tm, d = x1_ref.shape
    r_rows = xs_ref.shape[0]
    n_blk = r_rows // MOE_BLOCK
    sel = jnp.where(lax.broadcasted_iota(jnp.int32, (r_rows, tm), 0) == posr_ref[...], 1.0, 0.0).astype(BF16)
    gathered = _dot(sel, jnp.concatenate([h_ref[...], ghi_ref[...], glo_ref[...]], axis=1))
    xs_ref[...] = gathered[:, 0:d].astype(BF16)
    gs = gathered[:, d:d + LANE_TILE] + gathered[:, d + LANE_TILE:]
    base = i * BLK_STRIDE
    n_used = blk_ref[base + n_blk]
    for b in range(n_blk):
        rows = slice(b * MOE_BLOCK, (b + 1) * MOE_BLOCK)

        @pl.when(b < n_used)
        def _():
            first = blk_ref[base + b] * GROUP_SIZE
            xb = xs_ref[rows]
            experts = range(GROUP_SIZE)
            a = [_dot(xb, w1_ref[first + j]) for j in experts]
            g = [_dot(xb, w3_ref[first + j]) for j in experts]
            hid = [((a[j] * _sigmoid(a[j])) * g[j]).astype(BF16) for j in experts]
            out = [gs[rows, j:j + 1] * _dot(hid[j], w2_ref[first + j]) for j in experts]
            ys_ref[rows] = ((out[0] + out[1]) + (out[2] + out[3])).astype(BF16)

        @pl.when(b >= n_used)
        def _():
            ys_ref[rows] = jnp.zeros((MOE_BLOCK, d), BF16)

    selt = jnp.where(lax.broadcasted_iota(jnp.int32, (tm, r_rows), 1) == posc_ref[...], 1.0, 0.0).astype(BF16)
    row = row0 + i // tiles_per_seq
    gate2 = mod_ref[pl.ds(row, 1), 5 * d:6 * d]
    x2 = x1_ref[...] + gate2 * _dot(selt, ys_ref[...])
    if final_norm:
        ms = jnp.mean(x2 * x2, axis=-1, keepdims=True)
        x2 = x2 * lax.rsqrt(ms + NORM_EPS) * fg_ref[...]
    o_ref[...] = x2


def _moe(rows, blk, h2, pos_row, pos_col, g_hi, g_lo, x1, mod, w1, w3, w2, final_g, final_norm):
    n, d = x1.shape
    tm = MOE_TILE
    r_rows = _moe_blocks(tm) * MOE_BLOCK
    tok = lambda w: pl.BlockSpec((tm, w), lambda i, *_: (i, 0))
    resident = lambda a: pl.BlockSpec(a.shape, lambda i, *_: (0,) * a.ndim, pipeline_mode=pl.Buffered(1))
    grid_spec = pltpu.PrefetchScalarGridSpec(
        num_scalar_prefetch=1,
        grid=(n // tm,),
        in_specs=[tok(d), pl.BlockSpec((1, tm), lambda i, *_: (0, i)), tok(1), tok(g_hi.shape[1]),
                  tok(g_lo.shape[1]), tok(d), _const_spec(mod.shape), resident(w1), resident(w3), resident(w2),
                  _const_spec(final_g.shape)],
        out_specs=tok(d),
        scratch_shapes=[pltpu.VMEM((r_rows, d), BF16), pltpu.VMEM((r_rows, d), BF16)],
    )
    return pl.pallas_call(
        functools.partial(_moe_kernel, final_norm=final_norm, row0=rows[0],
                          tiles_per_seq=max(rows[1] // tm, 1)),
        grid_spec=grid_spec,
        out_shape=jax.ShapeDtypeStruct((n, d), F32),
        compiler_params=pltpu.CompilerParams(
            dimension_semantics=("arbitrary",), vmem_limit_bytes=VMEM_LIMIT),
        name="moe",
    )(blk, h2, pos_row, pos_col, g_hi, g_lo, x1, mod, w1, w3, w2, final_g)


def kernel(x_prompt, x_sample, state_rwkv, c, c_ctx, norm1_g, norm2_g, final_g, ada_w, ada_b, w_in,
           conv_w, rwkv_w0, rwkv_w2, rwkv_a0, rwkv_a2, rwkv_g2, rwkv_k_k, rwkv_k_a, rwkv_r_k,
           rwkv_lnx_w, rwkv_lnx_b, sgu_ln_g, sgu_ln_b, sgu_ws, sgu_bs, proj_conv, proj_rwkv, proj_sgu,
           w_out, router_w, router_b, exp_w1, exp_w3, exp_w2):
    n_ctx, t_ctx, d = x_prompt.shape
    n_lat, t_lat, _ = x_sample.shape
    n_layer = w_in.shape[0]
    _, n_head, hs = rwkv_r_k.shape
    c_rw = n_head * hs
    conv_c = conv_w.shape[2]
    sgu_c = sgu_ln_g.shape[1]
    n_grp, chunk, _ = sgu_ws.shape[1:]
    lora_w = rwkv_w2.shape[2]
    lora_a = rwkv_a2.shape[2]
    lora_g = rwkv_g2.shape[1]
    n_exp = router_w.shape[1]
    assert 2 * lora_w == 128 and 2 * lora_a == 128 and lora_g == 128
    assert TOKEN_TILE % t_ctx == 0 and t_lat % TOKEN_TILE == 0 and TOKEN_TILE % GRID_W == 0
    assert t_lat % MOE_TILE == 0 and (n_ctx * t_ctx) % MOE_TILE == 0 and TOKEN_TILE % chunk == 0
    assert n_exp == N_GROUPS * GROUP_SIZE and c_rw % (SCAN_HEADS * hs) == 0
    assert SCAN_CHUNK == hs and t_ctx % (SCAN_CHUNK * SCAN_SUB) == 0 and t_lat % (SCAN_CHUNK * SCAN_SUB) == 0
    assert n_ctx % SCAN_PAR == 0 and n_lat % SCAN_PAR == 0
    assert _moe_blocks(MOE_TILE) < BLK_STRIDE and MOE_BLOCK % 16 == 0

    mod_rows = 8 * ((1 + n_lat + 7) // 8)
    cvec = jnp.zeros((mod_rows, d), F32).at[0].set(c_ctx).at[1:1 + n_lat].set(c)
    mod = _modulation(cvec, ada_w, ada_b)

    sizes = (conv_c, conv_c, conv_c, c_rw, c_rw, c_rw, 2 * lora_w, 2 * lora_a, lora_g, sgu_c, sgu_c, 3 * d)
    offs = np.concatenate([[0], np.cumsum(sizes)])
    assert offs[-1] == w_in.shape[2]
    rw_lo, rw_hi = int(offs[3]), int(offs[9])
    seg = jnp.asarray(np.kron(np.eye(n_head), np.ones((hs, hs))), BF16)
    n_str = c_rw // (SCAN_HEADS * hs)

    def side_by_side(s):
        s = jnp.swapaxes(s, -1, -2).reshape(s.shape[0], 2, n_str, SCAN_HEADS, hs, hs)
        return jnp.swapaxes(s, 3, 4).reshape(s.shape[0], 2, n_str, hs, SCAN_HEADS * hs)

    def from_side_by_side(s):
        s = s.reshape(s.shape[0], 2, n_str, hs, SCAN_HEADS, hs)
        return jnp.swapaxes(jnp.swapaxes(s, 3, 4).reshape(s.shape[0], 2, n_head, hs, hs), -1, -2)

    rw_t = router_w.T
    rw_hi_b = rw_t.astype(BF16)
    rw_lo_b = (rw_t - rw_hi_b.astype(F32)).astype(BF16)
    bias_b = jnp.broadcast_to(router_b.astype(F32)[:, None], (n_exp, MOE_TILE))
    tri = jnp.asarray(np.triu(np.ones((MOE_TILE, MOE_TILE)), 1), BF16)
    final_g2 = final_g.reshape(1, d)
    scan_consts = _scan_constants(SCAN_CHUNK, hs)

    n_ctx_tok = n_ctx * t_ctx
    groups = [
        dict(x=x_prompt.reshape(n_ctx_tok, d), t=t_ctx, cols=t_ctx, rows=(0, n_ctx_tok), s0=None),
        dict(x=x_sample.reshape(n_lat * t_lat, d), t=t_lat, cols=GRID_W, rows=(1, t_lat), s0=state_rwkv),
    ]
    new_states = []
    for l in range(n_layer):
        mod_l = mod[l]
        g1 = norm1_g[l].reshape(1, d)
        win = w_in[l]
        win_r = win[:, rw_lo:rw_hi].astype(BF16)
        win_c = jnp.concatenate([win[:, :rw_lo], win[:, rw_hi:]], axis=1).astype(BF16)
        zero_w = jnp.zeros((lora_w, c_rw), F32)
        w2cat = jnp.concatenate([jnp.concatenate([rwkv_w2[l, 0], zero_w], axis=1),
                                 jnp.concatenate([zero_w, rwkv_w2[l, 1]], axis=1)], axis=0).astype(BF16)
        a2cat = jnp.concatenate([jnp.concatenate([rwkv_a2[l, 0], zero_w], axis=1),
                                 jnp.concatenate([zero_w, rwkv_a2[l, 1]], axis=1)], axis=0).astype(BF16)
        bs_full = jnp.repeat(sgu_bs[l].T, sgu_c // n_grp, axis=1)
        k_a = rwkv_k_a[l].reshape(1, c_rw)
        weights = dict(
            g2=rwkv_g2[l].astype(BF16), ws=sgu_ws[l].astype(BF16), pc=proj_conv[l].astype(BF16),
            ps=proj_sgu[l].astype(BF16), pr=proj_rwkv[l].astype(BF16), wo=w_out[l].astype(BF16),
            w1=exp_w1[l].astype(BF16), w3=exp_w3[l].astype(BF16), w2=exp_w2[l].astype(BF16))
        for grp in groups:
            x, rows = grp["x"], grp["rows"]
            r, k, v, kk, lw, aa, bonus, g = _prep(
                rows, x, mod_l, g1, win_r,
                rwkv_w0[l].reshape(1, 2 * c_rw), w2cat, rwkv_a0[l].reshape(1, 2 * c_rw), a2cat,
                weights["g2"], rwkv_k_k[l].reshape(1, c_rw), k_a, rwkv_r_k[l].reshape(1, c_rw), seg)
            part, grw = _branch(
                rows, grp["cols"], x, mod_l, g1, win_c, conv_w[l], sgu_ln_g[l].reshape(1, sgu_c),
                sgu_ln_b[l].reshape(1, sgu_c), weights["ws"], bs_full, weights["pc"], weights["ps"])
            n_seq = x.shape[0] // grp["t"]
            if grp["s0"] is None:
                s0 = jnp.zeros((n_seq, 2, n_str, hs, SCAN_HEADS * hs), F32)
            else:
                s0 = side_by_side(grp["s0"][:, l].astype(F32))
            yf, yb, s_fin = _scan(grp["t"], r, k, v, kk, lw, aa, k_a, s0, scan_consts)
            if grp["s0"] is None:
                new_states.append(from_side_by_side(s_fin))
            x1, h2, logits_t = _merge(
                rows, x, yf, yb, bonus, g, part, grw, mod_l, rwkv_lnx_w[l].reshape(1, c_rw),
                rwkv_lnx_b[l].reshape(1, c_rw), seg, weights["pr"], weights["wo"],
                norm2_g[l].reshape(1, d), rw_hi_b, rw_lo_b, hs)
            g4, pos, blk = _route(logits_t, bias_b, tri)
            g4 = jnp.pad(g4[:GROUP_SIZE].T, ((0, 0), (0, 128 - GROUP_SIZE)))
            g_hi = g4.astype(BF16)
            g_lo = (g4 - g_hi.astype(F32)).astype(BF16)
            grp["x"] = _moe(rows, blk[:, 0, :BLK_STRIDE].reshape(-1), h2, pos, pos.reshape(-1, 1), g_hi, g_lo,
                            x1, mod_l, weights["w1"], weights["w3"], weights["w2"], final_g2,
                            final_norm=(l == n_layer - 1))

    y_prompt = groups[0]["x"].reshape(n_ctx, t_ctx, d)
    y_sample = groups[1]["x"].reshape(n_lat, t_lat, d)
    new_state = jnp.stack(new_states, axis=1).astype(x_prompt.dtype)
    return (y_prompt, y_sample, new_state)
```

```python
import functools
import math

import jax
import jax.numpy as jnp
import numpy as np
from jax import lax
from jax.experimental import pallas as pl
from jax.experimental.pallas import tpu as pltpu

F32 = jnp.float32
BF16 = jnp.bfloat16

LANE_TILE = 128
SUBLANE_TILE = 8

GRID_W = 64
N_GROUPS = 4
GROUP_SIZE = 4
NORM_EPS = 1e-6
GN_EPS = 64e-5
KK_EPS = 1e-12
DECAY_SCALE = math.exp(-0.5)

TOKEN_TILE = 1024
MOE_TILE = 512
MOE_BLOCK = 128
BLK_STRIDE = 16
SCAN_CHUNK = 64
SCAN_HEADS = 4
SCAN_PAR = 2
SCAN_SUB = 4
SCAN_STAGGER = 4
MOD_COLS = 1536
VMEM_LIMIT = 56 * 1024 * 1024


def _dot(a, b):
    return jnp.dot(a, b, preferred_element_type=F32)


def _dot_nt(a, b):
    return lax.dot_general(a, b, (((1,), (1,)), ((), ())), preferred_element_type=F32)


def _split(x):
    hi = x.astype(BF16)
    lo = (x - hi.astype(F32)).astype(BF16)
    return hi, lo


def _seg_sum(x, seg):
    return _dot(x.astype(BF16), seg)


def _sigmoid(x):
    return 1.0 / (1.0 + jnp.exp(-x))


def _norm_mod(x, gain, shift, scale):
    ms = jnp.mean(x * x, axis=-1, keepdims=True)
    return (x * lax.rsqrt(ms + NORM_EPS) * gain) * (1.0 + scale) + shift


def _const_spec(shape):
    nd = len(shape)
    return pl.BlockSpec(shape, lambda *_: (0,) * nd)


def _mod_kernel(c_ref, w_ref, b_ref, o_ref):
    c = c_ref[...]
    s = c * _sigmoid(c)
    o_ref[0] = _dot(s.astype(BF16), w_ref[0].astype(BF16)) + b_ref[0]


def _modulation(cvec, ada_w, ada_b):
    n_layer, d, cols = ada_w.shape
    rows = cvec.shape[0]
    return pl.pallas_call(
        _mod_kernel,
        grid=(n_layer, cols // MOD_COLS),
        in_specs=[
            pl.BlockSpec((rows, d), lambda l, j: (0, 0)),
            pl.BlockSpec((1, d, MOD_COLS), lambda l, j: (l, 0, j)),
            pl.BlockSpec((1, 1, MOD_COLS), lambda l, j: (l, 0, j)),
        ],
        out_specs=pl.BlockSpec((1, rows, MOD_COLS), lambda l, j: (l, 0, j)),
        out_shape=jax.ShapeDtypeStruct((n_layer, rows, cols), F32),
        compiler_params=pltpu.CompilerParams(
            dimension_semantics=("arbitrary", "arbitrary"), vmem_limit_bytes=VMEM_LIMIT),
        name="mod",
    )(cvec, ada_w, ada_b.reshape(n_layer, 1, cols))


def _prep_kernel(x_ref, mod_ref, g1_ref, win_ref, w0_ref, w2_ref, a0_ref, a2_ref, g2_ref,
                 kk_ref, ka_ref, rk_ref, seg_ref,
                 r_o, k_o, v_o, kk_o, lw_o, aa_o, bonus_o, g_o, *, row0, tiles_per_seq):
    row = row0 + pl.program_id(0) // tiles_per_seq
    x = x_ref[...]
    d = x.shape[1]
    c = r_o.shape[1]
    shift = mod_ref[pl.ds(row, 1), 0:d]
    scale = mod_ref[pl.ds(row, 1), d:2 * d]
    h = _norm_mod(x, g1_ref[...], shift, scale)
    z = _dot(h.astype(BF16), win_ref[...])
    r = z[:, 0:c]
    k = z[:, c:2 * c]
    v = z[:, 2 * c:3 * c]
    lo = 3 * c
    w_lo = z[:, lo:lo + LANE_TILE]
    a_lo = z[:, lo + LANE_TILE:lo + 2 * LANE_TILE]
    g_lo = z[:, lo + 2 * LANE_TILE:lo + 3 * LANE_TILE]
    w_logit = w0_ref[...] + _dot(jnp.tanh(w_lo).astype(BF16), w2_ref[...])
    lw = -DECAY_SCALE * _sigmoid(w_logit)
    aa = _sigmoid(a0_ref[...] + _dot(a_lo.astype(BF16), a2_ref[...]))
    g = _dot(_sigmoid(g_lo).astype(BF16), g2_ref[...])
    seg = seg_ref[...]
    kk0 = k * kk_ref[...]
    kk = kk0 / jnp.maximum(jnp.sqrt(_seg_sum(kk0 * kk0, seg)), KK_EPS)
    ka = ka_ref[...]
    k_dirs = k * (1.0 + (aa[:, 0:c] - 1.0) * ka) + k * (1.0 + (aa[:, c:2 * c] - 1.0) * ka)
    rk = _seg_sum(r * k_dirs * rk_ref[...], seg)
    r_o[...] = r
    k_o[...] = k
    v_o[...] = v.astype(BF16)
    kk_o[...] = kk
    lw_o[0] = lw[:, 0:c]
    lw_o[1] = lw[:, c:2 * c]
    aa_o[0] = aa[:, 0:c]
    aa_o[1] = aa[:, c:2 * c]
    bonus_o[...] = (rk * v).astype(BF16)
    g_o[...] = g.astype(BF16)


def _prep(rows, x, mod, g1, win_r, w0, w2, a0, a2, g2, k_k, k_a, r_k, seg):
    n, d = x.shape
    c = seg.shape[0]
    tm = TOKEN_TILE
    tok = lambda w: pl.BlockSpec((tm, w), lambda i: (i, 0))
    tok2 = pl.BlockSpec((2, tm, c), lambda i: (0, i, 0))
    consts = [mod, g1, win_r, w0, w2, a0, a2, g2, k_k, k_a, r_k, seg]
    sds = jax.ShapeDtypeStruct
    return pl.pallas_call(
        functools.partial(_prep_kernel, row0=rows[0], tiles_per_seq=rows[1] // tm),
        grid=(n // tm,),
        in_specs=[tok(d)] + [_const_spec(a.shape) for a in consts],
        out_specs=[tok(c), tok(c), tok(c), tok(c), tok2, tok2, tok(c), tok(c)],
        out_shape=[sds((n, c), F32), sds((n, c), F32), sds((n, c), BF16), sds((n, c), F32)]
        + [sds((2, n, c), F32)] * 2 + [sds((n, c), BF16)] * 2,
        compiler_params=pltpu.CompilerParams(
            dimension_semantics=("arbitrary",), vmem_limit_bytes=VMEM_LIMIT),
        name="prep",
    )(x, *consts)


def _branch_kernel(x_ref, mod_ref, g1_ref, win_ref, cw_ref, lng_ref, lnb_ref,
                   ws_ref, bs_ref, pc_ref, ps_ref, part_o, grw_o, *, row0, tiles_per_seq, cols):
    row = row0 + pl.program_id(0) // tiles_per_seq
    x = x_ref[...]
    tm, d = x.shape
    cw_w = cw_ref.shape[1]
    sg_w = lng_ref.shape[1]
    chunk = ws_ref.shape[1]
    n_grp = ws_ref.shape[0]
    shift = mod_ref[pl.ds(row, 1), 0:d]
    scale = mod_ref[pl.ds(row, 1), d:2 * d]
    h = _norm_mod(x, g1_ref[...], shift, scale)
    z = _dot(h.astype(BF16), win_ref[...])
    cb = z[:, 0:cw_w]
    cc = z[:, cw_w:2 * cw_w]
    cx = z[:, 2 * cw_w:3 * cw_w]
    o = 3 * cw_w
    su = z[:, o:o + sg_w]
    sv = z[:, o + sg_w:o + 2 * sg_w]
    gates = _sigmoid(z[:, o + 2 * sg_w:])
    xc = cc * cx
    col = lax.broadcasted_iota(jnp.int32, xc.shape, 0) & (cols - 1)
    prev = jnp.where(col == 0, 0.0, pltpu.roll(xc, 1, 0))
    nxt = jnp.where(col == cols - 1, 0.0, pltpu.roll(xc, tm - 1, 0))
    cw = cw_ref[...]
    y_conv = cb * (cw[0:1] * prev + cw[1:2] * xc + cw[2:3] * nxt)
    mu = jnp.mean(sv, axis=-1, keepdims=True)
    dv = sv - mu
    var = jnp.mean(dv * dv, axis=-1, keepdims=True)
    vn = (dv * lax.rsqrt(var + NORM_EPS) * lng_ref[...] + lnb_ref[...]).astype(BF16)
    lane_grp = lax.broadcasted_iota(jnp.int32, (chunk, sg_w), 1) // (sg_w // n_grp)
    parts = []
    for ci in range(tm // chunk):
        vc = vn[ci * chunk:(ci + 1) * chunk]
        mixed = bs_ref[...]
        for gi in range(n_grp):
            mixed = mixed + jnp.where(lane_grp == gi, _dot(ws_ref[gi], vc), 0.0)
        parts.append(su[ci * chunk:(ci + 1) * chunk] * mixed)
    y_sgu = jnp.concatenate(parts, axis=0)
    part_o[...] = (gates[:, 0:d] * _dot(y_conv.astype(BF16), pc_ref[...])
                   + gates[:, 2 * d:3 * d] * _dot(y_sgu.astype(BF16), ps_ref[...])).astype(BF16)
    grw_o[...] = gates[:, d:2 * d].astype(BF16)


def _branch(rows, cols, x, mod, g1, win_c, conv_w, ln_g, ln_b, ws, bs_full, proj_conv, proj_sgu):
    n, d = x.shape
    tm = TOKEN_TILE
    tok = pl.BlockSpec((tm, d), lambda i: (i, 0))
    consts = [mod, g1, win_c, conv_w, ln_g, ln_b, ws, bs_full, proj_conv, proj_sgu]
    return pl.pallas_call(
        functools.partial(_branch_kernel, row0=rows[0], tiles_per_seq=rows[1] // tm, cols=cols),
        grid=(n // tm,),
        in_specs=[tok] + [_const_spec(a.shape) for a in consts],
        out_specs=[tok, tok],
        out_shape=[jax.ShapeDtypeStruct((n, d), BF16)] * 2,
        compiler_params=pltpu.CompilerParams(
            dimension_semantics=("arbitrary",), vmem_limit_bytes=VMEM_LIMIT),
        name="branch",
    )(x, *consts)


def _scan_constants(n_l, hs):
    lanes = SCAN_HEADS * hs
    t = np.arange(n_l)[:, None]
    s = np.arange(lanes)[None, :] % n_l
    ti, si = np.arange(n_l)[:, None], np.arange(n_l)[None, :]
    n_lvl = n_l.bit_length() - 1
    tri, masks, lvl = [], [], []
    for reverse in (False, True):
        if not reverse:
            level = lambda q: (((t >> q) & 1) == 1) & ((s >> q) == (t >> q) - 1)
            tri.append(ti >= si)
            masks.append(np.concatenate([t > s, t >= s, level(0)], axis=0))
        else:
            level = lambda q: (((t >> q) & 1) == 0) & ((s >> q) == (t >> q) + 1)
            tri.append(ti <= si)
            masks.append(np.concatenate([t < s, t <= s, level(0)], axis=0))
        lvl.append([level(q) for q in range(1, n_lvl)])
    as_j = lambda a, dt: jnp.asarray(np.asarray(a, np.float32), dt)
    return as_j(t == s, F32), as_j(masks, F32), as_j(lvl, BF16), as_j(tri, BF16)


class _HeadBlocks:
    def __init__(self, hs, lanes):
        self.hs, self.lanes = hs, lanes
        self.n_tiles = lanes // LANE_TILE
        self.per_tile = LANE_TILE // hs
        self.heads = lanes // hs
        lane_head = lax.broadcasted_iota(jnp.int32, (hs, LANE_TILE), 1) // hs
        self.half_f32 = [(lane_head == j).astype(F32) for j in range(self.per_tile)]
        self.half_bf16 = [m.astype(BF16) for m in self.half_f32]
        row_head = lax.broadcasted_iota(jnp.int32, (LANE_TILE, LANE_TILE), 0) // hs
        col_head = lax.broadcasted_iota(jnp.int32, (LANE_TILE, LANE_TILE), 1) // hs
        self.tile_mask = (row_head == col_head).astype(F32)

    def expand(self, compact):
        zero = jnp.zeros((self.hs, LANE_TILE), compact.dtype)
        rows = []
        for h in range(self.heads):
            tiles = [zero] * self.n_tiles
            tiles[h // self.per_tile] = compact[h * self.hs:(h + 1) * self.hs]
            rows.append(jnp.concatenate(tiles, axis=1))
        return jnp.concatenate(rows, axis=0)

    def compact(self, x, masks):
        return jnp.concatenate(
            [x[:, (h // self.per_tile) * LANE_TILE:(h // self.per_tile + 1) * LANE_TILE] * masks[h % self.per_tile]
             for h in range(self.heads)], axis=0)

    def diag(self, full, masks):
        return jnp.concatenate(
            [full[h * self.hs:(h + 1) * self.hs,
                  (h // self.per_tile) * LANE_TILE:(h // self.per_tile + 1) * LANE_TILE] * masks[h % self.per_tile]
             for h in range(self.heads)], axis=0)

    def side_by_side(self, compact):
        tiles = []
        for t in range(self.n_tiles):
            acc = compact[t * self.per_tile * self.hs:(t * self.per_tile + 1) * self.hs]
            for j in range(1, self.per_tile):
                h = t * self.per_tile + j
                acc = acc + compact[h * self.hs:(h + 1) * self.hs]
            tiles.append(acc)
        return jnp.concatenate(tiles, axis=1)

    def bd(self, x):
        return self.expand(self.compact(x.astype(BF16), self.half_bf16))

    def bd_t(self, x):
        tiles = []
        for t in range(self.n_tiles):
            square = jnp.concatenate([x[:, t * LANE_TILE:(t + 1) * LANE_TILE]] * self.per_tile, axis=0)
            tiles.append((square.T * self.tile_mask).astype(BF16))
        return self.expand(jnp.concatenate(tiles, axis=0))


def _scan_prepare(streams, hb, eye, masks_ref, lvl_ref, tri_ref, out):
    n_l = streams[0][0].shape[0]
    ns = range(len(streams))
    bd = hb.bd
    r, k, v, kk, lw, aa, ka, di = map(list, zip(*streams))
    tri = lambda i: tri_ref[di[i]]
    tril_mask = lambda i: masks_ref[di[i], 0:2 * n_l]
    end = [0 if d else n_l - 1 for d in di]
    l_hi = [lw[i].astype(BF16) for i in ns]
    l_mid = [(lw[i] - l_hi[i].astype(F32)).astype(BF16) for i in ns]
    l_lo = [(lw[i] - l_hi[i].astype(F32) - l_mid[i].astype(F32)).astype(BF16) for i in ns]
    c = [_dot(tri(i), l_hi[i]) + _dot(tri(i), l_mid[i]) + _dot(tri(i), l_lo[i]) for i in ns]
    yield
    b_vec = [kk[i] * aa[i] for i in ns]
    kd = [k[i] * (1.0 + (aa[i] - 1.0) * ka[i]) for i in ns]
    c_end = [c[i][end[i]:end[i] + 1] for i in ns]
    e_neg = [jnp.exp(-c[i]) for i in ns]
    lhs = [jnp.concatenate([-kk[i] * jnp.exp(c[i] - lw[i]), r[i] * jnp.exp(c[i])], axis=0).astype(BF16)
           for i in ns]
    yield
    m_b = [_dot(lhs[i], hb.bd_t(b_vec[i] * e_neg[i])) * tril_mask(i) for i in ns]
    yield
    m_k = [(_dot(lhs[i], hb.bd_t(kd[i] * e_neg[i])) * tril_mask(i)).astype(BF16) for i in ns]
    yield
    a_ab = [m_b[i][0:n_l] for i in ns]
    t_inv = [eye + a_ab[i] * masks_ref[di[i], 2 * n_l:3 * n_l] for i in ns]
    a_bf = [a_ab[i].astype(BF16) for i in ns]
    for q in range(lvl_ref.shape[1]):
        x1 = [_dot(t_inv[i].astype(BF16), bd(a_bf[i] * lvl_ref[di[i], q])) for i in ns]
        yield
        x2 = [_dot(x1[i].astype(BF16), bd(t_inv[i])) for i in ns]
        t_inv = [t_inv[i] + x2[i] for i in ns]
        yield
    v_k = [_dot(m_k[i], bd(v[i])) for i in ns]
    e_end = [jnp.exp(c_end[i] - c[i]) for i in ns]
    bk_t = [jnp.concatenate([b_vec[i] * e_end[i], kd[i] * e_end[i]], axis=0).T.astype(BF16) for i in ns]
    decay_col = [jnp.broadcast_to(jnp.exp(c_end[i]), (LANE_TILE, hb.lanes)).T for i in ns]
    out.extend((lhs[i], v_k[i], t_inv[i], m_b[i][n_l:].astype(BF16), v[i], bk_t[i], decay_col[i]) for i in ns)


def _scan_apply(pre, z, hb, out):
    lhs, v_k, t_inv, a_rb, v, bk_t, decay_col = map(list, zip(*pre))
    n_l = t_inv[0].shape[0]
    ns = range(len(pre))
    v_part = [_dot(lhs[i], hb.expand(z[i].astype(BF16))) + v_k[i] for i in ns]
    yield
    u = [_dot(t_inv[i].astype(BF16), hb.bd(v_part[i][0:n_l])) for i in ns]
    yield
    y = [v_part[i][n_l:] + _dot(a_rb[i], hb.bd(u[i])) for i in ns]
    yield
    uv = [jnp.concatenate([u[i].astype(BF16), v[i].astype(BF16)], axis=0) for i in ns]
    z_new = [z[i] * decay_col[i] + hb.diag(_dot(bk_t[i], uv[i]), hb.half_f32) for i in ns]
    out.extend(zip(y, z_new))


def _scan_kernel(rf_ref, kf_ref, vf_ref, kkf_ref, lwf_ref, aaf_ref,
                 rb_ref, kb_ref, vb_ref, kkb_ref, lwb_ref, aab_ref,
                 ka_ref, s0_ref, eye_ref, masks_ref, lvl_ref, tri_ref,
                 yf_o, yb_o, sfin_o, z_scr):
    j = pl.program_id(1)
    n_par, n_rows, c = rf_ref.shape
    n_l = eye_ref.shape[0]
    lanes = z_scr.shape[1]
    n_str = c // lanes
    hb = _HeadBlocks(lanes // SCAN_HEADS, lanes)
    slots = [(p, di, hg) for p in range(n_par) for di in range(2) for hg in range(n_str)]

    @pl.when(j == 0)
    def _():
        for zi, (p, di, hg) in enumerate(slots):
            z_scr[zi] = hb.compact(s0_ref[p, 0, di, hg], hb.half_f32)

    ka = ka_ref[...]
    dirs = ((rf_ref, kf_ref, vf_ref, kkf_ref, lwf_ref, aaf_ref, yf_o),
            (rb_ref, kb_ref, vb_ref, kkb_ref, lwb_ref, aab_ref, yb_o))
    n_sub = n_rows // n_l
    streams = []
    for sub in range(n_sub):
        for p, di, hg in slots:
            r_ref, k_ref, v_ref, kk_ref, lw_ref, aa_ref, _ = dirs[di]
            rs = slice((n_sub - 1 - sub if di else sub) * n_l, (n_sub - sub if di else sub + 1) * n_l)
            ls = slice(hg * lanes, (hg + 1) * lanes)
            streams.append((r_ref[p, rs, ls], k_ref[p, rs, ls], v_ref[p, rs, ls], kk_ref[p, rs, ls],
                            lw_ref[0, p, rs, ls], aa_ref[0, p, rs, ls], ka[:, ls], di))
    n_slot = len(slots)
    eye = eye_ref[...]
    pre = [[] for _ in range(n_sub)]
    res = [[] for _ in range(n_sub)]
    z0 = [z_scr[zi] for zi in range(n_slot)]

    gens = [_scan_prepare(streams[s * n_slot:(s + 1) * n_slot], hb, eye, masks_ref, lvl_ref, tri_ref, pre[s])
            for s in range(n_sub)]
    for _ in range(SCAN_STAGGER):
        next(gens[0])
    for s in range(n_sub):
        nxt = gens[s + 1] if s + 1 < n_sub else None
        for _ in gens[s]:
            if nxt is not None:
                next(nxt, None)
        z_in = z0 if s == 0 else [o[1] for o in res[s - 1]]
        for _ in _scan_apply(pre[s], z_in, hb, res[s]):
            if nxt is not None:
                next(nxt, None)
        for (y, _), (p, di, hg) in zip(res[s], slots):
            rs = slice((n_sub - 1 - s if di else s) * n_l, (n_sub - s if di else s + 1) * n_l)
            dirs[di][6][p, rs, hg * lanes:(hg + 1) * lanes] = y
    z = [o[1] for o in res[n_sub - 1]]
    for zi in range(len(slots)):
        z_scr[zi] = z[zi]

    @pl.when(j == pl.num_programs(1) - 1)
    def _():
        for zi, (p, di, hg) in enumerate(slots):
            sfin_o[p, 0, di, hg] = hb.side_by_side(z_scr[zi])


def _scan(t_len, r, k, v, kk, lw, aa, k_a, s0, consts):
    n, c = r.shape
    n_seq = s0.shape[0]
    n_par = SCAN_PAR
    n_l = SCAN_CHUNK * SCAN_SUB
    n_c = t_len // n_l
    hs = s0.shape[3]
    lanes = SCAN_HEADS * hs
    n_str = c // lanes
    part = lambda a: a.reshape(n_par, n // n_par, c)
    part2 = lambda a: a.reshape(2, n_par, n // n_par, c)
    s0p = s0.reshape((n_par, n_seq // n_par) + s0.shape[1:])
    fwd = pl.BlockSpec((n_par, n_l, c), lambda q, j: (0, q * n_c + j, 0))
    bwd = pl.BlockSpec((n_par, n_l, c), lambda q, j: (0, q * n_c + n_c - 1 - j, 0))
    fwd2 = pl.BlockSpec((1, n_par, n_l, c), lambda q, j: (0, 0, q * n_c + j, 0))
    bwd2 = pl.BlockSpec((1, n_par, n_l, c), lambda q, j: (1, 0, q * n_c + n_c - 1 - j, 0))
    st = pl.BlockSpec((n_par, 1, 2, n_str, hs, lanes), lambda q, j: (0, q, 0, 0, 0, 0))
    sds = jax.ShapeDtypeStruct
    args = [part(r), part(k), part(v), part(kk), part2(lw), part2(aa)]
    yf, yb, s_fin = pl.pallas_call(
        _scan_kernel,
        grid=(n_seq // n_par, n_c),
        in_specs=[fwd, fwd, fwd, fwd, fwd2, fwd2, bwd, bwd, bwd, bwd, bwd2, bwd2,
                  _const_spec(k_a.shape), st] + [_const_spec(a.shape) for a in consts],
        out_specs=[fwd, bwd, st],
        out_shape=[sds((n_par, n // n_par, c), F32), sds((n_par, n // n_par, c), F32), sds(s0p.shape, F32)],
        scratch_shapes=[pltpu.VMEM((n_par * 2 * n_str, lanes, LANE_TILE), F32)],
        compiler_params=pltpu.CompilerParams(
            dimension_semantics=("arbitrary", "arbitrary"), vmem_limit_bytes=VMEM_LIMIT),
        name="scan",
    )(*args, *args, k_a, s0p, *consts)
    return yf.reshape(n, c), yb.reshape(n, c), s_fin.reshape(s0.shape)


def _merge_kernel(x_ref, yf_ref, yb_ref, bonus_ref, g_ref, part_ref, grw_ref, mod_ref,
                  lnw_ref, lnb_ref, seg_ref, prw_ref, wout_ref, g2_ref, rwh_ref, rwl_ref,
                  x1_o, h2_o, lg_o, *, head_size, row0, tiles_per_seq):
    row = row0 + pl.program_id(0) // tiles_per_seq
    x = x_ref[...]
    d = x.shape[1]
    seg = seg_ref[...]
    y = yf_ref[...] + yb_ref[...]
    hs_inv = 1.0 / head_size
    mu = _seg_sum(y, seg) * hs_inv
    dy = y - mu
    var = _seg_sum(dy * dy, seg) * hs_inv
    yn = dy * lax.rsqrt(var + GN_EPS) * lnw_ref[...] + lnb_ref[...]
    y_rwkv = (yn + bonus_ref[...]) * g_ref[...]
    merged = part_ref[...] + grw_ref[...] * _dot(y_rwkv.astype(BF16), prw_ref[...])
    mix = _dot(merged.astype(BF16), wout_ref[...])
    gate1 = mod_ref[pl.ds(row, 1), 2 * d:3 * d]
    shift2 = mod_ref[pl.ds(row, 1), 3 * d:4 * d]
    scale2 = mod_ref[pl.ds(row, 1), 4 * d:5 * d]
    x1 = x + gate1 * mix
    h2 = _norm_mod(x1, g2_ref[...], shift2, scale2)
    h_hi, h_lo = _split(h2)
    rwh = rwh_ref[...]
    n_e = rwh.shape[0]
    both = _dot_nt(jnp.concatenate([rwh, rwl_ref[...]], axis=0), h_hi)
    x1_o[...] = x1
    h2_o[...] = h_hi
    lg_o[...] = both[0:n_e] + both[n_e:] + _dot_nt(rwh, h_lo)


def _merge(rows, x, yf, yb, bonus, g, part, grw, mod, lnw, lnb, seg, proj_rwkv, w_out, g2n, rw_hi, rw_lo,
           head_size):
    n, d = x.shape
    c = seg.shape[0]
    n_e = rw_hi.shape[0]
    tm = TOKEN_TILE
    tok = lambda w: pl.BlockSpec((tm, w), lambda i: (i, 0))
    consts = [mod, lnw, lnb, seg, proj_rwkv, w_out, g2n, rw_hi, rw_lo]
    sds = jax.ShapeDtypeStruct
    return pl.pallas_call(
        functools.partial(_merge_kernel, head_size=head_size, row0=rows[0], tiles_per_seq=rows[1] // tm),
        grid=(n // tm,),
        in_specs=[tok(d), tok(c), tok(c), tok(c), tok(c), tok(d), tok(d)]
        + [_const_spec(a.shape) for a in consts],
        out_specs=[tok(d), tok(d), pl.BlockSpec((n_e, tm), lambda i: (0, i))],
        out_shape=[sds((n, d), F32), sds((n, d), BF16), sds((n_e, n), F32)],
        compiler_params=pltpu.CompilerParams(
            dimension_semantics=("arbitrary",), vmem_limit_bytes=VMEM_LIMIT),
        name="merge",
    )(x, yf, yb, bonus, g, part, grw, *consts)


def _route_kernel(lg_ref, b_ref, tri_ref, g4_o, pos_o, blk_o):
    scores = _sigmoid(lg_ref[...])
    biased = scores + b_ref[...]
    n_e, n_t = scores.shape
    rows = [biased[e:e + 1] for e in range(n_e)]
    srow = [scores[e:e + 1] for e in range(n_e)]

    def top2_sum(a, b, c, d):
        s1, t1 = jnp.maximum(a, b), jnp.minimum(a, b)
        s2, t2 = jnp.maximum(c, d), jnp.minimum(c, d)
        return jnp.maximum(s1, s2) + jnp.maximum(jnp.minimum(s1, s2), jnp.maximum(t1, t2))

    best = top2_sum(*rows[0:GROUP_SIZE])
    g_sel = jnp.zeros(best.shape, jnp.int32)
    for gi in range(1, N_GROUPS):
        gs = top2_sum(*rows[gi * GROUP_SIZE:(gi + 1) * GROUP_SIZE])
        better = gs > best
        best = jnp.where(better, gs, best)
        g_sel = jnp.where(better, gi, g_sel)

    def pick(src, j):
        out = src[j]
        for gi in range(1, N_GROUPS):
            out = jnp.where(g_sel == gi, src[gi * GROUP_SIZE + j], out)
        return out

    in_b = [pick(rows, j) for j in range(GROUP_SIZE)]
    in_s = [pick(srow, j) for j in range(GROUP_SIZE)]

    def arg_first_max(vals):
        best_v = vals[0]
        best_i = jnp.zeros(best_v.shape, jnp.int32)
        for j in range(1, len(vals)):
            better = vals[j] > best_v
            best_v = jnp.where(better, vals[j], best_v)
            best_i = jnp.where(better, j, best_i)
        return best_i

    i1 = arg_first_max(in_b)
    i2 = arg_first_max([jnp.where(i1 == j, -jnp.inf, in_b[j]) for j in range(GROUP_SIZE)])

    def take(src, idx):
        out = src[0]
        for j in range(1, GROUP_SIZE):
            out = jnp.where(idx == j, src[j], out)
        return out

    g1, g2 = take(in_s, i1), take(in_s, i2)
    den = g1 + g2
    j_idx = lax.broadcasted_iota(jnp.int32, (g4_o.shape[0], n_t), 0)
    g4_o[...] = jnp.where(j_idx == i1, g1 / den, jnp.where(j_idx == i2, g2 / den, 0.0))

    grp = lax.broadcasted_iota(jnp.int32, (n_e, n_t), 0)
    onehot = (grp == g_sel).astype(F32)
    rank = _dot(onehot.astype(BF16), tri_ref[...])
    cnt = jnp.sum(onehot, axis=1, keepdims=True).astype(jnp.int32)
    blocks = jnp.zeros(cnt.shape, jnp.int32)
    for kb in range(-(-n_t // MOE_BLOCK)):
        blocks = blocks + (cnt > kb * MOE_BLOCK).astype(jnp.int32)
    used = jnp.zeros((1, 1), jnp.int32)
    pos = jnp.zeros((1, n_t), jnp.int32)
    lane = lax.broadcasted_iota(jnp.int32, (1, blk_o.shape[2]), 1)
    blk_grp = jnp.zeros(lane.shape, jnp.int32)
    for gi in range(N_GROUPS):
        pos = pos + jnp.where(g_sel == gi, used * MOE_BLOCK + rank[gi:gi + 1].astype(jnp.int32), 0)
        used = used + blocks[gi:gi + 1]
        if gi < N_GROUPS - 1:
            blk_grp = blk_grp + (used <= lane).astype(jnp.int32)
    pos_o[...] = pos
    n_blk = _moe_blocks(n_t)
    blk_o[0] = jnp.where(lane < n_blk, blk_grp, jnp.where(lane == n_blk, used, 0))


def _moe_blocks(tile):
    return -(-(tile + N_GROUPS * (MOE_BLOCK - 1)) // MOE_BLOCK)


def _route(logits_t, bias_b, tri):
    n_e, n = logits_t.shape
    rt = MOE_TILE
    sds = jax.ShapeDtypeStruct
    return pl.pallas_call(
        _route_kernel,
        grid=(n // rt,),
        in_specs=[pl.BlockSpec((n_e, rt), lambda i: (0, i)), _const_spec(bias_b.shape), _const_spec(tri.shape)],
        out_specs=[pl.BlockSpec((SUBLANE_TILE, rt), lambda i: (0, i)), pl.BlockSpec((1, rt), lambda i: (0, i)),
                   pl.BlockSpec((1, 1, LANE_TILE), lambda i: (i, 0, 0))],
        out_shape=[sds((SUBLANE_TILE, n), F32), sds((1, n), jnp.int32), sds((n // rt, 1, LANE_TILE), jnp.int32)],
        compiler_params=pltpu.CompilerParams(dimension_semantics=("arbitrary",)),
        name="route",
    )(logits_t, bias_b, tri)


def _moe_kernel(blk_ref, h_ref, posr_ref, posc_ref, ghi_ref, glo_ref, x1_ref, mod_ref, w1_ref, w3_ref, w2_ref,
                fg_ref, o_ref, xs_ref, ys_ref, *, final_norm, row0, tiles_per_seq):
    i = pl.program_id(0)
    tm, d = x1_ref.shape
    r_rows = xs_ref.shape[0]
    n_blk = r_rows // MOE_BLOCK
    sel = jnp.where(lax.broadcasted_iota(jnp.int32, (r_rows, tm), 0) == posr_ref[...], 1.0, 0.0).astype(BF16)
    gathered = _dot(sel, jnp.concatenate([h_ref[...], ghi_ref[...], glo_ref[...]], axis=1))
    xs_ref[...] = gathered[:, 0:d].astype(BF16)
    gs = gathered[:, d:d + LANE_TILE] + gathered[:, d + LANE_TILE:]
    base = i * BLK_STRIDE
    n_used = blk_ref[base + n_blk]
    for b in range(n_blk):
        rows = slice(b * MOE_BLOCK, (b + 1) * MOE_BLOCK)

        @pl.when(b < n_used)
        def _():
            first = blk_ref[base + b] * GROUP_SIZE
            xb = xs_ref[rows]
            experts = range(GROUP_SIZE)
            a = [_dot(xb, w1_ref[first + j]) for j in experts]
            g = [_dot(xb, w3_ref[first + j]) for j in experts]
            hid = [((a[j] * _sigmoid(a[j])) * g[j]).astype(BF16) for j in experts]
            out = [gs[rows, j:j + 1] * _dot(hid[j], w2_ref[first + j]) for j in experts]
            ys_ref[rows] = ((out[0] + out[1]) + (out[2] + out[3])).astype(BF16)

        @pl.when(b >= n_used)
        def _():
            ys_ref[rows] = jnp.zeros((MOE_BLOCK, d), BF16)

    selt = jnp.where(lax.broadcasted_iota(jnp.int32, (tm, r_rows), 1) == posc_ref[...], 1.0, 0.0).astype(BF16)
    row = row0 + i // tiles_per_seq
    gate2 = mod_ref[pl.ds(row, 1), 5 * d:6 * d]
    x2 = x1_ref[...] + gate2 * _dot(selt, ys_ref[...])
    if final_norm:
        ms = jnp.mean(x2 * x2, axis=-1, keepdims=True)
        x2 = x2 * lax.rsqrt(ms + NORM_EPS) * fg_ref[...]
    o_ref[...] = x2


def _moe(rows, blk, h2, pos_row, pos_col, g_hi, g_lo, x1, mod, w1, w3, w2, final_g, final_norm):
    n, d = x1.shape
    tm = MOE_TILE
    r_rows = _moe_blocks(tm) * MOE_BLOCK
    tok = lambda w: pl.BlockSpec((tm, w), lambda i, *_: (i, 0))
    resident = lambda a: pl.BlockSpec(a.shape, lambda i, *_: (0,) * a.ndim, pipeline_mode=pl.Buffered(1))
    grid_spec = pltpu.PrefetchScalarGridSpec(
        num_scalar_prefetch=1,
        grid=(n // tm,),
        in_specs=[tok(d), pl.BlockSpec((1, tm), lambda i, *_: (0, i)), tok(1), tok(g_hi.shape[1]),
                  tok(g_lo.shape[1]), tok(d), _const_spec(mod.shape), resident(w1), resident(w3), resident(w2),
                  _const_spec(final_g.shape)],
        out_specs=tok(d),
        scratch_shapes=[pltpu.VMEM((r_rows, d), BF16), pltpu.VMEM((r_rows, d), BF16)],
    )
    return pl.pallas_call(
        functools.partial(_moe_kernel, final_norm=final_norm, row0=rows[0],
                          tiles_per_seq=max(rows[1] // tm, 1)),
        grid_spec=grid_spec,
        out_shape=jax.ShapeDtypeStruct((n, d), F32),
        compiler_params=pltpu.CompilerParams(
            dimension_semantics=("arbitrary",), vmem_limit_bytes=VMEM_LIMIT),
        name="moe",
    )(blk, h2, pos_row, pos_col, g_hi, g_lo, x1, mod, w1, w3, w2, final_g)


def kernel(x_prompt, x_sample, state_rwkv, c, c_ctx, norm1_g, norm2_g, final_g, ada_w, ada_b, w_in,
           conv_w, rwkv_w0, rwkv_w2, rwkv_a0, rwkv_a2, rwkv_g2, rwkv_k_k, rwkv_k_a, rwkv_r_k,
           rwkv_lnx_w, rwkv_lnx_b, sgu_ln_g, sgu_ln_b, sgu_ws, sgu_bs, proj_conv, proj_rwkv, proj_sgu,
           w_out, router_w, router_b, exp_w1, exp_w3, exp_w2):
    n_ctx, t_ctx, d = x_prompt.shape
    n_lat, t_lat, _ = x_sample.shape
    n_layer = w_in.shape[0]
    _, n_head, hs = rwkv_r_k.shape
    c_rw = n_head * hs
    conv_c = conv_w.shape[2]
    sgu_c = sgu_ln_g.shape[1]
    n_grp, chunk, _ = sgu_ws.shape[1:]
    lora_w = rwkv_w2.shape[2]
    lora_a = rwkv_a2.shape[2]
    lora_g = rwkv_g2.shape[1]
    n_exp = router_w.shape[1]
    assert 2 * lora_w == LANE_TILE and 2 * lora_a == LANE_TILE and lora_g == LANE_TILE
    assert t_ctx & (t_ctx - 1) == 0 and GRID_W & (GRID_W - 1) == 0
    assert TOKEN_TILE % t_ctx == 0 and t_lat % TOKEN_TILE == 0 and TOKEN_TILE % GRID_W == 0
    assert t_lat % MOE_TILE == 0 and (n_ctx * t_ctx) % MOE_TILE == 0 and TOKEN_TILE % chunk == 0
    assert n_exp == N_GROUPS * GROUP_SIZE and c_rw % (SCAN_HEADS * hs) == 0
    assert SCAN_CHUNK == hs and t_ctx % (SCAN_CHUNK * SCAN_SUB) == 0 and t_lat % (SCAN_CHUNK * SCAN_SUB) == 0
    assert n_ctx % SCAN_PAR == 0 and n_lat % SCAN_PAR == 0
    assert _moe_blocks(MOE_TILE) < BLK_STRIDE and MOE_BLOCK % 16 == 0

    mod_rows = SUBLANE_TILE * (-(-(1 + n_lat) // SUBLANE_TILE))
    cvec = jnp.zeros((mod_rows, d), F32).at[0].set(c_ctx).at[1:1 + n_lat].set(c)
    mod = _modulation(cvec, ada_w, ada_b)

    sizes = (conv_c, conv_c, conv_c, c_rw, c_rw, c_rw, 2 * lora_w, 2 * lora_a, lora_g, sgu_c, sgu_c, 3 * d)
    offs = np.concatenate([[0], np.cumsum(sizes)])
    assert offs[-1] == w_in.shape[2]
    rw_lo, rw_hi = int(offs[3]), int(offs[9])
    seg = jnp.asarray(np.kron(np.eye(n_head), np.ones((hs, hs))), BF16)
    n_str = c_rw // (SCAN_HEADS * hs)

    def side_by_side(s):
        s = jnp.swapaxes(s, -1, -2).reshape(s.shape[0], 2, n_str, SCAN_HEADS, hs, hs)
        return jnp.swapaxes(s, 3, 4).reshape(s.shape[0], 2, n_str, hs, SCAN_HEADS * hs)

    def from_side_by_side(s):
        s = s.reshape(s.shape[0], 2, n_str, hs, SCAN_HEADS, hs)
        return jnp.swapaxes(jnp.swapaxes(s, 3, 4).reshape(s.shape[0], 2, n_head, hs, hs), -1, -2)

    rw_t = router_w.T
    rw_hi_b = rw_t.astype(BF16)
    rw_lo_b = (rw_t - rw_hi_b.astype(F32)).astype(BF16)
    bias_b = jnp.broadcast_to(router_b.astype(F32)[:, None], (n_exp, MOE_TILE))
    tri = jnp.asarray(np.triu(np.ones((MOE_TILE, MOE_TILE)), 1), BF16)
    final_g2 = final_g.reshape(1, d)
    scan_consts = _scan_constants(SCAN_CHUNK, hs)

    n_ctx_tok = n_ctx * t_ctx
    groups = [
        dict(x=x_prompt.reshape(n_ctx_tok, d), t=t_ctx, cols=t_ctx, rows=(0, n_ctx_tok), s0=None),
        dict(x=x_sample.reshape(n_lat * t_lat, d), t=t_lat, cols=GRID_W, rows=(1, t_lat), s0=state_rwkv),
    ]
    new_states = []
    for l in range(n_layer):
        mod_l = mod[l]
        g1 = norm1_g[l].reshape(1, d)
        win = w_in[l]
        win_r = win[:, rw_lo:rw_hi].astype(BF16)
        win_c = jnp.concatenate([win[:, :rw_lo], win[:, rw_hi:]], axis=1).astype(BF16)
        zero_w = jnp.zeros((lora_w, c_rw), F32)
        w2cat = jnp.concatenate([jnp.concatenate([rwkv_w2[l, 0], zero_w], axis=1),
                                 jnp.concatenate([zero_w, rwkv_w2[l, 1]], axis=1)], axis=0).astype(BF16)
        a2cat = jnp.concatenate([jnp.concatenate([rwkv_a2[l, 0], zero_w], axis=1),
                                 jnp.concatenate([zero_w, rwkv_a2[l, 1]], axis=1)], axis=0).astype(BF16)
        bs_full = jnp.repeat(sgu_bs[l].T, sgu_c // n_grp, axis=1)
        k_a = rwkv_k_a[l].reshape(1, c_rw)
        weights = dict(
            g2=rwkv_g2[l].astype(BF16), ws=sgu_ws[l].astype(BF16), pc=proj_conv[l].astype(BF16),
            ps=proj_sgu[l].astype(BF16), pr=proj_rwkv[l].astype(BF16), wo=w_out[l].astype(BF16),
            w1=exp_w1[l].astype(BF16), w3=exp_w3[l].astype(BF16), w2=exp_w2[l].astype(BF16))
        for grp in groups:
            x, rows = grp["x"], grp["rows"]
            r, k, v, kk, lw, aa, bonus, g = _prep(
                rows, x, mod_l, g1, win_r,
                rwkv_w0[l].reshape(1, 2 * c_rw), w2cat, rwkv_a0[l].reshape(1, 2 * c_rw), a2cat,
                weights["g2"], rwkv_k_k[l].reshape(1, c_rw), k_a, rwkv_r_k[l].reshape(1, c_rw), seg)
            part, grw = _branch(
                rows, grp["cols"], x, mod_l, g1, win_c, conv_w[l], sgu_ln_g[l].reshape(1, sgu_c),
                sgu_ln_b[l].reshape(1, sgu_c), weights["ws"], bs_full, weights["pc"], weights["ps"])
            n_seq = x.shape[0] // grp["t"]
            if grp["s0"] is None:
                s0 = jnp.zeros((n_seq, 2, n_str, hs, SCAN_HEADS * hs), F32)
            else:
                s0 = side_by_side(grp["s0"][:, l].astype(F32))
            yf, yb, s_fin = _scan(grp["t"], r, k, v, kk, lw, aa, k_a, s0, scan_consts)
            if grp["s0"] is None:
                new_states.append(from_side_by_side(s_fin))
            x1, h2, logits_t = _merge(
                rows, x, yf, yb, bonus, g, part, grw, mod_l, rwkv_lnx_w[l].reshape(1, c_rw),
                rwkv_lnx_b[l].reshape(1, c_rw), seg, weights["pr"], weights["wo"],
                norm2_g[l].reshape(1, d), rw_hi_b, rw_lo_b, hs)
            g4, pos, blk = _route(logits_t, bias_b, tri)
            g4 = jnp.pad(g4[:GROUP_SIZE].T, ((0, 0), (0, LANE_TILE - GROUP_SIZE)))
            g_hi = g4.astype(BF16)
            g_lo = (g4 - g_hi.astype(F32)).astype(BF16)
            grp["x"] = _moe(rows, blk[:, 0, :BLK_STRIDE].reshape(-1), h2, pos, pos.reshape(-1, 1), g_hi, g_lo,
                            x1, mod_l, weights["w1"], weights["w3"], weights["w2"], final_g2,
                            final_norm=(l == n_layer - 1))

    y_prompt = groups[0]["x"].reshape(n_ctx, t_ctx, d)
    y_sample = groups[1]["x"].reshape(n_lat, t_lat, d)
    new_state = jnp.stack(new_states, axis=1).astype(x_prompt.dtype)
    return (y_prompt, y_sample, new_state)
```

```python
import functools
import math

import jax
import jax.numpy as jnp
import numpy as np
from jax import lax
from jax.experimental import pallas as pl
from jax.experimental.pallas import tpu as pltpu

F32 = jnp.float32
BF16 = jnp.bfloat16

LANE_TILE = 128
SUBLANE_TILE = 8

GRID_W = 64
N_GROUPS = 4
GROUP_SIZE = 4
NORM_EPS = 1e-6
GN_EPS = 64e-5
KK_EPS = 1e-12
DECAY_SCALE = math.exp(-0.5)

TOKEN_TILE = 1024
MOE_TILE = 512
MOE_BLOCK = 128
BLK_STRIDE = 16
SCAN_CHUNK = 64
SCAN_HEADS = 4
SCAN_PAR = 2
SCAN_SUB = 4
SCAN_STAGGER = 4
MOD_COLS = 1536
VMEM_LIMIT = 56 * 1024 * 1024


def _dot(a, b):
    return jnp.dot(a, b, preferred_element_type=F32)


def _dot_nt(a, b):
    return lax.dot_general(a, b, (((1,), (1,)), ((), ())), preferred_element_type=F32)


def _split(x):
    hi = x.astype(BF16)
    lo = (x - hi.astype(F32)).astype(BF16)
    return hi, lo


def _seg_sum(x, seg):
    return _dot(x.astype(BF16), seg)


def _sigmoid(x):
    return 1.0 / (1.0 + jnp.exp(-x))


def _norm_mod(x, gain, shift, scale):
    ms = jnp.mean(x * x, axis=-1, keepdims=True)
    return (x * lax.rsqrt(ms + NORM_EPS) * gain) * (1.0 + scale) + shift


def _const_spec(shape):
    nd = len(shape)
    return pl.BlockSpec(shape, lambda *_: (0,) * nd)


def _mod_kernel(c_ref, w_ref, b_ref, o_ref):
    c = c_ref[...]
    s = c * _sigmoid(c)
    o_ref[0] = _dot(s.astype(BF16), w_ref[0].astype(BF16)) + b_ref[0]


def _modulation(cvec, ada_w, ada_b):
    n_layer, d, cols = ada_w.shape
    rows = cvec.shape[0]
    return pl.pallas_call(
        _mod_kernel,
        grid=(n_layer, cols // MOD_COLS),
        in_specs=[
            pl.BlockSpec((rows, d), lambda l, j: (0, 0)),
            pl.BlockSpec((1, d, MOD_COLS), lambda l, j: (l, 0, j)),
            pl.BlockSpec((1, 1, MOD_COLS), lambda l, j: (l, 0, j)),
        ],
        out_specs=pl.BlockSpec((1, rows, MOD_COLS), lambda l, j: (l, 0, j)),
        out_shape=jax.ShapeDtypeStruct((n_layer, rows, cols), F32),
        compiler_params=pltpu.CompilerParams(
            dimension_semantics=("arbitrary", "arbitrary"), vmem_limit_bytes=VMEM_LIMIT),
        name="mod",
    )(cvec, ada_w, ada_b.reshape(n_layer, 1, cols))


def _prep_kernel(x_ref, mod_ref, g1_ref, win_ref, w0_ref, w2_ref, a0_ref, a2_ref, g2_ref,
                 kk_ref, ka_ref, rk_ref, seg_ref,
                 r_o, k_o, v_o, kk_o, lw_o, aa_o, bonus_o, g_o, *, row0, tiles_per_seq):
    row = row0 + pl.program_id(0) // tiles_per_seq
    x = x_ref[...]
    d = x.shape[1]
    c = r_o.shape[1]
    shift = mod_ref[pl.ds(row, 1), 0:d]
    scale = mod_ref[pl.ds(row, 1), d:2 * d]
    h = _norm_mod(x, g1_ref[...], shift, scale)
    z = _dot(h.astype(BF16), win_ref[...])
    r = z[:, 0:c]
    k = z[:, c:2 * c]
    v = z[:, 2 * c:3 * c]
    lo = 3 * c
    w_lo = z[:, lo:lo + LANE_TILE]
    a_lo = z[:, lo + LANE_TILE:lo + 2 * LANE_TILE]
    g_lo = z[:, lo + 2 * LANE_TILE:lo + 3 * LANE_TILE]
    w_logit = w0_ref[...] + _dot(jnp.tanh(w_lo).astype(BF16), w2_ref[...])
    lw = -DECAY_SCALE * _sigmoid(w_logit)
    aa = _sigmoid(a0_ref[...] + _dot(a_lo.astype(BF16), a2_ref[...]))
    g = _dot(_sigmoid(g_lo).astype(BF16), g2_ref[...])
    seg = seg_ref[...]
    kk0 = k * kk_ref[...]
    kk = kk0 / jnp.maximum(jnp.sqrt(_seg_sum(kk0 * kk0, seg)), KK_EPS)
    ka = ka_ref[...]
    k_dirs = k * (1.0 + (aa[:, 0:c] - 1.0) * ka) + k * (1.0 + (aa[:, c:2 * c] - 1.0) * ka)
    rk = _seg_sum(r * k_dirs * rk_ref[...], seg)
    r_o[...] = r
    k_o[...] = k
    v_o[...] = v.astype(BF16)
    kk_o[...] = kk
    lw_o[0] = lw[:, 0:c]
    lw_o[1] = lw[:, c:2 * c]
    aa_o[0] = aa[:, 0:c]
    aa_o[1] = aa[:, c:2 * c]
    bonus_o[...] = (rk * v).astype(BF16)
    g_o[...] = g.astype(BF16)


def _prep(rows, x, mod, g1, win_r, w0, w2, a0, a2, g2, k_k, k_a, r_k, seg):
    n, d = x.shape
    c = seg.shape[0]
    tm = TOKEN_TILE
    tok = lambda w: pl.BlockSpec((tm, w), lambda i: (i, 0))
    tok2 = pl.BlockSpec((2, tm, c), lambda i: (0, i, 0))
    consts = [mod, g1, win_r, w0, w2, a0, a2, g2, k_k, k_a, r_k, seg]
    sds = jax.ShapeDtypeStruct
    return pl.pallas_call(
        functools.partial(_prep_kernel, row0=rows[0], tiles_per_seq=rows[1] // tm),
        grid=(n // tm,),
        in_specs=[tok(d)] + [_const_spec(a.shape) for a in consts],
        out_specs=[tok(c), tok(c), tok(c), tok(c), tok2, tok2, tok(c), tok(c)],
        out_shape=[sds((n, c), F32), sds((n, c), F32), sds((n, c), BF16), sds((n, c), F32)]
        + [sds((2, n, c), F32)] * 2 + [sds((n, c), BF16)] * 2,
        compiler_params=pltpu.CompilerParams(
            dimension_semantics=("arbitrary",), vmem_limit_bytes=VMEM_LIMIT),
        name="prep",
    )(x, *consts)


def _branch_kernel(x_ref, mod_ref, g1_ref, win_ref, cw_ref, lng_ref, lnb_ref,
                   ws_ref, bs_ref, pc_ref, ps_ref, part_o, grw_o, *, row0, tiles_per_seq, cols):
    row = row0 + pl.program_id(0) // tiles_per_seq
    x = x_ref[...]
    tm, d = x.shape
    cw_w = cw_ref.shape[1]
    sg_w = lng_ref.shape[1]
    chunk = ws_ref.shape[1]
    n_grp = ws_ref.shape[0]
    shift = mod_ref[pl.ds(row, 1), 0:d]
    scale = mod_ref[pl.ds(row, 1), d:2 * d]
    h = _norm_mod(x, g1_ref[...], shift, scale)
    z = _dot(h.astype(BF16), win_ref[...])
    cb = z[:, 0:cw_w]
    cc = z[:, cw_w:2 * cw_w]
    cx = z[:, 2 * cw_w:3 * cw_w]
    o = 3 * cw_w
    su = z[:, o:o + sg_w]
    sv = z[:, o + sg_w:o + 2 * sg_w]
    gates = _sigmoid(z[:, o + 2 * sg_w:])
    xc = cc * cx
    col = lax.broadcasted_iota(jnp.int32, xc.shape, 0) & (cols - 1)
    prev = jnp.where(col == 0, 0.0, pltpu.roll(xc, 1, 0))
    nxt = jnp.where(col == cols - 1, 0.0, pltpu.roll(xc, tm - 1, 0))
    cw = cw_ref[...]
    y_conv = cb * (cw[0:1] * prev + cw[1:2] * xc + cw[2:3] * nxt)
    mu = jnp.mean(sv, axis=-1, keepdims=True)
    dv = sv - mu
    var = jnp.mean(dv * dv, axis=-1, keepdims=True)
    vn = (dv * lax.rsqrt(var + NORM_EPS) * lng_ref[...] + lnb_ref[...]).astype(BF16)
    lane_grp = lax.broadcasted_iota(jnp.int32, (chunk, sg_w), 1) // (sg_w // n_grp)
    parts = []
    for ci in range(tm // chunk):
        vc = vn[ci * chunk:(ci + 1) * chunk]
        mixed = bs_ref[...]
        for gi in range(n_grp):
            mixed = mixed + jnp.where(lane_grp == gi, _dot(ws_ref[gi], vc), 0.0)
        parts.append(su[ci * chunk:(ci + 1) * chunk] * mixed)
    y_sgu = jnp.concatenate(parts, axis=0)
    part_o[...] = (gates[:, 0:d] * _dot(y_conv.astype(BF16), pc_ref[...])
                   + gates[:, 2 * d:3 * d] * _dot(y_sgu.astype(BF16), ps_ref[...])).astype(BF16)
    grw_o[...] = gates[:, d:2 * d].astype(BF16)


def _branch(rows, cols, x, mod, g1, win_c, conv_w, ln_g, ln_b, ws, bs_full, proj_conv, proj_sgu):
    n, d = x.shape
    tm = TOKEN_TILE
    tok = pl.BlockSpec((tm, d), lambda i: (i, 0))
    consts = [mod, g1, win_c, conv_w, ln_g, ln_b, ws, bs_full, proj_conv, proj_sgu]
    return pl.pallas_call(
        functools.partial(_branch_kernel, row0=rows[0], tiles_per_seq=rows[1] // tm, cols=cols),
        grid=(n // tm,),
        in_specs=[tok] + [_const_spec(a.shape) for a in consts],
        out_specs=[tok, tok],
        out_shape=[jax.ShapeDtypeStruct((n, d), BF16)] * 2,
        compiler_params=pltpu.CompilerParams(
            dimension_semantics=("arbitrary",), vmem_limit_bytes=VMEM_LIMIT),
        name="branch",
    )(x, *consts)


def _scan_constants(n_l, hs):
    lanes = SCAN_HEADS * hs
    t = np.arange(n_l)[:, None]
    s = np.arange(lanes)[None, :] % n_l
    ti, si = np.arange(n_l)[:, None], np.arange(n_l)[None, :]
    n_lvl = n_l.bit_length() - 1
    tri, masks, lvl = [], [], []
    for reverse in (False, True):
        if not reverse:
            level = lambda q: (((t >> q) & 1) == 1) & ((s >> q) == (t >> q) - 1)
            tri.append(ti >= si)
            masks.append(np.concatenate([t > s, t >= s, level(0)], axis=0))
        else:
            level = lambda q: (((t >> q) & 1) == 0) & ((s >> q) == (t >> q) + 1)
            tri.append(ti <= si)
            masks.append(np.concatenate([t < s, t <= s, level(0)], axis=0))
        lvl.append([level(q) for q in range(1, n_lvl)])
    as_j = lambda a, dt: jnp.asarray(np.asarray(a, np.float32), dt)
    return as_j(t == s, F32), as_j(masks, F32), as_j(lvl, BF16), as_j(tri, BF16)


class _HeadBlocks:
    def __init__(self, hs, lanes):
        self.hs, self.lanes = hs, lanes
        self.n_tiles = lanes // LANE_TILE
        self.per_tile = LANE_TILE // hs
        self.heads = lanes // hs
        lane_head = lax.broadcasted_iota(jnp.int32, (hs, LANE_TILE), 1) // hs
        self.half_f32 = [(lane_head == j).astype(F32) for j in range(self.per_tile)]
        self.half_bf16 = [m.astype(BF16) for m in self.half_f32]
        row_head = lax.broadcasted_iota(jnp.int32, (LANE_TILE, LANE_TILE), 0) // hs
        col_head = lax.broadcasted_iota(jnp.int32, (LANE_TILE, LANE_TILE), 1) // hs
        self.tile_mask = (row_head == col_head).astype(F32)

    def expand(self, compact):
        zero = jnp.zeros((self.hs, LANE_TILE), compact.dtype)
        rows = []
        for h in range(self.heads):
            tiles = [zero] * self.n_tiles
            tiles[h // self.per_tile] = compact[h * self.hs:(h + 1) * self.hs]
            rows.append(jnp.concatenate(tiles, axis=1))
        return jnp.concatenate(rows, axis=0)

    def compact(self, x, masks):
        return jnp.concatenate(
            [x[:, (h // self.per_tile) * LANE_TILE:(h // self.per_tile + 1) * LANE_TILE] * masks[h % self.per_tile]
             for h in range(self.heads)], axis=0)

    def diag(self, full, masks):
        return jnp.concatenate(
            [full[h * self.hs:(h + 1) * self.hs,
                  (h // self.per_tile) * LANE_TILE:(h // self.per_tile + 1) * LANE_TILE] * masks[h % self.per_tile]
             for h in range(self.heads)], axis=0)

    def side_by_side(self, compact):
        tiles = []
        for t in range(self.n_tiles):
            acc = compact[t * self.per_tile * self.hs:(t * self.per_tile + 1) * self.hs]
            for j in range(1, self.per_tile):
                h = t * self.per_tile + j
                acc = acc + compact[h * self.hs:(h + 1) * self.hs]
            tiles.append(acc)
        return jnp.concatenate(tiles, axis=1)

    def bd(self, x):
        return self.expand(self.compact(x.astype(BF16), self.half_bf16))

    def bd_t(self, x):
        tiles = []
        for t in range(self.n_tiles):
            square = jnp.concatenate([x[:, t * LANE_TILE:(t + 1) * LANE_TILE]] * self.per_tile, axis=0)
            tiles.append((square.T * self.tile_mask).astype(BF16))
        return self.expand(jnp.concatenate(tiles, axis=0))


def _scan_prepare(streams, hb, eye, masks_ref, lvl_ref, tri_ref, out):
    n_l = streams[0][0].shape[0]
    ns = range(len(streams))
    bd = hb.bd
    r, k, v, kk, lw, aa, ka, di = map(list, zip(*streams))
    tri = lambda i: tri_ref[di[i]]
    tril_mask = lambda i: masks_ref[di[i], 0:2 * n_l]
    end = [0 if d else n_l - 1 for d in di]
    l_hi = [lw[i].astype(BF16) for i in ns]
    l_mid = [(lw[i] - l_hi[i].astype(F32)).astype(BF16) for i in ns]
    l_lo = [(lw[i] - l_hi[i].astype(F32) - l_mid[i].astype(F32)).astype(BF16) for i in ns]
    c = [_dot(tri(i), l_hi[i]) + _dot(tri(i), l_mid[i]) + _dot(tri(i), l_lo[i]) for i in ns]
    yield
    b_vec = [kk[i] * aa[i] for i in ns]
    kd = [k[i] * (1.0 + (aa[i] - 1.0) * ka[i]) for i in ns]
    c_end = [c[i][end[i]:end[i] + 1] for i in ns]
    e_neg = [jnp.exp(-c[i]) for i in ns]
    lhs = [jnp.concatenate([-kk[i] * jnp.exp(c[i] - lw[i]), r[i] * jnp.exp(c[i])], axis=0).astype(BF16)
           for i in ns]
    yield
    m_b = [_dot(lhs[i], hb.bd_t(b_vec[i] * e_neg[i])) * tril_mask(i) for i in ns]
    yield
    m_k = [(_dot(lhs[i], hb.bd_t(kd[i] * e_neg[i])) * tril_mask(i)).astype(BF16) for i in ns]
    yield
    a_ab = [m_b[i][0:n_l] for i in ns]
    t_inv = [eye + a_ab[i] * masks_ref[di[i], 2 * n_l:3 * n_l] for i in ns]
    a_bf = [a_ab[i].astype(BF16) for i in ns]
    for q in range(lvl_ref.shape[1]):
        x1 = [_dot(t_inv[i].astype(BF16), bd(a_bf[i] * lvl_ref[di[i], q])) for i in ns]
        yield
        x2 = [_dot(x1[i].astype(BF16), bd(t_inv[i])) for i in ns]
        t_inv = [t_inv[i] + x2[i] for i in ns]
        yield
    v_k = [_dot(m_k[i], bd(v[i])) for i in ns]
    e_end = [jnp.exp(c_end[i] - c[i]) for i in ns]
    bk_t = [jnp.concatenate([b_vec[i] * e_end[i], kd[i] * e_end[i]], axis=0).T.astype(BF16) for i in ns]
    decay_col = [jnp.broadcast_to(jnp.exp(c_end[i]), (LANE_TILE, hb.lanes)).T for i in ns]
    out.extend((lhs[i], v_k[i], t_inv[i], m_b[i][n_l:].astype(BF16), v[i], bk_t[i], decay_col[i]) for i in ns)


def _scan_apply(pre, z, hb, out):
    lhs, v_k, t_inv, a_rb, v, bk_t, decay_col = map(list, zip(*pre))
    n_l = t_inv[0].shape[0]
    ns = range(len(pre))
    v_part = [_dot(lhs[i], hb.expand(z[i].astype(BF16))) + v_k[i] for i in ns]
    yield
    u = [_dot(t_inv[i].astype(BF16), hb.bd(v_part[i][0:n_l])) for i in ns]
    yield
    y = [v_part[i][n_l:] + _dot(a_rb[i], hb.bd(u[i])) for i in ns]
    yield
    uv = [jnp.concatenate([u[i].astype(BF16), v[i].astype(BF16)], axis=0) for i in ns]
    z_new = [z[i] * decay_col[i] + hb.diag(_dot(bk_t[i], uv[i]), hb.half_f32) for i in ns]
    out.extend(zip(y, z_new))


def _scan_kernel(rf_ref, kf_ref, vf_ref, kkf_ref, lwf_ref, aaf_ref,
                 rb_ref, kb_ref, vb_ref, kkb_ref, lwb_ref, aab_ref,
                 ka_ref, s0_ref, eye_ref, masks_ref, lvl_ref, tri_ref,
                 yf_o, yb_o, sfin_o, z_scr):
    j = pl.program_id(1)
    n_par, n_rows, c = rf_ref.shape
    n_l = eye_ref.shape[0]
    lanes = z_scr.shape[1]
    n_str = c // lanes
    hb = _HeadBlocks(lanes // SCAN_HEADS, lanes)
    slots = [(p, di, hg) for p in range(n_par) for di in range(2) for hg in range(n_str)]

    @pl.when(j == 0)
    def _():
        for zi, (p, di, hg) in enumerate(slots):
            z_scr[zi] = hb.compact(s0_ref[p, 0, di, hg], hb.half_f32)

    ka = ka_ref[...]
    dirs = ((rf_ref, kf_ref, vf_ref, kkf_ref, lwf_ref, aaf_ref, yf_o),
            (rb_ref, kb_ref, vb_ref, kkb_ref, lwb_ref, aab_ref, yb_o))
    n_sub = n_rows // n_l
    streams = []
    for sub in range(n_sub):
        for p, di, hg in slots:
            r_ref, k_ref, v_ref, kk_ref, lw_ref, aa_ref, _ = dirs[di]
            rs = slice((n_sub - 1 - sub if di else sub) * n_l, (n_sub - sub if di else sub + 1) * n_l)
            ls = slice(hg * lanes, (hg + 1) * lanes)
            streams.append((r_ref[p, rs, ls], k_ref[p, rs, ls], v_ref[p, rs, ls], kk_ref[p, rs, ls],
                            lw_ref[0, p, rs, ls], aa_ref[0, p, rs, ls], ka[:, ls], di))
    n_slot = len(slots)
    eye = eye_ref[...]
    pre = [[] for _ in range(n_sub)]
    res = [[] for _ in range(n_sub)]
    z0 = [z_scr[zi] for zi in range(n_slot)]

    gens = [_scan_prepare(streams[s * n_slot:(s + 1) * n_slot], hb, eye, masks_ref, lvl_ref, tri_ref, pre[s])
            for s in range(n_sub)]
    for _ in range(SCAN_STAGGER):
        next(gens[0])
    for s in range(n_sub):
        nxt = gens[s + 1] if s + 1 < n_sub else None
        for _ in gens[s]:
            if nxt is not None:
                next(nxt, None)
        z_in = z0 if s == 0 else [o[1] for o in res[s - 1]]
        for _ in _scan_apply(pre[s], z_in, hb, res[s]):
            if nxt is not None:
                next(nxt, None)
        for (y, _), (p, di, hg) in zip(res[s], slots):
            rs = slice((n_sub - 1 - s if di else s) * n_l, (n_sub - s if di else s + 1) * n_l)
            dirs[di][6][p, rs, hg * lanes:(hg + 1) * lanes] = y
    z = [o[1] for o in res[n_sub - 1]]
    for zi in range(len(slots)):
        z_scr[zi] = z[zi]

    @pl.when(j == pl.num_programs(1) - 1)
    def _():
        for zi, (p, di, hg) in enumerate(slots):
            sfin_o[p, 0, di, hg] = hb.side_by_side(z_scr[zi])


def _scan(t_len, r, k, v, kk, lw, aa, k_a, s0, consts):
    n, c = r.shape
    n_seq = s0.shape[0]
    n_par = SCAN_PAR
    n_l = SCAN_CHUNK * SCAN_SUB
    n_c = t_len // n_l
    hs = s0.shape[3]
    lanes = SCAN_HEADS * hs
    n_str = c // lanes
    part = lambda a: a.reshape(n_par, n // n_par, c)
    part2 = lambda a: a.reshape(2, n_par, n // n_par, c)
    s0p = s0.reshape((n_par, n_seq // n_par) + s0.shape[1:])
    fwd = pl.BlockSpec((n_par, n_l, c), lambda q, j: (0, q * n_c + j, 0))
    bwd = pl.BlockSpec((n_par, n_l, c), lambda q, j: (0, q * n_c + n_c - 1 - j, 0))
    fwd2 = pl.BlockSpec((1, n_par, n_l, c), lambda q, j: (0, 0, q * n_c + j, 0))
    bwd2 = pl.BlockSpec((1, n_par, n_l, c), lambda q, j: (1, 0, q * n_c + n_c - 1 - j, 0))
    st = pl.BlockSpec((n_par, 1, 2, n_str, hs, lanes), lambda q, j: (0, q, 0, 0, 0, 0))
    sds = jax.ShapeDtypeStruct
    args = [part(r), part(k), part(v), part(kk), part2(lw), part2(aa)]
    yf, yb, s_fin = pl.pallas_call(
        _scan_kernel,
        grid=(n_seq // n_par, n_c),
        in_specs=[fwd, fwd, fwd, fwd, fwd2, fwd2, bwd, bwd, bwd, bwd, bwd2, bwd2,
                  _const_spec(k_a.shape), st] + [_const_spec(a.shape) for a in consts],
        out_specs=[fwd, bwd, st],
        out_shape=[sds((n_par, n // n_par, c), F32), sds((n_par, n // n_par, c), F32), sds(s0p.shape, F32)],
        scratch_shapes=[pltpu.VMEM((n_par * 2 * n_str, lanes, LANE_TILE), F32)],
        compiler_params=pltpu.CompilerParams(
            dimension_semantics=("arbitrary", "arbitrary"), vmem_limit_bytes=VMEM_LIMIT),
        name="scan",
    )(*args, *args, k_a, s0p, *consts)
    return yf.reshape(n, c), yb.reshape(n, c), s_fin.reshape(s0.shape)


def _merge_kernel(x_ref, yf_ref, yb_ref, bonus_ref, g_ref, part_ref, grw_ref, mod_ref,
                  lnw_ref, lnb_ref, seg_ref, prw_ref, wout_ref, g2_ref, rwh_ref, rwl_ref,
                  x1_o, h2_o, lg_o, *, head_size, row0, tiles_per_seq):
    row = row0 + pl.program_id(0) // tiles_per_seq
    x = x_ref[...]
    d = x.shape[1]
    seg = seg_ref[...]
    y = yf_ref[...] + yb_ref[...]
    hs_inv = 1.0 / head_size
    mu = _seg_sum(y, seg) * hs_inv
    dy = y - mu
    var = _seg_sum(dy * dy, seg) * hs_inv
    yn = dy * lax.rsqrt(var + GN_EPS) * lnw_ref[...] + lnb_ref[...]
    y_rwkv = (yn + bonus_ref[...]) * g_ref[...]
    merged = part_ref[...] + grw_ref[...] * _dot(y_rwkv.astype(BF16), prw_ref[...])
    mix = _dot(merged.astype(BF16), wout_ref[...])
    gate1 = mod_ref[pl.ds(row, 1), 2 * d:3 * d]
    shift2 = mod_ref[pl.ds(row, 1), 3 * d:4 * d]
    scale2 = mod_ref[pl.ds(row, 1), 4 * d:5 * d]
    x1 = x + gate1 * mix
    h2 = _norm_mod(x1, g2_ref[...], shift2, scale2)
    h_hi, h_lo = _split(h2)
    rwh = rwh_ref[...]
    n_e = rwh.shape[0]
    both = _dot_nt(jnp.concatenate([rwh, rwl_ref[...]], axis=0), h_hi)
    x1_o[...] = x1
    h2_o[...] = h_hi
    lg_o[...] = both[0:n_e] + both[n_e:] + _dot_nt(rwh, h_lo)


def _merge(rows, x, yf, yb, bonus, g, part, grw, mod, lnw, lnb, seg, proj_rwkv, w_out, g2n, rw_hi, rw_lo,
           head_size):
    n, d = x.shape
    c = seg.shape[0]
    n_e = rw_hi.shape[0]
    tm = TOKEN_TILE
    tok = lambda w: pl.BlockSpec((tm, w), lambda i: (i, 0))
    consts = [mod, lnw, lnb, seg, proj_rwkv, w_out, g2n, rw_hi, rw_lo]
    sds = jax.ShapeDtypeStruct
    return pl.pallas_call(
        functools.partial(_merge_kernel, head_size=head_size, row0=rows[0], tiles_per_seq=rows[1] // tm),
        grid=(n // tm,),
        in_specs=[tok(d), tok(c), tok(c), tok(c), tok(c), tok(d), tok(d)]
        + [_const_spec(a.shape) for a in consts],
        out_specs=[tok(d), tok(d), pl.BlockSpec((n_e, tm), lambda i: (0, i))],
        out_shape=[sds((n, d), F32), sds((n, d), BF16), sds((n_e, n), F32)],
        compiler_params=pltpu.CompilerParams(
            dimension_semantics=("arbitrary",), vmem_limit_bytes=VMEM_LIMIT),
        name="merge",
    )(x, yf, yb, bonus, g, part, grw, *consts)


def _route_kernel(lg_ref, b_ref, tri_ref, tok_o, pos_o, blk_o):
    scores = _sigmoid(lg_ref[...])
    biased = scores + b_ref[...]
    n_e, n_t = scores.shape
    rows = [biased[e:e + 1] for e in range(n_e)]
    srow = [scores[e:e + 1] for e in range(n_e)]

    def top2_sum(a, b, c, d):
        s1, t1 = jnp.maximum(a, b), jnp.minimum(a, b)
        s2, t2 = jnp.maximum(c, d), jnp.minimum(c, d)
        return jnp.maximum(s1, s2) + jnp.maximum(jnp.minimum(s1, s2), jnp.maximum(t1, t2))

    best = top2_sum(*rows[0:GROUP_SIZE])
    g_sel = jnp.zeros(best.shape, jnp.int32)
    for gi in range(1, N_GROUPS):
        gs = top2_sum(*rows[gi * GROUP_SIZE:(gi + 1) * GROUP_SIZE])
        better = gs > best
        best = jnp.where(better, gs, best)
        g_sel = jnp.where(better, gi, g_sel)

    def pick(src, j):
        out = src[j]
        for gi in range(1, N_GROUPS):
            out = jnp.where(g_sel == gi, src[gi * GROUP_SIZE + j], out)
        return out

    in_b = [pick(rows, j) for j in range(GROUP_SIZE)]
    in_s = [pick(srow, j) for j in range(GROUP_SIZE)]

    def arg_first_max(vals):
        best_v = vals[0]
        best_i = jnp.zeros(best_v.shape, jnp.int32)
        for j in range(1, len(vals)):
            better = vals[j] > best_v
            best_v = jnp.where(better, vals[j], best_v)
            best_i = jnp.where(better, j, best_i)
        return best_i

    i1 = arg_first_max(in_b)
    i2 = arg_first_max([jnp.where(i1 == j, -jnp.inf, in_b[j]) for j in range(GROUP_SIZE)])

    def take(src, idx):
        out = src[0]
        for j in range(1, GROUP_SIZE):
            out = jnp.where(idx == j, src[j], out)
        return out

    g1, g2 = take(in_s, i1), take(in_s, i2)
    den = g1 + g2
    j_idx = lax.broadcasted_iota(jnp.int32, (SUBLANE_TILE, n_t), 0)
    g4 = jnp.where(j_idx == i1, g1 / den, jnp.where(j_idx == i2, g2 / den, 0.0))

    grp = lax.broadcasted_iota(jnp.int32, (n_e, n_t), 0)
    onehot = (grp == g_sel).astype(F32)
    rank = _dot(onehot.astype(BF16), tri_ref[...])
    cnt = jnp.sum(onehot, axis=1, keepdims=True).astype(jnp.int32)
    blocks = jnp.zeros(cnt.shape, jnp.int32)
    for kb in range(-(-n_t // MOE_BLOCK)):
        blocks = blocks + (cnt > kb * MOE_BLOCK).astype(jnp.int32)
    used = jnp.zeros((1, 1), jnp.int32)
    pos = jnp.zeros((1, n_t), jnp.int32)
    lane = lax.broadcasted_iota(jnp.int32, (1, blk_o.shape[2]), 1)
    blk_grp = jnp.zeros(lane.shape, jnp.int32)
    for gi in range(N_GROUPS):
        pos = pos + jnp.where(g_sel == gi, used * MOE_BLOCK + rank[gi:gi + 1].astype(jnp.int32), 0)
        used = used + blocks[gi:gi + 1]
        if gi < N_GROUPS - 1:
            blk_grp = blk_grp + (used <= lane).astype(jnp.int32)
    pos_o[...] = pos
    per_token = jnp.where(j_idx == GROUP_SIZE, pos.astype(F32), g4)
    tok_o[...] = jnp.concatenate([per_token, jnp.zeros((LANE_TILE - SUBLANE_TILE, n_t), F32)], axis=0).T
    n_blk = _moe_blocks(n_t)
    blk_o[0] = jnp.where(lane < n_blk, blk_grp, jnp.where(lane == n_blk, used, 0))


def _moe_blocks(tile):
    return -(-(tile + N_GROUPS * (MOE_BLOCK - 1)) // MOE_BLOCK)


def _route(logits_t, bias_b, tri):
    n_e, n = logits_t.shape
    rt = MOE_TILE
    sds = jax.ShapeDtypeStruct
    return pl.pallas_call(
        _route_kernel,
        grid=(n // rt,),
        in_specs=[pl.BlockSpec((n_e, rt), lambda i: (0, i)), _const_spec(bias_b.shape), _const_spec(tri.shape)],
        out_specs=[pl.BlockSpec((rt, LANE_TILE), lambda i: (i, 0)), pl.BlockSpec((1, rt), lambda i: (0, i)),
                   pl.BlockSpec((1, 1, LANE_TILE), lambda i: (i, 0, 0))],
        out_shape=[sds((n, LANE_TILE), F32), sds((1, n), jnp.int32), sds((n // rt, 1, LANE_TILE), jnp.int32)],
        compiler_params=pltpu.CompilerParams(dimension_semantics=("arbitrary",)),
        name="route",
    )(logits_t, bias_b, tri)


def _moe_kernel(blk_ref, h_ref, posr_ref, tok_ref, x1_ref, mod_ref, w1_ref, w3_ref, w2_ref,
                fg_ref, o_ref, xs_ref, ys_ref, *, final_norm, row0, tiles_per_seq):
    i = pl.program_id(0)
    tm, d = x1_ref.shape
    r_rows = xs_ref.shape[0]
    n_blk = r_rows // MOE_BLOCK
    sel = jnp.where(lax.broadcasted_iota(jnp.int32, (r_rows, tm), 0) == posr_ref[...], 1.0, 0.0).astype(BF16)
    tok = tok_ref[...]
    g_hi = tok.astype(BF16)
    g_lo = (tok - g_hi.astype(F32)).astype(BF16)
    gathered = _dot(sel, jnp.concatenate([h_ref[...], g_hi, g_lo], axis=1))
    xs_ref[...] = gathered[:, 0:d].astype(BF16)
    gs = gathered[:, d:d + LANE_TILE] + gathered[:, d + LANE_TILE:]
    base = i * BLK_STRIDE
    n_used = blk_ref[base + n_blk]
    for b in range(n_blk):
        rows = slice(b * MOE_BLOCK, (b + 1) * MOE_BLOCK)

        @pl.when(b < n_used)
        def _():
            first = blk_ref[base + b] * GROUP_SIZE
            xb = xs_ref[rows]
            experts = range(GROUP_SIZE)
            a = [_dot(xb, w1_ref[first + j]) for j in experts]
            g = [_dot(xb, w3_ref[first + j]) for j in experts]
            hid = [((a[j] * _sigmoid(a[j])) * g[j]).astype(BF16) for j in experts]
            out = [gs[rows, j:j + 1] * _dot(hid[j], w2_ref[first + j]) for j in experts]
            ys_ref[rows] = ((out[0] + out[1]) + (out[2] + out[3])).astype(BF16)

        @pl.when(b >= n_used)
        def _():
            ys_ref[rows] = jnp.zeros((MOE_BLOCK, d), BF16)

    pos_col = tok[:, GROUP_SIZE:GROUP_SIZE + 1].astype(jnp.int32)
    selt = jnp.where(lax.broadcasted_iota(jnp.int32, (tm, r_rows), 1) == pos_col, 1.0, 0.0).astype(BF16)
    row = row0 + i // tiles_per_seq
    gate2 = mod_ref[pl.ds(row, 1), 5 * d:6 * d]
    x2 = x1_ref[...] + gate2 * _dot(selt, ys_ref[...])
    if final_norm:
        ms = jnp.mean(x2 * x2, axis=-1, keepdims=True)
        x2 = x2 * lax.rsqrt(ms + NORM_EPS) * fg_ref[...]
    o_ref[...] = x2


def _moe(rows, blk, h2, pos_row, tok, x1, mod, w1, w3, w2, final_g, final_norm):
    n, d = x1.shape
    tm = MOE_TILE
    r_rows = _moe_blocks(tm) * MOE_BLOCK
    rows_of = lambda w: pl.BlockSpec((tm, w), lambda i, *_: (i, 0))
    resident = lambda a: pl.BlockSpec(a.shape, lambda i, *_: (0,) * a.ndim, pipeline_mode=pl.Buffered(1))
    grid_spec = pltpu.PrefetchScalarGridSpec(
        num_scalar_prefetch=1,
        grid=(n // tm,),
        in_specs=[rows_of(d), pl.BlockSpec((1, tm), lambda i, *_: (0, i)), rows_of(LANE_TILE), rows_of(d),
                  _const_spec(mod.shape), resident(w1), resident(w3), resident(w2), _const_spec(final_g.shape)],
        out_specs=rows_of(d),
        scratch_shapes=[pltpu.VMEM((r_rows, d), BF16), pltpu.VMEM((r_rows, d), BF16)],
    )
    return pl.pallas_call(
        functools.partial(_moe_kernel, final_norm=final_norm, row0=rows[0],
                          tiles_per_seq=max(rows[1] // tm, 1)),
        grid_spec=grid_spec,
        out_shape=jax.ShapeDtypeStruct((n, d), F32),
        compiler_params=pltpu.CompilerParams(
            dimension_semantics=("arbitrary",), vmem_limit_bytes=VMEM_LIMIT),
        name="moe",
    )(blk, h2, pos_row, tok, x1, mod, w1, w3, w2, final_g)


def kernel(x_prompt, x_sample, state_rwkv, c, c_ctx, norm1_g, norm2_g, final_g, ada_w, ada_b, w_in,
           conv_w, rwkv_w0, rwkv_w2, rwkv_a0, rwkv_a2, rwkv_g2, rwkv_k_k, rwkv_k_a, rwkv_r_k,
           rwkv_lnx_w, rwkv_lnx_b, sgu_ln_g, sgu_ln_b, sgu_ws, sgu_bs, proj_conv, proj_rwkv, proj_sgu,
           w_out, router_w, router_b, exp_w1, exp_w3, exp_w2):
    n_ctx, t_ctx, d = x_prompt.shape
    n_lat, t_lat, _ = x_sample.shape
    n_layer = w_in.shape[0]
    _, n_head, hs = rwkv_r_k.shape
    c_rw = n_head * hs
    conv_c = conv_w.shape[2]
    sgu_c = sgu_ln_g.shape[1]
    n_grp, chunk, _ = sgu_ws.shape[1:]
    lora_w = rwkv_w2.shape[2]
    lora_a = rwkv_a2.shape[2]
    lora_g = rwkv_g2.shape[1]
    n_exp = router_w.shape[1]
    assert 2 * lora_w == LANE_TILE and 2 * lora_a == LANE_TILE and lora_g == LANE_TILE
    assert t_ctx & (t_ctx - 1) == 0 and GRID_W & (GRID_W - 1) == 0
    assert TOKEN_TILE % t_ctx == 0 and t_lat % TOKEN_TILE == 0 and TOKEN_TILE % GRID_W == 0
    assert t_lat % MOE_TILE == 0 and (n_ctx * t_ctx) % MOE_TILE == 0 and TOKEN_TILE % chunk == 0
    assert n_exp == N_GROUPS * GROUP_SIZE and c_rw % (SCAN_HEADS * hs) == 0
    assert SCAN_CHUNK == hs and t_ctx % (SCAN_CHUNK * SCAN_SUB) == 0 and t_lat % (SCAN_CHUNK * SCAN_SUB) == 0
    assert n_ctx % SCAN_PAR == 0 and n_lat % SCAN_PAR == 0
    assert _moe_blocks(MOE_TILE) < BLK_STRIDE and MOE_BLOCK % 16 == 0

    mod_rows = SUBLANE_TILE * (-(-(1 + n_lat) // SUBLANE_TILE))
    cvec = jnp.zeros((mod_rows, d), F32).at[0].set(c_ctx).at[1:1 + n_lat].set(c)
    mod = _modulation(cvec, ada_w, ada_b)

    sizes = (conv_c, conv_c, conv_c, c_rw, c_rw, c_rw, 2 * lora_w, 2 * lora_a, lora_g, sgu_c, sgu_c, 3 * d)
    offs = np.concatenate([[0], np.cumsum(sizes)])
    assert offs[-1] == w_in.shape[2]
    rw_lo, rw_hi = int(offs[3]), int(offs[9])
    seg = jnp.asarray(np.kron(np.eye(n_head), np.ones((hs, hs))), BF16)
    n_str = c_rw // (SCAN_HEADS * hs)

    def side_by_side(s):
        s = jnp.swapaxes(s, -1, -2).reshape(s.shape[0], 2, n_str, SCAN_HEADS, hs, hs)
        return jnp.swapaxes(s, 3, 4).reshape(s.shape[0], 2, n_str, hs, SCAN_HEADS * hs)

    def from_side_by_side(s):
        s = s.reshape(s.shape[0], 2, n_str, hs, SCAN_HEADS, hs)
        return jnp.swapaxes(jnp.swapaxes(s, 3, 4).reshape(s.shape[0], 2, n_head, hs, hs), -1, -2)

    rw_t = router_w.T
    rw_hi_b = rw_t.astype(BF16)
    rw_lo_b = (rw_t - rw_hi_b.astype(F32)).astype(BF16)
    bias_b = jnp.broadcast_to(router_b.astype(F32)[:, None], (n_exp, MOE_TILE))
    tri = jnp.asarray(np.triu(np.ones((MOE_TILE, MOE_TILE)), 1), BF16)
    final_g2 = final_g.reshape(1, d)
    scan_consts = _scan_constants(SCAN_CHUNK, hs)

    n_ctx_tok = n_ctx * t_ctx
    groups = [
        dict(x=x_prompt.reshape(n_ctx_tok, d), t=t_ctx, cols=t_ctx, rows=(0, n_ctx_tok), s0=None),
        dict(x=x_sample.reshape(n_lat * t_lat, d), t=t_lat, cols=GRID_W, rows=(1, t_lat), s0=state_rwkv),
    ]
    new_states = []
    for l in range(n_layer):
        mod_l = mod[l]
        g1 = norm1_g[l].reshape(1, d)
        win = w_in[l]
        win_r = win[:, rw_lo:rw_hi].astype(BF16)
        win_c = jnp.concatenate([win[:, :rw_lo], win[:, rw_hi:]], axis=1).astype(BF16)
        zero_w = jnp.zeros((lora_w, c_rw), F32)
        w2cat = jnp.concatenate([jnp.concatenate([rwkv_w2[l, 0], zero_w], axis=1),
                                 jnp.concatenate([zero_w, rwkv_w2[l, 1]], axis=1)], axis=0).astype(BF16)
        a2cat = jnp.concatenate([jnp.concatenate([rwkv_a2[l, 0], zero_w], axis=1),
                                 jnp.concatenate([zero_w, rwkv_a2[l, 1]], axis=1)], axis=0).astype(BF16)
        bs_full = jnp.repeat(sgu_bs[l].T, sgu_c // n_grp, axis=1)
        k_a = rwkv_k_a[l].reshape(1, c_rw)
        weights = dict(
            g2=rwkv_g2[l].astype(BF16), ws=sgu_ws[l].astype(BF16), pc=proj_conv[l].astype(BF16),
            ps=proj_sgu[l].astype(BF16), pr=proj_rwkv[l].astype(BF16), wo=w_out[l].astype(BF16),
            w1=exp_w1[l].astype(BF16), w3=exp_w3[l].astype(BF16), w2=exp_w2[l].astype(BF16))
        for grp in groups:
            x, rows = grp["x"], grp["rows"]
            r, k, v, kk, lw, aa, bonus, g = _prep(
                rows, x, mod_l, g1, win_r,
                rwkv_w0[l].reshape(1, 2 * c_rw), w2cat, rwkv_a0[l].reshape(1, 2 * c_rw), a2cat,
                weights["g2"], rwkv_k_k[l].reshape(1, c_rw), k_a, rwkv_r_k[l].reshape(1, c_rw), seg)
            part, grw = _branch(
                rows, grp["cols"], x, mod_l, g1, win_c, conv_w[l], sgu_ln_g[l].reshape(1, sgu_c),
                sgu_ln_b[l].reshape(1, sgu_c), weights["ws"], bs_full, weights["pc"], weights["ps"])
            n_seq = x.shape[0] // grp["t"]
            if grp["s0"] is None:
                s0 = jnp.zeros((n_seq, 2, n_str, hs, SCAN_HEADS * hs), F32)
            else:
                s0 = side_by_side(grp["s0"][:, l].astype(F32))
            yf, yb, s_fin = _scan(grp["t"], r, k, v, kk, lw, aa, k_a, s0, scan_consts)
            if grp["s0"] is None:
                new_states.append(from_side_by_side(s_fin))
            x1, h2, logits_t = _merge(
                rows, x, yf, yb, bonus, g, part, grw, mod_l, rwkv_lnx_w[l].reshape(1, c_rw),
                rwkv_lnx_b[l].reshape(1, c_rw), seg, weights["pr"], weights["wo"],
                norm2_g[l].reshape(1, d), rw_hi_b, rw_lo_b, hs)
            tok, pos, blk = _route(logits_t, bias_b, tri)
            grp["x"] = _moe(rows, blk[:, 0, :BLK_STRIDE].reshape(-1), h2, pos, tok, x1, mod_l, weights["w1"],
                            weights["w3"], weights["w2"], final_g2, final_norm=(l == n_layer - 1))

    y_prompt = groups[0]["x"].reshape(n_ctx, t_ctx, d)
    y_sample = groups[1]["x"].reshape(n_lat, t_lat, d)
    new_state = jnp.stack(new_states, axis=1).astype(x_prompt.dtype)
    return (y_prompt, y_sample, new_state)
```

```python
import functools
import math

import jax
import jax.numpy as jnp
import numpy as np
from jax import lax
from jax.experimental import pallas as pl
from jax.experimental.pallas import tpu as pltpu

F32 = jnp.float32
BF16 = jnp.bfloat16

LANE_TILE = 128
SUBLANE_TILE = 8

GRID_W = 64
N_GROUPS = 4
GROUP_SIZE = 4
NORM_EPS = 1e-6
GN_EPS = 64e-5
KK_EPS = 1e-12
DECAY_SCALE = math.exp(-0.5)

TOKEN_TILE = 1024
MOE_TILE = 512
MOE_BLOCK = 128
BLK_STRIDE = 16
SCAN_CHUNK = 64
SCAN_HEADS = 4
SCAN_PAR = 2
SCAN_SUB = 4
SCAN_STAGGER = 4
MOD_COLS = 1536
VMEM_LIMIT = 56 * 1024 * 1024


def _dot(a, b):
    return jnp.dot(a, b, preferred_element_type=F32)


def _dot_nt(a, b):
    return lax.dot_general(a, b, (((1,), (1,)), ((), ())), preferred_element_type=F32)


def _split(x):
    hi = x.astype(BF16)
    lo = (x - hi.astype(F32)).astype(BF16)
    return hi, lo


def _seg_sum(x, seg):
    return _dot(x.astype(BF16), seg)


def _sigmoid(x):
    return 1.0 / (1.0 + jnp.exp(-x))


def _norm_mod(x, gain, shift, scale):
    ms = jnp.mean(x * x, axis=-1, keepdims=True)
    return (x * lax.rsqrt(ms + NORM_EPS) * gain) * (1.0 + scale) + shift


def _const_spec(shape):
    nd = len(shape)
    return pl.BlockSpec(shape, lambda *_: (0,) * nd)


def _mod_kernel(c_ref, w_ref, b_ref, o_ref):
    c = c_ref[...]
    s = c * _sigmoid(c)
    o_ref[0] = _dot(s.astype(BF16), w_ref[0].astype(BF16)) + b_ref[0]


def _modulation(cvec, ada_w, ada_b):
    n_layer, d, cols = ada_w.shape
    rows = cvec.shape[0]
    return pl.pallas_call(
        _mod_kernel,
        grid=(n_layer, cols // MOD_COLS),
        in_specs=[
            pl.BlockSpec((rows, d), lambda l, j: (0, 0)),
            pl.BlockSpec((1, d, MOD_COLS), lambda l, j: (l, 0, j)),
            pl.BlockSpec((1, 1, MOD_COLS), lambda l, j: (l, 0, j)),
        ],
        out_specs=pl.BlockSpec((1, rows, MOD_COLS), lambda l, j: (l, 0, j)),
        out_shape=jax.ShapeDtypeStruct((n_layer, rows, cols), F32),
        compiler_params=pltpu.CompilerParams(
            dimension_semantics=("arbitrary", "arbitrary"), vmem_limit_bytes=VMEM_LIMIT),
        name="mod",
    )(cvec, ada_w, ada_b.reshape(n_layer, 1, cols))


def _prep_kernel(x_ref, mod_ref, g1_ref, win_ref, w0_ref, w2_ref, a0_ref, a2_ref, g2_ref,
                 kk_ref, ka_ref, rk_ref, seg_ref,
                 r_o, k_o, v_o, kk_o, lw_o, aa_o, bonus_o, g_o, *, row0, tiles_per_seq):
    row = row0 + pl.program_id(0) // tiles_per_seq
    x = x_ref[...]
    d = x.shape[1]
    c = r_o.shape[1]
    shift = mod_ref[pl.ds(row, 1), 0:d]
    scale = mod_ref[pl.ds(row, 1), d:2 * d]
    h = _norm_mod(x, g1_ref[...], shift, scale)
    z = _dot(h.astype(BF16), win_ref[...])
    r = z[:, 0:c]
    k = z[:, c:2 * c]
    v = z[:, 2 * c:3 * c]
    lo = 3 * c
    w_lo = z[:, lo:lo + LANE_TILE]
    a_lo = z[:, lo + LANE_TILE:lo + 2 * LANE_TILE]
    g_lo = z[:, lo + 2 * LANE_TILE:lo + 3 * LANE_TILE]
    w_logit = w0_ref[...] + _dot(jnp.tanh(w_lo).astype(BF16), w2_ref[...])
    lw = -DECAY_SCALE * _sigmoid(w_logit)
    aa = _sigmoid(a0_ref[...] + _dot(a_lo.astype(BF16), a2_ref[...]))
    g = _dot(_sigmoid(g_lo).astype(BF16), g2_ref[...])
    seg = seg_ref[...]
    kk0 = k * kk_ref[...]
    kk = kk0 / jnp.maximum(jnp.sqrt(_seg_sum(kk0 * kk0, seg)), KK_EPS)
    ka = ka_ref[...]
    k_dirs = k * (1.0 + (aa[:, 0:c] - 1.0) * ka) + k * (1.0 + (aa[:, c:2 * c] - 1.0) * ka)
    rk = _seg_sum(r * k_dirs * rk_ref[...], seg)
    r_o[...] = r
    k_o[...] = k
    v_o[...] = v.astype(BF16)
    kk_o[...] = kk
    lw_o[0] = lw[:, 0:c]
    lw_o[1] = lw[:, c:2 * c]
    aa_o[0] = aa[:, 0:c]
    aa_o[1] = aa[:, c:2 * c]
    bonus_o[...] = (rk * v).astype(BF16)
    g_o[...] = g.astype(BF16)


def _prep(rows, x, mod, g1, win_r, w0, w2, a0, a2, g2, k_k, k_a, r_k, seg):
    n, d = x.shape
    c = seg.shape[0]
    tm = TOKEN_TILE
    tok = lambda w: pl.BlockSpec((tm, w), lambda i: (i, 0))
    tok2 = pl.BlockSpec((2, tm, c), lambda i: (0, i, 0))
    consts = [mod, g1, win_r, w0, w2, a0, a2, g2, k_k, k_a, r_k, seg]
    sds = jax.ShapeDtypeStruct
    return pl.pallas_call(
        functools.partial(_prep_kernel, row0=rows[0], tiles_per_seq=rows[1] // tm),
        grid=(n // tm,),
        in_specs=[tok(d)] + [_const_spec(a.shape) for a in consts],
        out_specs=[tok(c), tok(c), tok(c), tok(c), tok2, tok2, tok(c), tok(c)],
        out_shape=[sds((n, c), F32), sds((n, c), F32), sds((n, c), BF16), sds((n, c), F32)]
        + [sds((2, n, c), F32)] * 2 + [sds((n, c), BF16)] * 2,
        compiler_params=pltpu.CompilerParams(
            dimension_semantics=("arbitrary",), vmem_limit_bytes=VMEM_LIMIT),
        name="prep",
    )(x, *consts)


def _branch_kernel(x_ref, mod_ref, g1_ref, win_ref, cw_ref, lng_ref, lnb_ref,
                   ws_ref, bs_ref, pc_ref, ps_ref, part_o, grw_o, *, row0, tiles_per_seq, cols):
    row = row0 + pl.program_id(0) // tiles_per_seq
    x = x_ref[...]
    tm, d = x.shape
    cw_w = cw_ref.shape[1]
    sg_w = lng_ref.shape[1]
    chunk = ws_ref.shape[1]
    n_grp = ws_ref.shape[0]
    shift = mod_ref[pl.ds(row, 1), 0:d]
    scale = mod_ref[pl.ds(row, 1), d:2 * d]
    h = _norm_mod(x, g1_ref[...], shift, scale)
    z = _dot(h.astype(BF16), win_ref[...])
    cb = z[:, 0:cw_w]
    cc = z[:, cw_w:2 * cw_w]
    cx = z[:, 2 * cw_w:3 * cw_w]
    o = 3 * cw_w
    su = z[:, o:o + sg_w]
    sv = z[:, o + sg_w:o + 2 * sg_w]
    gates = _sigmoid(z[:, o + 2 * sg_w:])
    xc = cc * cx
    col = lax.broadcasted_iota(jnp.int32, xc.shape, 0) & (cols - 1)
    prev = jnp.where(col == 0, 0.0, pltpu.roll(xc, 1, 0))
    nxt = jnp.where(col == cols - 1, 0.0, pltpu.roll(xc, tm - 1, 0))
    cw = cw_ref[...]
    y_conv = cb * (cw[0:1] * prev + cw[1:2] * xc + cw[2:3] * nxt)
    mu = jnp.mean(sv, axis=-1, keepdims=True)
    dv = sv - mu
    var = jnp.mean(dv * dv, axis=-1, keepdims=True)
    vn = (dv * lax.rsqrt(var + NORM_EPS) * lng_ref[...] + lnb_ref[...]).astype(BF16)
    lane_grp = lax.broadcasted_iota(jnp.int32, (chunk, sg_w), 1) // (sg_w // n_grp)
    parts = []
    for ci in range(tm // chunk):
        vc = vn[ci * chunk:(ci + 1) * chunk]
        mixed = bs_ref[...]
        for gi in range(n_grp):
            mixed = mixed + jnp.where(lane_grp == gi, _dot(ws_ref[gi], vc), 0.0)
        parts.append(su[ci * chunk:(ci + 1) * chunk] * mixed)
    y_sgu = jnp.concatenate(parts, axis=0)
    part_o[...] = (gates[:, 0:d] * _dot(y_conv.astype(BF16), pc_ref[...])
                   + gates[:, 2 * d:3 * d] * _dot(y_sgu.astype(BF16), ps_ref[...])).astype(BF16)
    grw_o[...] = gates[:, d:2 * d].astype(BF16)


def _branch(rows, cols, x, mod, g1, win_c, conv_w, ln_g, ln_b, ws, bs_full, proj_conv, proj_sgu):
    n, d = x.shape
    tm = TOKEN_TILE
    tok = pl.BlockSpec((tm, d), lambda i: (i, 0))
    consts = [mod, g1, win_c, conv_w, ln_g, ln_b, ws, bs_full, proj_conv, proj_sgu]
    return pl.pallas_call(
        functools.partial(_branch_kernel, row0=rows[0], tiles_per_seq=rows[1] // tm, cols=cols),
        grid=(n // tm,),
        in_specs=[tok] + [_const_spec(a.shape) for a in consts],
        out_specs=[tok, tok],
        out_shape=[jax.ShapeDtypeStruct((n, d), BF16)] * 2,
        compiler_params=pltpu.CompilerParams(
            dimension_semantics=("arbitrary",), vmem_limit_bytes=VMEM_LIMIT),
        name="branch",
    )(x, *consts)


def _scan_constants(n_l, hs):
    lanes = SCAN_HEADS * hs
    t = np.arange(n_l)[:, None]
    s = np.arange(lanes)[None, :] % n_l
    n_lvl = n_l.bit_length() - 1
    masks, lvl = [], []
    for reverse in (False, True):
        if not reverse:
            level = lambda q: (((t >> q) & 1) == 1) & ((s >> q) == (t >> q) - 1)
            masks.append(np.concatenate([t > s, t >= s, level(0)], axis=0))
        else:
            level = lambda q: (((t >> q) & 1) == 0) & ((s >> q) == (t >> q) + 1)
            masks.append(np.concatenate([t < s, t <= s, level(0)], axis=0))
        lvl.append([level(q) for q in range(1, n_lvl)])
    as_j = lambda a, dt: jnp.asarray(np.asarray(a, np.float32), dt)
    return as_j(t == s, F32), as_j(masks, F32), as_j(lvl, BF16)


class _HeadBlocks:
    def __init__(self, hs, lanes):
        self.hs, self.lanes = hs, lanes
        self.n_tiles = lanes // LANE_TILE
        self.per_tile = LANE_TILE // hs
        self.heads = lanes // hs
        lane_head = lax.broadcasted_iota(jnp.int32, (hs, LANE_TILE), 1) // hs
        self.half_f32 = [(lane_head == j).astype(F32) for j in range(self.per_tile)]
        self.half_bf16 = [m.astype(BF16) for m in self.half_f32]
        row_head = lax.broadcasted_iota(jnp.int32, (LANE_TILE, LANE_TILE), 0) // hs
        col_head = lax.broadcasted_iota(jnp.int32, (LANE_TILE, LANE_TILE), 1) // hs
        self.tile_mask = (row_head == col_head).astype(F32)

    def expand(self, compact):
        zero = jnp.zeros((self.hs, LANE_TILE), compact.dtype)
        rows = []
        for h in range(self.heads):
            tiles = [zero] * self.n_tiles
            tiles[h // self.per_tile] = compact[h * self.hs:(h + 1) * self.hs]
            rows.append(jnp.concatenate(tiles, axis=1))
        return jnp.concatenate(rows, axis=0)

    def compact(self, x, masks):
        return jnp.concatenate(
            [x[:, (h // self.per_tile) * LANE_TILE:(h // self.per_tile + 1) * LANE_TILE] * masks[h % self.per_tile]
             for h in range(self.heads)], axis=0)

    def diag(self, full, masks):
        return jnp.concatenate(
            [full[h * self.hs:(h + 1) * self.hs,
                  (h // self.per_tile) * LANE_TILE:(h // self.per_tile + 1) * LANE_TILE] * masks[h % self.per_tile]
             for h in range(self.heads)], axis=0)

    def side_by_side(self, compact):
        tiles = []
        for t in range(self.n_tiles):
            acc = compact[t * self.per_tile * self.hs:(t * self.per_tile + 1) * self.hs]
            for j in range(1, self.per_tile):
                h = t * self.per_tile + j
                acc = acc + compact[h * self.hs:(h + 1) * self.hs]
            tiles.append(acc)
        return jnp.concatenate(tiles, axis=1)

    def bd(self, x):
        return self.expand(self.compact(x.astype(BF16), self.half_bf16))

    def bd_t(self, x):
        tiles = []
        for t in range(self.n_tiles):
            square = jnp.concatenate([x[:, t * LANE_TILE:(t + 1) * LANE_TILE]] * self.per_tile, axis=0)
            tiles.append((square.T * self.tile_mask).astype(BF16))
        return self.expand(jnp.concatenate(tiles, axis=0))


def _scan_prepare(streams, hb, eye, masks_ref, lvl_ref, out):
    n_l = streams[0][0].shape[0]
    ns = range(len(streams))
    bd = hb.bd
    r, k, v, kk, lw, aa, ka, di = map(list, zip(*streams))
    tril_mask = lambda i: masks_ref[di[i], 0:2 * n_l]
    end = [0 if d else n_l - 1 for d in di]
    step_row = lax.broadcasted_iota(jnp.int32, lw[0].shape, 0)
    c = list(lw)
    shift = 1
    while shift < n_l:
        for i in ns:
            if di[i]:
                moved = jnp.where(step_row < n_l - shift, pltpu.roll(c[i], n_l - shift, 0), 0.0)
            else:
                moved = jnp.where(step_row >= shift, pltpu.roll(c[i], shift, 0), 0.0)
            c[i] = c[i] + moved
        shift *= 2
    yield
    b_vec = [kk[i] * aa[i] for i in ns]
    kd = [k[i] * (1.0 + (aa[i] - 1.0) * ka[i]) for i in ns]
    c_end = [c[i][end[i]:end[i] + 1] for i in ns]
    e_neg = [jnp.exp(-c[i]) for i in ns]
    lhs = [jnp.concatenate([-kk[i] * jnp.exp(c[i] - lw[i]), r[i] * jnp.exp(c[i])], axis=0).astype(BF16)
           for i in ns]
    yield
    m_b = [_dot(lhs[i], hb.bd_t(b_vec[i] * e_neg[i])) * tril_mask(i) for i in ns]
    yield
    m_k = [(_dot(lhs[i], hb.bd_t(kd[i] * e_neg[i])) * tril_mask(i)).astype(BF16) for i in ns]
    yield
    a_ab = [m_b[i][0:n_l] for i in ns]
    t_inv = [eye + a_ab[i] * masks_ref[di[i], 2 * n_l:3 * n_l] for i in ns]
    a_bf = [a_ab[i].astype(BF16) for i in ns]
    for q in range(lvl_ref.shape[1]):
        x1 = [_dot(t_inv[i].astype(BF16), bd(a_bf[i] * lvl_ref[di[i], q])) for i in ns]
        yield
        x2 = [_dot(x1[i].astype(BF16), bd(t_inv[i])) for i in ns]
        t_inv = [t_inv[i] + x2[i] for i in ns]
        yield
    v_k = [_dot(m_k[i], bd(v[i])) for i in ns]
    e_end = [jnp.exp(c_end[i] - c[i]) for i in ns]
    bk_t = [jnp.concatenate([b_vec[i] * e_end[i], kd[i] * e_end[i]], axis=0).T.astype(BF16) for i in ns]
    decay_col = [jnp.broadcast_to(jnp.exp(c_end[i]), (LANE_TILE, hb.lanes)).T for i in ns]
    out.extend((lhs[i], v_k[i], t_inv[i], m_b[i][n_l:].astype(BF16), v[i], bk_t[i], decay_col[i]) for i in ns)


def _scan_apply(pre, z, hb, out):
    lhs, v_k, t_inv, a_rb, v, bk_t, decay_col = map(list, zip(*pre))
    n_l = t_inv[0].shape[0]
    ns = range(len(pre))
    v_part = [_dot(lhs[i], hb.expand(z[i].astype(BF16))) + v_k[i] for i in ns]
    yield
    u = [_dot(t_inv[i].astype(BF16), hb.bd(v_part[i][0:n_l])) for i in ns]
    yield
    y = [v_part[i][n_l:] + _dot(a_rb[i], hb.bd(u[i])) for i in ns]
    yield
    uv = [jnp.concatenate([u[i].astype(BF16), v[i].astype(BF16)], axis=0) for i in ns]
    z_new = [z[i] * decay_col[i] + hb.diag(_dot(bk_t[i], uv[i]), hb.half_f32) for i in ns]
    out.extend(zip(y, z_new))


def _scan_kernel(rf_ref, kf_ref, vf_ref, kkf_ref, lwf_ref, aaf_ref,
                 rb_ref, kb_ref, vb_ref, kkb_ref, lwb_ref, aab_ref,
                 ka_ref, s0_ref, eye_ref, masks_ref, lvl_ref,
                 yf_o, yb_o, sfin_o, z_scr):
    j = pl.program_id(1)
    n_par, n_rows, c = rf_ref.shape
    n_l = eye_ref.shape[0]
    lanes = z_scr.shape[1]
    n_str = c // lanes
    hb = _HeadBlocks(lanes // SCAN_HEADS, lanes)
    slots = [(p, di, hg) for p in range(n_par) for di in range(2) for hg in range(n_str)]

    @pl.when(j == 0)
    def _():
        for zi, (p, di, hg) in enumerate(slots):
            z_scr[zi] = hb.compact(s0_ref[p, 0, di, hg], hb.half_f32)

    ka = ka_ref[...]
    dirs = ((rf_ref, kf_ref, vf_ref, kkf_ref, lwf_ref, aaf_ref, yf_o),
            (rb_ref, kb_ref, vb_ref, kkb_ref, lwb_ref, aab_ref, yb_o))
    n_sub = n_rows // n_l
    streams = []
    for sub in range(n_sub):
        for p, di, hg in slots:
            r_ref, k_ref, v_ref, kk_ref, lw_ref, aa_ref, _ = dirs[di]
            rs = slice((n_sub - 1 - sub if di else sub) * n_l, (n_sub - sub if di else sub + 1) * n_l)
            ls = slice(hg * lanes, (hg + 1) * lanes)
            streams.append((r_ref[p, rs, ls], k_ref[p, rs, ls], v_ref[p, rs, ls], kk_ref[p, rs, ls],
                            lw_ref[0, p, rs, ls], aa_ref[0, p, rs, ls], ka[:, ls], di))
    n_slot = len(slots)
    eye = eye_ref[...]
    pre = [[] for _ in range(n_sub)]
    res = [[] for _ in range(n_sub)]
    z0 = [z_scr[zi] for zi in range(n_slot)]

    gens = [_scan_prepare(streams[s * n_slot:(s + 1) * n_slot], hb, eye, masks_ref, lvl_ref, pre[s])
            for s in range(n_sub)]
    for _ in range(SCAN_STAGGER):
        next(gens[0])
    for s in range(n_sub):
        nxt = gens[s + 1] if s + 1 < n_sub else None
        for _ in gens[s]:
            if nxt is not None:
                next(nxt, None)
        z_in = z0 if s == 0 else [o[1] for o in res[s - 1]]
        for _ in _scan_apply(pre[s], z_in, hb, res[s]):
            if nxt is not None:
                next(nxt, None)
        for (y, _), (p, di, hg) in zip(res[s], slots):
            rs = slice((n_sub - 1 - s if di else s) * n_l, (n_sub - s if di else s + 1) * n_l)
            dirs[di][6][p, rs, hg * lanes:(hg + 1) * lanes] = y
    z = [o[1] for o in res[n_sub - 1]]
    for zi in range(len(slots)):
        z_scr[zi] = z[zi]

    @pl.when(j == pl.num_programs(1) - 1)
    def _():
        for zi, (p, di, hg) in enumerate(slots):
            sfin_o[p, 0, di, hg] = hb.side_by_side(z_scr[zi])


def _scan(t_len, r, k, v, kk, lw, aa, k_a, s0, consts):
    n, c = r.shape
    n_seq = s0.shape[0]
    n_par = SCAN_PAR
    n_l = SCAN_CHUNK * SCAN_SUB
    n_c = t_len // n_l
    hs = s0.shape[3]
    lanes = SCAN_HEADS * hs
    n_str = c // lanes
    part = lambda a: a.reshape(n_par, n // n_par, c)
    part2 = lambda a: a.reshape(2, n_par, n // n_par, c)
    s0p = s0.reshape((n_par, n_seq // n_par) + s0.shape[1:])
    fwd = pl.BlockSpec((n_par, n_l, c), lambda q, j: (0, q * n_c + j, 0))
    bwd = pl.BlockSpec((n_par, n_l, c), lambda q, j: (0, q * n_c + n_c - 1 - j, 0))
    fwd2 = pl.BlockSpec((1, n_par, n_l, c), lambda q, j: (0, 0, q * n_c + j, 0))
    bwd2 = pl.BlockSpec((1, n_par, n_l, c), lambda q, j: (1, 0, q * n_c + n_c - 1 - j, 0))
    st = pl.BlockSpec((n_par, 1, 2, n_str, hs, lanes), lambda q, j: (0, q, 0, 0, 0, 0))
    sds = jax.ShapeDtypeStruct
    args = [part(r), part(k), part(v), part(kk), part2(lw), part2(aa)]
    yf, yb, s_fin = pl.pallas_call(
        _scan_kernel,
        grid=(n_seq // n_par, n_c),
        in_specs=[fwd, fwd, fwd, fwd, fwd2, fwd2, bwd, bwd, bwd, bwd, bwd2, bwd2,
                  _const_spec(k_a.shape), st] + [_const_spec(a.shape) for a in consts],
        out_specs=[fwd, bwd, st],
        out_shape=[sds((n_par, n // n_par, c), F32), sds((n_par, n // n_par, c), F32), sds(s0p.shape, F32)],
        scratch_shapes=[pltpu.VMEM((n_par * 2 * n_str, lanes, LANE_TILE), F32)],
        compiler_params=pltpu.CompilerParams(
            dimension_semantics=("arbitrary", "arbitrary"), vmem_limit_bytes=VMEM_LIMIT),
        name="scan",
    )(*args, *args, k_a, s0p, *consts)
    return yf.reshape(n, c), yb.reshape(n, c), s_fin.reshape(s0.shape)


def _merge_kernel(x_ref, yf_ref, yb_ref, bonus_ref, g_ref, part_ref, grw_ref, mod_ref,
                  lnw_ref, lnb_ref, seg_ref, prw_ref, wout_ref, g2_ref, rwh_ref, rwl_ref,
                  x1_o, h2_o, lg_o, *, head_size, row0, tiles_per_seq):
    row = row0 + pl.program_id(0) // tiles_per_seq
    x = x_ref[...]
    d = x.shape[1]
    seg = seg_ref[...]
    y = yf_ref[...] + yb_ref[...]
    hs_inv = 1.0 / head_size
    mu = _seg_sum(y, seg) * hs_inv
    dy = y - mu
    var = _seg_sum(dy * dy, seg) * hs_inv
    yn = dy * lax.rsqrt(var + GN_EPS) * lnw_ref[...] + lnb_ref[...]
    y_rwkv = (yn + bonus_ref[...]) * g_ref[...]
    merged = part_ref[...] + grw_ref[...] * _dot(y_rwkv.astype(BF16), prw_ref[...])
    mix = _dot(merged.astype(BF16), wout_ref[...])
    gate1 = mod_ref[pl.ds(row, 1), 2 * d:3 * d]
    shift2 = mod_ref[pl.ds(row, 1), 3 * d:4 * d]
    scale2 = mod_ref[pl.ds(row, 1), 4 * d:5 * d]
    x1 = x + gate1 * mix
    h2 = _norm_mod(x1, g2_ref[...], shift2, scale2)
    h_hi, h_lo = _split(h2)
    rwh = rwh_ref[...]
    n_e = rwh.shape[0]
    both = _dot_nt(jnp.concatenate([rwh, rwl_ref[...]], axis=0), h_hi)
    x1_o[...] = x1
    h2_o[...] = h_hi
    lg_o[...] = both[0:n_e] + both[n_e:] + _dot_nt(rwh, h_lo)


def _merge(rows, x, yf, yb, bonus, g, part, grw, mod, lnw, lnb, seg, proj_rwkv, w_out, g2n, rw_hi, rw_lo,
           head_size):
    n, d = x.shape
    c = seg.shape[0]
    n_e = rw_hi.shape[0]
    tm = TOKEN_TILE
    tok = lambda w: pl.BlockSpec((tm, w), lambda i: (i, 0))
    consts = [mod, lnw, lnb, seg, proj_rwkv, w_out, g2n, rw_hi, rw_lo]
    sds = jax.ShapeDtypeStruct
    return pl.pallas_call(
        functools.partial(_merge_kernel, head_size=head_size, row0=rows[0], tiles_per_seq=rows[1] // tm),
        grid=(n // tm,),
        in_specs=[tok(d), tok(c), tok(c), tok(c), tok(c), tok(d), tok(d)]
        + [_const_spec(a.shape) for a in consts],
        out_specs=[tok(d), tok(d), pl.BlockSpec((n_e, tm), lambda i: (0, i))],
        out_shape=[sds((n, d), F32), sds((n, d), BF16), sds((n_e, n), F32)],
        compiler_params=pltpu.CompilerParams(
            dimension_semantics=("arbitrary",), vmem_limit_bytes=VMEM_LIMIT),
        name="merge",
    )(x, yf, yb, bonus, g, part, grw, *consts)


def _route_kernel(lg_ref, b_ref, tri_ref, tok_o, pos_o, blk_o):
    scores = _sigmoid(lg_ref[...])
    biased = scores + b_ref[...]
    n_e, n_t = scores.shape
    rows = [biased[e:e + 1] for e in range(n_e)]
    srow = [scores[e:e + 1] for e in range(n_e)]

    def top2_sum(a, b, c, d):
        s1, t1 = jnp.maximum(a, b), jnp.minimum(a, b)
        s2, t2 = jnp.maximum(c, d), jnp.minimum(c, d)
        return jnp.maximum(s1, s2) + jnp.maximum(jnp.minimum(s1, s2), jnp.maximum(t1, t2))

    best = top2_sum(*rows[0:GROUP_SIZE])
    g_sel = jnp.zeros(best.shape, jnp.int32)
    for gi in range(1, N_GROUPS):
        gs = top2_sum(*rows[gi * GROUP_SIZE:(gi + 1) * GROUP_SIZE])
        better = gs > best
        best = jnp.where(better, gs, best)
        g_sel = jnp.where(better, gi, g_sel)

    def pick(src, j):
        out = src[j]
        for gi in range(1, N_GROUPS):
            out = jnp.where(g_sel == gi, src[gi * GROUP_SIZE + j], out)
        return out

    in_b = [pick(rows, j) for j in range(GROUP_SIZE)]
    in_s = [pick(srow, j) for j in range(GROUP_SIZE)]

    def arg_first_max(vals):
        best_v = vals[0]
        best_i = jnp.zeros(best_v.shape, jnp.int32)
        for j in range(1, len(vals)):
            better = vals[j] > best_v
            best_v = jnp.where(better, vals[j], best_v)
            best_i = jnp.where(better, j, best_i)
        return best_i

    i1 = arg_first_max(in_b)
    i2 = arg_first_max([jnp.where(i1 == j, -jnp.inf, in_b[j]) for j in range(GROUP_SIZE)])

    def take(src, idx):
        out = src[0]
        for j in range(1, GROUP_SIZE):
            out = jnp.where(idx == j, src[j], out)
        return out

    g1, g2 = take(in_s, i1), take(in_s, i2)
    den = g1 + g2
    j_idx = lax.broadcasted_iota(jnp.int32, (SUBLANE_TILE, n_t), 0)
    g4 = jnp.where(j_idx == i1, g1 / den, jnp.where(j_idx == i2, g2 / den, 0.0))

    grp = lax.broadcasted_iota(jnp.int32, (n_e, n_t), 0)
    onehot = (grp == g_sel).astype(F32)
    rank = _dot(onehot.astype(BF16), tri_ref[...])
    cnt = jnp.sum(onehot, axis=1, keepdims=True).astype(jnp.int32)
    blocks = jnp.zeros(cnt.shape, jnp.int32)
    for kb in range(-(-n_t // MOE_BLOCK)):
        blocks = blocks + (cnt > kb * MOE_BLOCK).astype(jnp.int32)
    used = jnp.zeros((1, 1), jnp.int32)
    pos = jnp.zeros((1, n_t), jnp.int32)
    lane = lax.broadcasted_iota(jnp.int32, (1, blk_o.shape[2]), 1)
    blk_grp = jnp.zeros(lane.shape, jnp.int32)
    for gi in range(N_GROUPS):
        pos = pos + jnp.where(g_sel == gi, used * MOE_BLOCK + rank[gi:gi + 1].astype(jnp.int32), 0)
        used = used + blocks[gi:gi + 1]
        if gi < N_GROUPS - 1:
            blk_grp = blk_grp + (used <= lane).astype(jnp.int32)
    pos_o[...] = pos
    per_token = jnp.where(j_idx == GROUP_SIZE, pos.astype(F32), g4)
    tok_o[...] = jnp.concatenate([per_token, jnp.zeros((LANE_TILE - SUBLANE_TILE, n_t), F32)], axis=0).T
    n_blk = _moe_blocks(n_t)
    blk_o[0] = jnp.where(lane < n_blk, blk_grp, jnp.where(lane == n_blk, used, 0))


def _moe_blocks(tile):
    return -(-(tile + N_GROUPS * (MOE_BLOCK - 1)) // MOE_BLOCK)


def _route(logits_t, bias_b, tri):
    n_e, n = logits_t.shape
    rt = MOE_TILE
    sds = jax.ShapeDtypeStruct
    return pl.pallas_call(
        _route_kernel,
        grid=(n // rt,),
        in_specs=[pl.BlockSpec((n_e, rt), lambda i: (0, i)), _const_spec(bias_b.shape), _const_spec(tri.shape)],
        out_specs=[pl.BlockSpec((rt, LANE_TILE), lambda i: (i, 0)), pl.BlockSpec((1, rt), lambda i: (0, i)),
                   pl.BlockSpec((1, 1, LANE_TILE), lambda i: (i, 0, 0))],
        out_shape=[sds((n, LANE_TILE), F32), sds((1, n), jnp.int32), sds((n // rt, 1, LANE_TILE), jnp.int32)],
        compiler_params=pltpu.CompilerParams(dimension_semantics=("arbitrary",)),
        name="route",
    )(logits_t, bias_b, tri)


def _moe_kernel(blk_ref, h_ref, posr_ref, tok_ref, x1_ref, mod_ref, w1_ref, w3_ref, w2_ref,
                fg_ref, o_ref, xs_ref, ys_ref, *, final_norm, row0, tiles_per_seq):
    i = pl.program_id(0)
    tm, d = x1_ref.shape
    r_rows = xs_ref.shape[0]
    n_blk = r_rows // MOE_BLOCK
    sel = jnp.where(lax.broadcasted_iota(jnp.int32, (r_rows, tm), 0) == posr_ref[...], 1.0, 0.0).astype(BF16)
    tok = tok_ref[...]
    g_hi = tok.astype(BF16)
    g_lo = (tok - g_hi.astype(F32)).astype(BF16)
    gathered = _dot(sel, jnp.concatenate([h_ref[...], g_hi, g_lo], axis=1))
    xs_ref[...] = gathered[:, 0:d].astype(BF16)
    gs = gathered[:, d:d + LANE_TILE] + gathered[:, d + LANE_TILE:]
    base = i * BLK_STRIDE
    n_used = blk_ref[base + n_blk]
    for b in range(n_blk):
        rows = slice(b * MOE_BLOCK, (b + 1) * MOE_BLOCK)

        @pl.when(b < n_used)
        def _():
            first = blk_ref[base + b] * GROUP_SIZE
            xb = xs_ref[rows]
            experts = range(GROUP_SIZE)
            a = [_dot(xb, w1_ref[first + j]) for j in experts]
            g = [_dot(xb, w3_ref[first + j]) for j in experts]
            hid = [((a[j] * _sigmoid(a[j])) * g[j]).astype(BF16) for j in experts]
            out = [gs[rows, j:j + 1] * _dot(hid[j], w2_ref[first + j]) for j in experts]
            ys_ref[rows] = ((out[0] + out[1]) + (out[2] + out[3])).astype(BF16)

        @pl.when(b >= n_used)
        def _():
            ys_ref[rows] = jnp.zeros((MOE_BLOCK, d), BF16)

    pos_col = tok[:, GROUP_SIZE:GROUP_SIZE + 1].astype(jnp.int32)
    selt = jnp.where(lax.broadcasted_iota(jnp.int32, (tm, r_rows), 1) == pos_col, 1.0, 0.0).astype(BF16)
    row = row0 + i // tiles_per_seq
    gate2 = mod_ref[pl.ds(row, 1), 5 * d:6 * d]
    x2 = x1_ref[...] + gate2 * _dot(selt, ys_ref[...])
    if final_norm:
        ms = jnp.mean(x2 * x2, axis=-1, keepdims=True)
        x2 = x2 * lax.rsqrt(ms + NORM_EPS) * fg_ref[...]
    o_ref[...] = x2


def _moe(rows, blk, h2, pos_row, tok, x1, mod, w1, w3, w2, final_g, final_norm):
    n, d = x1.shape
    tm = MOE_TILE
    r_rows = _moe_blocks(tm) * MOE_BLOCK
    rows_of = lambda w: pl.BlockSpec((tm, w), lambda i, *_: (i, 0))
    resident = lambda a: pl.BlockSpec(a.shape, lambda i, *_: (0,) * a.ndim, pipeline_mode=pl.Buffered(1))
    grid_spec = pltpu.PrefetchScalarGridSpec(
        num_scalar_prefetch=1,
        grid=(n // tm,),
        in_specs=[rows_of(d), pl.BlockSpec((1, tm), lambda i, *_: (0, i)), rows_of(LANE_TILE), rows_of(d),
                  _const_spec(mod.shape), resident(w1), resident(w3), resident(w2), _const_spec(final_g.shape)],
        out_specs=rows_of(d),
        scratch_shapes=[pltpu.VMEM((r_rows, d), BF16), pltpu.VMEM((r_rows, d), BF16)],
    )
    return pl.pallas_call(
        functools.partial(_moe_kernel, final_norm=final_norm, row0=rows[0],
                          tiles_per_seq=max(rows[1] // tm, 1)),
        grid_spec=grid_spec,
        out_shape=jax.ShapeDtypeStruct((n, d), F32),
        compiler_params=pltpu.CompilerParams(
            dimension_semantics=("arbitrary",), vmem_limit_bytes=VMEM_LIMIT),
        name="moe",
    )(blk, h2, pos_row, tok, x1, mod, w1, w3, w2, final_g)


def kernel(x_prompt, x_sample, state_rwkv, c, c_ctx, norm1_g, norm2_g, final_g, ada_w, ada_b, w_in,
           conv_w, rwkv_w0, rwkv_w2, rwkv_a0, rwkv_a2, rwkv_g2, rwkv_k_k, rwkv_k_a, rwkv_r_k,
           rwkv_lnx_w, rwkv_lnx_b, sgu_ln_g, sgu_ln_b, sgu_ws, sgu_bs, proj_conv, proj_rwkv, proj_sgu,
           w_out, router_w, router_b, exp_w1, exp_w3, exp_w2):
    n_ctx, t_ctx, d = x_prompt.shape
    n_lat, t_lat, _ = x_sample.shape
    n_layer = w_in.shape[0]
    _, n_head, hs = rwkv_r_k.shape
    c_rw = n_head * hs
    conv_c = conv_w.shape[2]
    sgu_c = sgu_ln_g.shape[1]
    n_grp, chunk, _ = sgu_ws.shape[1:]
    lora_w = rwkv_w2.shape[2]
    lora_a = rwkv_a2.shape[2]
    lora_g = rwkv_g2.shape[1]
    n_exp = router_w.shape[1]
    assert 2 * lora_w == LANE_TILE and 2 * lora_a == LANE_TILE and lora_g == LANE_TILE
    assert t_ctx & (t_ctx - 1) == 0 and GRID_W & (GRID_W - 1) == 0
    assert TOKEN_TILE % t_ctx == 0 and t_lat % TOKEN_TILE == 0 and TOKEN_TILE % GRID_W == 0
    assert t_lat % MOE_TILE == 0 and (n_ctx * t_ctx) % MOE_TILE == 0 and TOKEN_TILE % chunk == 0
    assert n_exp == N_GROUPS * GROUP_SIZE and c_rw % (SCAN_HEADS * hs) == 0
    assert SCAN_CHUNK == hs and t_ctx % (SCAN_CHUNK * SCAN_SUB) == 0 and t_lat % (SCAN_CHUNK * SCAN_SUB) == 0
    assert n_ctx % SCAN_PAR == 0 and n_lat % SCAN_PAR == 0
    assert _moe_blocks(MOE_TILE) < BLK_STRIDE and MOE_BLOCK % 16 == 0

    mod_rows = SUBLANE_TILE * (-(-(1 + n_lat) // SUBLANE_TILE))
    cvec = jnp.zeros((mod_rows, d), F32).at[0].set(c_ctx).at[1:1 + n_lat].set(c)
    mod = _modulation(cvec, ada_w, ada_b)

    sizes = (conv_c, conv_c, conv_c, c_rw, c_rw, c_rw, 2 * lora_w, 2 * lora_a, lora_g, sgu_c, sgu_c, 3 * d)
    offs = np.concatenate([[0], np.cumsum(sizes)])
    assert offs[-1] == w_in.shape[2]
    rw_lo, rw_hi = int(offs[3]), int(offs[9])
    seg = jnp.asarray(np.kron(np.eye(n_head), np.ones((hs, hs))), BF16)
    n_str = c_rw // (SCAN_HEADS * hs)

    def side_by_side(s):
        s = jnp.swapaxes(s, -1, -2).reshape(s.shape[0], 2, n_str, SCAN_HEADS, hs, hs)
        return jnp.swapaxes(s, 3, 4).reshape(s.shape[0], 2, n_str, hs, SCAN_HEADS * hs)

    def from_side_by_side(s):
        s = s.reshape(s.shape[0], 2, n_str, hs, SCAN_HEADS, hs)
        return jnp.swapaxes(jnp.swapaxes(s, 3, 4).reshape(s.shape[0], 2, n_head, hs, hs), -1, -2)

    rw_t = router_w.T
    rw_hi_b = rw_t.astype(BF16)
    rw_lo_b = (rw_t - rw_hi_b.astype(F32)).astype(BF16)
    bias_b = jnp.broadcast_to(router_b.astype(F32)[:, None], (n_exp, MOE_TILE))
    tri = jnp.asarray(np.triu(np.ones((MOE_TILE, MOE_TILE)), 1), BF16)
    final_g2 = final_g.reshape(1, d)
    scan_consts = _scan_constants(SCAN_CHUNK, hs)

    n_ctx_tok = n_ctx * t_ctx
    groups = [
        dict(x=x_prompt.reshape(n_ctx_tok, d), t=t_ctx, cols=t_ctx, rows=(0, n_ctx_tok), s0=None),
        dict(x=x_sample.reshape(n_lat * t_lat, d), t=t_lat, cols=GRID_W, rows=(1, t_lat), s0=state_rwkv),
    ]
    new_states = []
    for l in range(n_layer):
        mod_l = mod[l]
        g1 = norm1_g[l].reshape(1, d)
        win = w_in[l]
        win_r = win[:, rw_lo:rw_hi].astype(BF16)
        win_c = jnp.concatenate([win[:, :rw_lo], win[:, rw_hi:]], axis=1).astype(BF16)
        zero_w = jnp.zeros((lora_w, c_rw), F32)
        w2cat = jnp.concatenate([jnp.concatenate([rwkv_w2[l, 0], zero_w], axis=1),
                                 jnp.concatenate([zero_w, rwkv_w2[l, 1]], axis=1)], axis=0).astype(BF16)
        a2cat = jnp.concatenate([jnp.concatenate([rwkv_a2[l, 0], zero_w], axis=1),
                                 jnp.concatenate([zero_w, rwkv_a2[l, 1]], axis=1)], axis=0).astype(BF16)
        bs_full = jnp.repeat(sgu_bs[l].T, sgu_c // n_grp, axis=1)
        k_a = rwkv_k_a[l].reshape(1, c_rw)
        weights = dict(
            g2=rwkv_g2[l].astype(BF16), ws=sgu_ws[l].astype(BF16), pc=proj_conv[l].astype(BF16),
            ps=proj_sgu[l].astype(BF16), pr=proj_rwkv[l].astype(BF16), wo=w_out[l].astype(BF16),
            w1=exp_w1[l].astype(BF16), w3=exp_w3[l].astype(BF16), w2=exp_w2[l].astype(BF16))
        for grp in groups:
            x, rows = grp["x"], grp["rows"]
            r, k, v, kk, lw, aa, bonus, g = _prep(
                rows, x, mod_l, g1, win_r,
                rwkv_w0[l].reshape(1, 2 * c_rw), w2cat, rwkv_a0[l].reshape(1, 2 * c_rw), a2cat,
                weights["g2"], rwkv_k_k[l].reshape(1, c_rw), k_a, rwkv_r_k[l].reshape(1, c_rw), seg)
            part, grw = _branch(
                rows, grp["cols"], x, mod_l, g1, win_c, conv_w[l], sgu_ln_g[l].reshape(1, sgu_c),
                sgu_ln_b[l].reshape(1, sgu_c), weights["ws"], bs_full, weights["pc"], weights["ps"])
            n_seq = x.shape[0] // grp["t"]
            if grp["s0"] is None:
                s0 = jnp.zeros((n_seq, 2, n_str, hs, SCAN_HEADS * hs), F32)
            else:
                s0 = side_by_side(grp["s0"][:, l].astype(F32))
            yf, yb, s_fin = _scan(grp["t"], r, k, v, kk, lw, aa, k_a, s0, scan_consts)
            if grp["s0"] is None:
                new_states.append(from_side_by_side(s_fin))
            x1, h2, logits_t = _merge(
                rows, x, yf, yb, bonus, g, part, grw, mod_l, rwkv_lnx_w[l].reshape(1, c_rw),
                rwkv_lnx_b[l].reshape(1, c_rw), seg, weights["pr"], weights["wo"],
                norm2_g[l].reshape(1, d), rw_hi_b, rw_lo_b, hs)
            tok, pos, blk = _route(logits_t, bias_b, tri)
            grp["x"] = _moe(rows, blk[:, 0, :BLK_STRIDE].reshape(-1), h2, pos, tok, x1, mod_l, weights["w1"],
                            weights["w3"], weights["w2"], final_g2, final_norm=(l == n_layer - 1))

    y_prompt = groups[0]["x"].reshape(n_ctx, t_ctx, d)
    y_sample = groups[1]["x"].reshape(n_lat, t_lat, d)
    new_state = jnp.stack(new_states, axis=1).astype(x_prompt.dtype)
    return (y_prompt, y_sample, new_state)
```

```python
import functools
import math

import jax
import jax.numpy as jnp
import numpy as np
from jax import lax
from jax.experimental import pallas as pl
from jax.experimental.pallas import tpu as pltpu

F32 = jnp.float32
BF16 = jnp.bfloat16

LANE_TILE = 128
SUBLANE_TILE = 8

GRID_W = 64
N_GROUPS = 4
GROUP_SIZE = 4
NORM_EPS = 1e-6
GN_EPS = 64e-5
KK_EPS = 1e-12
DECAY_SCALE = math.exp(-0.5)

TOKEN_TILE = 1024
MOE_TILE = 512
MOE_BLOCK = 128
BLK_STRIDE = 16
SCAN_CHUNK = 64
SCAN_HEADS = 4
SCAN_PAR = 2
SCAN_SUB = 4
SCAN_STAGGER = 4
MOD_COLS = 1536
VMEM_LIMIT = 56 * 1024 * 1024


def _dot(a, b):
    return jnp.dot(a, b, preferred_element_type=F32)


def _dot_nt(a, b):
    return lax.dot_general(a, b, (((1,), (1,)), ((), ())), preferred_element_type=F32)


def _split(x):
    hi = x.astype(BF16)
    lo = (x - hi.astype(F32)).astype(BF16)
    return hi, lo


def _seg_sum(x, seg):
    return _dot(x.astype(BF16), seg)


def _sigmoid(x):
    return 1.0 / (1.0 + jnp.exp(-x))


def _norm_mod(x, gain, shift, scale):
    ms = jnp.mean(x * x, axis=-1, keepdims=True)
    return (x * lax.rsqrt(ms + NORM_EPS) * gain) * (1.0 + scale) + shift


def _const_spec(shape):
    nd = len(shape)
    return pl.BlockSpec(shape, lambda *_: (0,) * nd)


def _mod_kernel(c_ref, w_ref, b_ref, o_ref):
    c = c_ref[...]
    s = c * _sigmoid(c)
    o_ref[0] = _dot(s.astype(BF16), w_ref[0].astype(BF16)) + b_ref[0]


def _modulation(cvec, ada_w, ada_b):
    n_layer, d, cols = ada_w.shape
    rows = cvec.shape[0]
    return pl.pallas_call(
        _mod_kernel,
        grid=(n_layer, cols // MOD_COLS),
        in_specs=[
            pl.BlockSpec((rows, d), lambda l, j: (0, 0)),
            pl.BlockSpec((1, d, MOD_COLS), lambda l, j: (l, 0, j)),
            pl.BlockSpec((1, 1, MOD_COLS), lambda l, j: (l, 0, j)),
        ],
        out_specs=pl.BlockSpec((1, rows, MOD_COLS), lambda l, j: (l, 0, j)),
        out_shape=jax.ShapeDtypeStruct((n_layer, rows, cols), F32),
        compiler_params=pltpu.CompilerParams(
            dimension_semantics=("arbitrary", "arbitrary"), vmem_limit_bytes=VMEM_LIMIT),
        name="mod",
    )(cvec, ada_w, ada_b.reshape(n_layer, 1, cols))


def _prep_kernel(x_ref, mod_ref, g1_ref, win_ref, w0_ref, w2_ref, a0_ref, a2_ref, g2_ref,
                 kk_ref, ka_ref, rk_ref, seg_ref,
                 r_o, k_o, v_o, kk_o, lw_o, aa_o, bonus_o, g_o, *, row0, tiles_per_seq):
    row = row0 + pl.program_id(0) // tiles_per_seq
    x = x_ref[...]
    d = x.shape[1]
    c = r_o.shape[1]
    shift = mod_ref[pl.ds(row, 1), 0:d]
    scale = mod_ref[pl.ds(row, 1), d:2 * d]
    h = _norm_mod(x, g1_ref[...], shift, scale)
    z = _dot(h.astype(BF16), win_ref[...])
    r = z[:, 0:c]
    k = z[:, c:2 * c]
    v = z[:, 2 * c:3 * c]
    lo = 3 * c
    w_lo = z[:, lo:lo + LANE_TILE]
    a_lo = z[:, lo + LANE_TILE:lo + 2 * LANE_TILE]
    g_lo = z[:, lo + 2 * LANE_TILE:lo + 3 * LANE_TILE]
    w_logit = w0_ref[...] + _dot(jnp.tanh(w_lo).astype(BF16), w2_ref[...])
    lw = -DECAY_SCALE * _sigmoid(w_logit)
    aa = _sigmoid(a0_ref[...] + _dot(a_lo.astype(BF16), a2_ref[...]))
    g = _dot(_sigmoid(g_lo).astype(BF16), g2_ref[...])
    seg = seg_ref[...]
    kk0 = k * kk_ref[...]
    kk = kk0 / jnp.maximum(jnp.sqrt(_seg_sum(kk0 * kk0, seg)), KK_EPS)
    ka = ka_ref[...]
    k_dirs = k * (1.0 + (aa[:, 0:c] - 1.0) * ka) + k * (1.0 + (aa[:, c:2 * c] - 1.0) * ka)
    rk = _seg_sum(r * k_dirs * rk_ref[...], seg)
    r_o[...] = r
    k_o[...] = k
    v_o[...] = v.astype(BF16)
    kk_o[...] = kk
    lw_o[0] = lw[:, 0:c]
    lw_o[1] = lw[:, c:2 * c]
    aa_o[0] = aa[:, 0:c]
    aa_o[1] = aa[:, c:2 * c]
    bonus_o[...] = (rk * v).astype(BF16)
    g_o[...] = g.astype(BF16)


def _prep(rows, x, mod, g1, win_r, w0, w2, a0, a2, g2, k_k, k_a, r_k, seg):
    n, d = x.shape
    c = seg.shape[0]
    tm = TOKEN_TILE
    tok = lambda w: pl.BlockSpec((tm, w), lambda i: (i, 0))
    tok2 = pl.BlockSpec((2, tm, c), lambda i: (0, i, 0))
    consts = [mod, g1, win_r, w0, w2, a0, a2, g2, k_k, k_a, r_k, seg]
    sds = jax.ShapeDtypeStruct
    return pl.pallas_call(
        functools.partial(_prep_kernel, row0=rows[0], tiles_per_seq=rows[1] // tm),
        grid=(n // tm,),
        in_specs=[tok(d)] + [_const_spec(a.shape) for a in consts],
        out_specs=[tok(c), tok(c), tok(c), tok(c), tok2, tok2, tok(c), tok(c)],
        out_shape=[sds((n, c), F32), sds((n, c), F32), sds((n, c), BF16), sds((n, c), F32)]
        + [sds((2, n, c), F32)] * 2 + [sds((n, c), BF16)] * 2,
        compiler_params=pltpu.CompilerParams(
            dimension_semantics=("arbitrary",), vmem_limit_bytes=VMEM_LIMIT),
        name="prep",
    )(x, *consts)


def _branch_kernel(x_ref, mod_ref, g1_ref, win_ref, cw_ref, lng_ref, lnb_ref,
                   ws_ref, bs_ref, pc_ref, ps_ref, part_o, grw_o, *, row0, tiles_per_seq, cols):
    row = row0 + pl.program_id(0) // tiles_per_seq
    x = x_ref[...]
    tm, d = x.shape
    cw_w = cw_ref.shape[1]
    sg_w = lng_ref.shape[1]
    chunk = ws_ref.shape[1]
    n_grp = ws_ref.shape[0]
    shift = mod_ref[pl.ds(row, 1), 0:d]
    scale = mod_ref[pl.ds(row, 1), d:2 * d]
    h = _norm_mod(x, g1_ref[...], shift, scale)
    z = _dot(h.astype(BF16), win_ref[...])
    cb = z[:, 0:cw_w]
    cc = z[:, cw_w:2 * cw_w]
    cx = z[:, 2 * cw_w:3 * cw_w]
    o = 3 * cw_w
    su = z[:, o:o + sg_w]
    sv = z[:, o + sg_w:o + 2 * sg_w]
    gates = _sigmoid(z[:, o + 2 * sg_w:])
    xc = cc * cx
    col = lax.broadcasted_iota(jnp.int32, xc.shape, 0) & (cols - 1)
    prev = jnp.where(col == 0, 0.0, pltpu.roll(xc, 1, 0))
    nxt = jnp.where(col == cols - 1, 0.0, pltpu.roll(xc, tm - 1, 0))
    cw = cw_ref[...]
    y_conv = cb * (cw[0:1] * prev + cw[1:2] * xc + cw[2:3] * nxt)
    mu = jnp.mean(sv, axis=-1, keepdims=True)
    dv = sv - mu
    var = jnp.mean(dv * dv, axis=-1, keepdims=True)
    vn = (dv * lax.rsqrt(var + NORM_EPS) * lng_ref[...] + lnb_ref[...]).astype(BF16)
    lane_grp = lax.broadcasted_iota(jnp.int32, (chunk, sg_w), 1) // (sg_w // n_grp)
    parts = []
    for ci in range(tm // chunk):
        vc = vn[ci * chunk:(ci + 1) * chunk]
        mixed = bs_ref[...]
        for gi in range(n_grp):
            mixed = mixed + jnp.where(lane_grp == gi, _dot(ws_ref[gi], vc), 0.0)
        parts.append(su[ci * chunk:(ci + 1) * chunk] * mixed)
    y_sgu = jnp.concatenate(parts, axis=0)
    part_o[...] = (gates[:, 0:d] * _dot(y_conv.astype(BF16), pc_ref[...])
                   + gates[:, 2 * d:3 * d] * _dot(y_sgu.astype(BF16), ps_ref[...])).astype(BF16)
    grw_o[...] = gates[:, d:2 * d].astype(BF16)


def _branch(rows, cols, x, mod, g1, win_c, conv_w, ln_g, ln_b, ws, bs_full, proj_conv, proj_sgu):
    n, d = x.shape
    tm = TOKEN_TILE
    tok = pl.BlockSpec((tm, d), lambda i: (i, 0))
    consts = [mod, g1, win_c, conv_w, ln_g, ln_b, ws, bs_full, proj_conv, proj_sgu]
    return pl.pallas_call(
        functools.partial(_branch_kernel, row0=rows[0], tiles_per_seq=rows[1] // tm, cols=cols),
        grid=(n // tm,),
        in_specs=[tok] + [_const_spec(a.shape) for a in consts],
        out_specs=[tok, tok],
        out_shape=[jax.ShapeDtypeStruct((n, d), BF16)] * 2,
        compiler_params=pltpu.CompilerParams(
            dimension_semantics=("arbitrary",), vmem_limit_bytes=VMEM_LIMIT),
        name="branch",
    )(x, *consts)


def _scan_constants(n_l, hs):
    lanes = SCAN_HEADS * hs
    t = np.arange(n_l)[:, None]
    s = np.arange(lanes)[None, :] % n_l
    n_lvl = n_l.bit_length() - 1
    masks, lvl = [], []
    for reverse in (False, True):
        if not reverse:
            level = lambda q: (((t >> q) & 1) == 1) & ((s >> q) == (t >> q) - 1)
            masks.append(np.concatenate([t > s, t >= s, level(0)], axis=0))
        else:
            level = lambda q: (((t >> q) & 1) == 0) & ((s >> q) == (t >> q) + 1)
            masks.append(np.concatenate([t < s, t <= s, level(0)], axis=0))
        lvl.append([level(q) for q in range(1, n_lvl)])
    as_j = lambda a, dt: jnp.asarray(np.asarray(a, np.float32), dt)
    return as_j(t == s, F32), as_j(masks, F32), as_j(lvl, BF16)


class _HeadBlocks:
    def __init__(self, hs, lanes):
        self.hs, self.lanes = hs, lanes
        self.n_tiles = lanes // LANE_TILE
        self.per_tile = LANE_TILE // hs
        self.heads = lanes // hs
        lane_head = lax.broadcasted_iota(jnp.int32, (hs, LANE_TILE), 1) // hs
        self.half_f32 = [(lane_head == j).astype(F32) for j in range(self.per_tile)]
        self.half_bf16 = [m.astype(BF16) for m in self.half_f32]
        row_head = lax.broadcasted_iota(jnp.int32, (LANE_TILE, LANE_TILE), 0) // hs
        col_head = lax.broadcasted_iota(jnp.int32, (LANE_TILE, LANE_TILE), 1) // hs
        self.tile_mask = (row_head == col_head).astype(F32)

    def expand(self, compact):
        zero = jnp.zeros((self.hs, LANE_TILE), compact.dtype)
        rows = []
        for h in range(self.heads):
            tiles = [zero] * self.n_tiles
            tiles[h // self.per_tile] = compact[h * self.hs:(h + 1) * self.hs]
            rows.append(jnp.concatenate(tiles, axis=1))
        return jnp.concatenate(rows, axis=0)

    def compact(self, x, masks):
        return jnp.concatenate(
            [x[:, (h // self.per_tile) * LANE_TILE:(h // self.per_tile + 1) * LANE_TILE] * masks[h % self.per_tile]
             for h in range(self.heads)], axis=0)

    def diag(self, full, masks):
        return jnp.concatenate(
            [full[h * self.hs:(h + 1) * self.hs,
                  (h // self.per_tile) * LANE_TILE:(h // self.per_tile + 1) * LANE_TILE] * masks[h % self.per_tile]
             for h in range(self.heads)], axis=0)

    def side_by_side(self, compact):
        tiles = []
        for t in range(self.n_tiles):
            acc = compact[t * self.per_tile * self.hs:(t * self.per_tile + 1) * self.hs]
            for j in range(1, self.per_tile):
                h = t * self.per_tile + j
                acc = acc + compact[h * self.hs:(h + 1) * self.hs]
            tiles.append(acc)
        return jnp.concatenate(tiles, axis=1)

    def bd(self, x):
        return self.expand(self.compact(x.astype(BF16), self.half_bf16))

    def bd_t(self, x):
        tiles = []
        for t in range(self.n_tiles):
            square = jnp.concatenate([x[:, t * LANE_TILE:(t + 1) * LANE_TILE]] * self.per_tile, axis=0)
            tiles.append((square.T * self.tile_mask).astype(BF16))
        return self.expand(jnp.concatenate(tiles, axis=0))


def _scan_prepare(streams, hb, eye, masks_ref, lvl_ref, out):
    n_l = streams[0][0].shape[0]
    ns = range(len(streams))
    bd = hb.bd
    r, k, v, kk, lw, aa, ka, di = map(list, zip(*streams))
    tril_mask = lambda i: masks_ref[di[i], 0:2 * n_l]
    end = [0 if d else n_l - 1 for d in di]
    step_row = lax.broadcasted_iota(jnp.int32, lw[0].shape, 0)
    c = list(lw)
    shift = 1
    while shift < n_l:
        for i in ns:
            if di[i]:
                moved = jnp.where(step_row < n_l - shift, pltpu.roll(c[i], n_l - shift, 0), 0.0)
            else:
                moved = jnp.where(step_row >= shift, pltpu.roll(c[i], shift, 0), 0.0)
            c[i] = c[i] + moved
        shift *= 2
    yield
    b_vec = [kk[i] * aa[i] for i in ns]
    kd = [k[i] * (1.0 + (aa[i] - 1.0) * ka[i]) for i in ns]
    c_end = [c[i][end[i]:end[i] + 1] for i in ns]
    e_neg = [jnp.exp(-c[i]) for i in ns]
    lhs = [jnp.concatenate([-kk[i] * jnp.exp(c[i] - lw[i]), r[i] * jnp.exp(c[i])], axis=0).astype(BF16)
           for i in ns]
    yield
    m_b = [_dot(lhs[i], hb.bd_t(b_vec[i] * e_neg[i])) * tril_mask(i) for i in ns]
    yield
    m_k = [(_dot(lhs[i], hb.bd_t(kd[i] * e_neg[i])) * tril_mask(i)).astype(BF16) for i in ns]
    yield
    a_ab = [m_b[i][0:n_l] for i in ns]
    t_inv = [eye + a_ab[i] * masks_ref[di[i], 2 * n_l:3 * n_l] for i in ns]
    a_bf = [a_ab[i].astype(BF16) for i in ns]
    for q in range(lvl_ref.shape[1]):
        x1 = [_dot(t_inv[i].astype(BF16), bd(a_bf[i] * lvl_ref[di[i], q])) for i in ns]
        yield
        x2 = [_dot(x1[i].astype(BF16), bd(t_inv[i])) for i in ns]
        t_inv = [t_inv[i] + x2[i] for i in ns]
        yield
    v_k = [_dot(m_k[i], bd(v[i])) for i in ns]
    e_end = [jnp.exp(c_end[i] - c[i]) for i in ns]
    bk_t = [jnp.concatenate([b_vec[i] * e_end[i], kd[i] * e_end[i]], axis=0).T.astype(BF16) for i in ns]
    decay_col = [jnp.broadcast_to(jnp.exp(c_end[i]), (LANE_TILE, hb.lanes)).T for i in ns]
    out.extend((lhs[i], v_k[i], t_inv[i], m_b[i][n_l:].astype(BF16), v[i], bk_t[i], decay_col[i]) for i in ns)


def _scan_apply(pre, z, hb, out):
    lhs, v_k, t_inv, a_rb, v, bk_t, decay_col = map(list, zip(*pre))
    n_l = t_inv[0].shape[0]
    ns = range(len(pre))
    v_part = [_dot(lhs[i], hb.expand(z[i].astype(BF16))) + v_k[i] for i in ns]
    yield
    u = [_dot(t_inv[i].astype(BF16), hb.bd(v_part[i][0:n_l])) for i in ns]
    yield
    y = [v_part[i][n_l:] + _dot(a_rb[i], hb.bd(u[i])) for i in ns]
    yield
    uv = [jnp.concatenate([u[i].astype(BF16), v[i].astype(BF16)], axis=0) for i in ns]
    z_new = [z[i] * decay_col[i] + hb.diag(_dot(bk_t[i], uv[i]), hb.half_f32) for i in ns]
    out.extend(zip(y, z_new))


def _scan_kernel(rf_ref, kf_ref, vf_ref, kkf_ref, lwf_ref, aaf_ref,
                 rb_ref, kb_ref, vb_ref, kkb_ref, lwb_ref, aab_ref,
                 ka_ref, s0_ref, eye_ref, masks_ref, lvl_ref,
                 yf_o, yb_o, sfin_o, z_scr):
    j = pl.program_id(1)
    n_par, n_rows, c = rf_ref.shape
    n_l = eye_ref.shape[0]
    lanes = z_scr.shape[1]
    n_str = c // lanes
    hb = _HeadBlocks(lanes // SCAN_HEADS, lanes)
    slots = [(p, di, hg) for p in range(n_par) for di in range(2) for hg in range(n_str)]

    @pl.when(j == 0)
    def _():
        for zi, (p, di, hg) in enumerate(slots):
            z_scr[zi] = hb.compact(s0_ref[p, 0, di, hg], hb.half_f32)

    ka = ka_ref[...]
    dirs = ((rf_ref, kf_ref, vf_ref, kkf_ref, lwf_ref, aaf_ref, yf_o),
            (rb_ref, kb_ref, vb_ref, kkb_ref, lwb_ref, aab_ref, yb_o))
    n_sub = n_rows // n_l
    streams = []
    for sub in range(n_sub):
        for p, di, hg in slots:
            r_ref, k_ref, v_ref, kk_ref, lw_ref, aa_ref, _ = dirs[di]
            rs = slice((n_sub - 1 - sub if di else sub) * n_l, (n_sub - sub if di else sub + 1) * n_l)
            ls = slice(hg * lanes, (hg + 1) * lanes)
            streams.append((r_ref[p, rs, ls], k_ref[p, rs, ls], v_ref[p, rs, ls], kk_ref[p, rs, ls],
                            lw_ref[0, p, rs, ls], aa_ref[0, p, rs, ls], ka[:, ls], di))
    n_slot = len(slots)
    eye = eye_ref[...]
    pre = [[] for _ in range(n_sub)]
    res = [[] for _ in range(n_sub)]
    z0 = [z_scr[zi] for zi in range(n_slot)]

    gens = [_scan_prepare(streams[s * n_slot:(s + 1) * n_slot], hb, eye, masks_ref, lvl_ref, pre[s])
            for s in range(n_sub)]
    for _ in range(SCAN_STAGGER):
        next(gens[0])
    for s in range(n_sub):
        nxt = gens[s + 1] if s + 1 < n_sub else None
        for _ in gens[s]:
            if nxt is not None:
                next(nxt, None)
        z_in = z0 if s == 0 else [o[1] for o in res[s - 1]]
        for _ in _scan_apply(pre[s], z_in, hb, res[s]):
            if nxt is not None:
                next(nxt, None)
        for (y, _), (p, di, hg) in zip(res[s], slots):
            rs = slice((n_sub - 1 - s if di else s) * n_l, (n_sub - s if di else s + 1) * n_l)
            dirs[di][6][p, rs, hg * lanes:(hg + 1) * lanes] = y
    z = [o[1] for o in res[n_sub - 1]]
    for zi in range(len(slots)):
        z_scr[zi] = z[zi]

    @pl.when(j == pl.num_programs(1) - 1)
    def _():
        for zi, (p, di, hg) in enumerate(slots):
            sfin_o[p, 0, di, hg] = hb.side_by_side(z_scr[zi])


def _scan(t_len, r, k, v, kk, lw, aa, k_a, s0, consts):
    n, c = r.shape
    n_seq = s0.shape[0]
    n_par = SCAN_PAR
    n_l = SCAN_CHUNK * SCAN_SUB
    n_c = t_len // n_l
    hs = s0.shape[3]
    lanes = SCAN_HEADS * hs
    n_str = c // lanes
    part = lambda a: a.reshape(n_par, n // n_par, c)
    part2 = lambda a: a.reshape(2, n_par, n // n_par, c)
    s0p = s0.reshape((n_par, n_seq // n_par) + s0.shape[1:])
    fwd = pl.BlockSpec((n_par, n_l, c), lambda q, j: (0, q * n_c + j, 0))
    bwd = pl.BlockSpec((n_par, n_l, c), lambda q, j: (0, q * n_c + n_c - 1 - j, 0))
    fwd2 = pl.BlockSpec((1, n_par, n_l, c), lambda q, j: (0, 0, q * n_c + j, 0))
    bwd2 = pl.BlockSpec((1, n_par, n_l, c), lambda q, j: (1, 0, q * n_c + n_c - 1 - j, 0))
    st = pl.BlockSpec((n_par, 1, 2, n_str, hs, lanes), lambda q, j: (0, q, 0, 0, 0, 0))
    sds = jax.ShapeDtypeStruct
    args = [part(r), part(k), part(v), part(kk), part2(lw), part2(aa)]
    yf, yb, s_fin = pl.pallas_call(
        _scan_kernel,
        grid=(n_seq // n_par, n_c),
        in_specs=[fwd, fwd, fwd, fwd, fwd2, fwd2, bwd, bwd, bwd, bwd, bwd2, bwd2,
                  _const_spec(k_a.shape), st] + [_const_spec(a.shape) for a in consts],
        out_specs=[fwd, bwd, st],
        out_shape=[sds((n_par, n // n_par, c), F32), sds((n_par, n // n_par, c), F32), sds(s0p.shape, F32)],
        scratch_shapes=[pltpu.VMEM((n_par * 2 * n_str, lanes, LANE_TILE), F32)],
        compiler_params=pltpu.CompilerParams(
            dimension_semantics=("arbitrary", "arbitrary"), vmem_limit_bytes=VMEM_LIMIT),
        name="scan",
    )(*args, *args, k_a, s0p, *consts)
    return yf.reshape(n, c), yb.reshape(n, c), s_fin.reshape(s0.shape)


def _merge_kernel(x_ref, yf_ref, yb_ref, bonus_ref, g_ref, part_ref, grw_ref, mod_ref,
                  lnw_ref, lnb_ref, seg_ref, prw_ref, wout_ref, g2_ref, rwh_ref, rwl_ref, rb_ref, tri_ref,
                  x1_o, h2_o, tok_o, pos_o, blk_o, *, head_size, row0, tiles_per_seq):
    row = row0 + pl.program_id(0) // tiles_per_seq
    x = x_ref[...]
    d = x.shape[1]
    seg = seg_ref[...]
    y = yf_ref[...] + yb_ref[...]
    hs_inv = 1.0 / head_size
    mu = _seg_sum(y, seg) * hs_inv
    dy = y - mu
    var = _seg_sum(dy * dy, seg) * hs_inv
    yn = dy * lax.rsqrt(var + GN_EPS) * lnw_ref[...] + lnb_ref[...]
    y_rwkv = (yn + bonus_ref[...]) * g_ref[...]
    merged = part_ref[...] + grw_ref[...] * _dot(y_rwkv.astype(BF16), prw_ref[...])
    mix = _dot(merged.astype(BF16), wout_ref[...])
    gate1 = mod_ref[pl.ds(row, 1), 2 * d:3 * d]
    shift2 = mod_ref[pl.ds(row, 1), 3 * d:4 * d]
    scale2 = mod_ref[pl.ds(row, 1), 4 * d:5 * d]
    x1 = x + gate1 * mix
    h2 = _norm_mod(x1, g2_ref[...], shift2, scale2)
    h_hi, h_lo = _split(h2)
    rwh = rwh_ref[...]
    n_e = rwh.shape[0]
    both = _dot_nt(jnp.concatenate([rwh, rwl_ref[...]], axis=0), h_hi)
    x1_o[...] = x1
    h2_o[...] = h_hi
    logits = both[0:n_e] + both[n_e:] + _dot_nt(rwh, h_lo)
    rt = tri_ref.shape[0]
    for part in range(logits.shape[1] // rt):
        cols = slice(part * rt, (part + 1) * rt)
        tok, pos, blk = _route_tile(logits[:, cols], rb_ref[...], tri_ref[...], blk_o.shape[2])
        tok_o[cols, :] = tok
        pos_o[:, cols] = pos
        blk_o[part] = blk


def _merge(rows, x, yf, yb, bonus, g, part, grw, mod, lnw, lnb, seg, proj_rwkv, w_out, g2n, rw_hi, rw_lo,
           bias_b, tri, head_size):
    n, d = x.shape
    c = seg.shape[0]
    n_e = rw_hi.shape[0]
    tm = TOKEN_TILE
    tok = lambda w: pl.BlockSpec((tm, w), lambda i: (i, 0))
    consts = [mod, lnw, lnb, seg, proj_rwkv, w_out, g2n, rw_hi, rw_lo, bias_b, tri]
    rt = tri.shape[0]
    sds = jax.ShapeDtypeStruct
    return pl.pallas_call(
        functools.partial(_merge_kernel, head_size=head_size, row0=rows[0], tiles_per_seq=rows[1] // tm),
        grid=(n // tm,),
        in_specs=[tok(d), tok(c), tok(c), tok(c), tok(c), tok(d), tok(d)]
        + [_const_spec(a.shape) for a in consts],
        out_specs=[tok(d), tok(d), tok(LANE_TILE), pl.BlockSpec((1, tm), lambda i: (0, i)),
                   pl.BlockSpec((tm // rt, 1, LANE_TILE), lambda i: (i, 0, 0))],
        out_shape=[sds((n, d), F32), sds((n, d), BF16), sds((n, LANE_TILE), F32), sds((1, n), jnp.int32),
                   sds((n // rt, 1, LANE_TILE), jnp.int32)],
        compiler_params=pltpu.CompilerParams(
            dimension_semantics=("arbitrary",), vmem_limit_bytes=VMEM_LIMIT),
        name="merge",
    )(x, yf, yb, bonus, g, part, grw, *consts)


def _route_tile(logits, bias, tri, blk_lanes):
    scores = _sigmoid(logits)
    biased = scores + bias
    n_e, n_t = scores.shape
    rows = [biased[e:e + 1] for e in range(n_e)]
    srow = [scores[e:e + 1] for e in range(n_e)]

    def top2_sum(a, b, c, d):
        s1, t1 = jnp.maximum(a, b), jnp.minimum(a, b)
        s2, t2 = jnp.maximum(c, d), jnp.minimum(c, d)
        return jnp.maximum(s1, s2) + jnp.maximum(jnp.minimum(s1, s2), jnp.maximum(t1, t2))

    best = top2_sum(*rows[0:GROUP_SIZE])
    g_sel = jnp.zeros(best.shape, jnp.int32)
    for gi in range(1, N_GROUPS):
        gs = top2_sum(*rows[gi * GROUP_SIZE:(gi + 1) * GROUP_SIZE])
        better = gs > best
        best = jnp.where(better, gs, best)
        g_sel = jnp.where(better, gi, g_sel)

    def pick(src, j):
        out = src[j]
        for gi in range(1, N_GROUPS):
            out = jnp.where(g_sel == gi, src[gi * GROUP_SIZE + j], out)
        return out

    in_b = [pick(rows, j) for j in range(GROUP_SIZE)]
    in_s = [pick(srow, j) for j in range(GROUP_SIZE)]

    def arg_first_max(vals):
        best_v = vals[0]
        best_i = jnp.zeros(best_v.shape, jnp.int32)
        for j in range(1, len(vals)):
            better = vals[j] > best_v
            best_v = jnp.where(better, vals[j], best_v)
            best_i = jnp.where(better, j, best_i)
        return best_i

    i1 = arg_first_max(in_b)
    i2 = arg_first_max([jnp.where(i1 == j, -jnp.inf, in_b[j]) for j in range(GROUP_SIZE)])

    def take(src, idx):
        out = src[0]
        for j in range(1, GROUP_SIZE):
            out = jnp.where(idx == j, src[j], out)
        return out

    g1, g2 = take(in_s, i1), take(in_s, i2)
    den = g1 + g2
    j_idx = lax.broadcasted_iota(jnp.int32, (SUBLANE_TILE, n_t), 0)
    g4 = jnp.where(j_idx == i1, g1 / den, jnp.where(j_idx == i2, g2 / den, 0.0))

    grp = lax.broadcasted_iota(jnp.int32, (n_e, n_t), 0)
    onehot = (grp == g_sel).astype(F32)
    rank = _dot(onehot.astype(BF16), tri)
    cnt = jnp.sum(onehot, axis=1, keepdims=True).astype(jnp.int32)
    blocks = jnp.zeros(cnt.shape, jnp.int32)
    for kb in range(-(-n_t // MOE_BLOCK)):
        blocks = blocks + (cnt > kb * MOE_BLOCK).astype(jnp.int32)
    used = jnp.zeros((1, 1), jnp.int32)
    pos = jnp.zeros((1, n_t), jnp.int32)
    lane = lax.broadcasted_iota(jnp.int32, (1, blk_lanes), 1)
    blk_grp = jnp.zeros(lane.shape, jnp.int32)
    for gi in range(N_GROUPS):
        pos = pos + jnp.where(g_sel == gi, used * MOE_BLOCK + rank[gi:gi + 1].astype(jnp.int32), 0)
        used = used + blocks[gi:gi + 1]
        if gi < N_GROUPS - 1:
            blk_grp = blk_grp + (used <= lane).astype(jnp.int32)
    per_token = jnp.where(j_idx == GROUP_SIZE, pos.astype(F32), g4)
    tok = jnp.concatenate([per_token, jnp.zeros((LANE_TILE - SUBLANE_TILE, n_t), F32)], axis=0).T
    n_blk = _moe_blocks(n_t)
    return tok, pos, jnp.where(lane < n_blk, blk_grp, jnp.where(lane == n_blk, used, 0))


def _moe_blocks(tile):
    return -(-(tile + N_GROUPS * (MOE_BLOCK - 1)) // MOE_BLOCK)


def _moe_kernel(blk_ref, h_ref, posr_ref, tok_ref, x1_ref, mod_ref, w1_ref, w3_ref, w2_ref,
                fg_ref, o_ref, xs_ref, ys_ref, *, final_norm, row0, tiles_per_seq):
    i = pl.program_id(0)
    tm, d = x1_ref.shape
    r_rows = xs_ref.shape[0]
    n_blk = r_rows // MOE_BLOCK
    sel = jnp.where(lax.broadcasted_iota(jnp.int32, (r_rows, tm), 0) == posr_ref[...], 1.0, 0.0).astype(BF16)
    tok = tok_ref[...]
    g_hi = tok.astype(BF16)
    g_lo = (tok - g_hi.astype(F32)).astype(BF16)
    gathered = _dot(sel, jnp.concatenate([h_ref[...], g_hi, g_lo], axis=1))
    xs_ref[...] = gathered[:, 0:d].astype(BF16)
    gs = gathered[:, d:d + LANE_TILE] + gathered[:, d + LANE_TILE:]
    base = i * BLK_STRIDE
    n_used = blk_ref[base + n_blk]
    for b in range(n_blk):
        rows = slice(b * MOE_BLOCK, (b + 1) * MOE_BLOCK)

        @pl.when(b < n_used)
        def _():
            first = blk_ref[base + b] * GROUP_SIZE
            xb = xs_ref[rows]
            experts = range(GROUP_SIZE)
            a = [_dot(xb, w1_ref[first + j]) for j in experts]
            g = [_dot(xb, w3_ref[first + j]) for j in experts]
            hid = [((a[j] * _sigmoid(a[j])) * g[j]).astype(BF16) for j in experts]
            out = [gs[rows, j:j + 1] * _dot(hid[j], w2_ref[first + j]) for j in experts]
            ys_ref[rows] = ((out[0] + out[1]) + (out[2] + out[3])).astype(BF16)

        @pl.when(b >= n_used)
        def _():
            ys_ref[rows] = jnp.zeros((MOE_BLOCK, d), BF16)

    pos_col = tok[:, GROUP_SIZE:GROUP_SIZE + 1].astype(jnp.int32)
    selt = jnp.where(lax.broadcasted_iota(jnp.int32, (tm, r_rows), 1) == pos_col, 1.0, 0.0).astype(BF16)
    row = row0 + i // tiles_per_seq
    gate2 = mod_ref[pl.ds(row, 1), 5 * d:6 * d]
    x2 = x1_ref[...] + gate2 * _dot(selt, ys_ref[...])
    if final_norm:
        ms = jnp.mean(x2 * x2, axis=-1, keepdims=True)
        x2 = x2 * lax.rsqrt(ms + NORM_EPS) * fg_ref[...]
    o_ref[...] = x2


def _moe(rows, blk, h2, pos_row, tok, x1, mod, w1, w3, w2, final_g, final_norm):
    n, d = x1.shape
    tm = MOE_TILE
    r_rows = _moe_blocks(tm) * MOE_BLOCK
    rows_of = lambda w: pl.BlockSpec((tm, w), lambda i, *_: (i, 0))
    resident = lambda a: pl.BlockSpec(a.shape, lambda i, *_: (0,) * a.ndim, pipeline_mode=pl.Buffered(1))
    grid_spec = pltpu.PrefetchScalarGridSpec(
        num_scalar_prefetch=1,
        grid=(n // tm,),
        in_specs=[rows_of(d), pl.BlockSpec((1, tm), lambda i, *_: (0, i)), rows_of(LANE_TILE), rows_of(d),
                  _const_spec(mod.shape), resident(w1), resident(w3), resident(w2), _const_spec(final_g.shape)],
        out_specs=rows_of(d),
        scratch_shapes=[pltpu.VMEM((r_rows, d), BF16), pltpu.VMEM((r_rows, d), BF16)],
    )
    return pl.pallas_call(
        functools.partial(_moe_kernel, final_norm=final_norm, row0=rows[0],
                          tiles_per_seq=max(rows[1] // tm, 1)),
        grid_spec=grid_spec,
        out_shape=jax.ShapeDtypeStruct((n, d), F32),
        compiler_params=pltpu.CompilerParams(
            dimension_semantics=("arbitrary",), vmem_limit_bytes=VMEM_LIMIT),
        name="moe",
    )(blk, h2, pos_row, tok, x1, mod, w1, w3, w2, final_g)


def kernel(x_prompt, x_sample, state_rwkv, c, c_ctx, norm1_g, norm2_g, final_g, ada_w, ada_b, w_in,
           conv_w, rwkv_w0, rwkv_w2, rwkv_a0, rwkv_a2, rwkv_g2, rwkv_k_k, rwkv_k_a, rwkv_r_k,
           rwkv_lnx_w, rwkv_lnx_b, sgu_ln_g, sgu_ln_b, sgu_ws, sgu_bs, proj_conv, proj_rwkv, proj_sgu,
           w_out, router_w, router_b, exp_w1, exp_w3, exp_w2):
    n_ctx, t_ctx, d = x_prompt.shape
    n_lat, t_lat, _ = x_sample.shape
    n_layer = w_in.shape[0]
    _, n_head, hs = rwkv_r_k.shape
    c_rw = n_head * hs
    conv_c = conv_w.shape[2]
    sgu_c = sgu_ln_g.shape[1]
    n_grp, chunk, _ = sgu_ws.shape[1:]
    lora_w = rwkv_w2.shape[2]
    lora_a = rwkv_a2.shape[2]
    lora_g = rwkv_g2.shape[1]
    n_exp = router_w.shape[1]
    assert 2 * lora_w == LANE_TILE and 2 * lora_a == LANE_TILE and lora_g == LANE_TILE
    assert t_ctx & (t_ctx - 1) == 0 and GRID_W & (GRID_W - 1) == 0
    assert TOKEN_TILE % t_ctx == 0 and t_lat % TOKEN_TILE == 0 and TOKEN_TILE % GRID_W == 0
    assert t_lat % MOE_TILE == 0 and (n_ctx * t_ctx) % MOE_TILE == 0 and TOKEN_TILE % chunk == 0
    assert n_exp == N_GROUPS * GROUP_SIZE and c_rw % (SCAN_HEADS * hs) == 0
    assert SCAN_CHUNK == hs and t_ctx % (SCAN_CHUNK * SCAN_SUB) == 0 and t_lat % (SCAN_CHUNK * SCAN_SUB) == 0
    assert n_ctx % SCAN_PAR == 0 and n_lat % SCAN_PAR == 0 and TOKEN_TILE % MOE_TILE == 0
    assert _moe_blocks(MOE_TILE) < BLK_STRIDE and MOE_BLOCK % 16 == 0

    mod_rows = SUBLANE_TILE * (-(-(1 + n_lat) // SUBLANE_TILE))
    cvec = jnp.zeros((mod_rows, d), F32).at[0].set(c_ctx).at[1:1 + n_lat].set(c)
    mod = _modulation(cvec, ada_w, ada_b)

    sizes = (conv_c, conv_c, conv_c, c_rw, c_rw, c_rw, 2 * lora_w, 2 * lora_a, lora_g, sgu_c, sgu_c, 3 * d)
    offs = np.concatenate([[0], np.cumsum(sizes)])
    assert offs[-1] == w_in.shape[2]
    rw_lo, rw_hi = int(offs[3]), int(offs[9])
    seg = jnp.asarray(np.kron(np.eye(n_head), np.ones((hs, hs))), BF16)
    n_str = c_rw // (SCAN_HEADS * hs)

    def side_by_side(s):
        s = jnp.swapaxes(s, -1, -2).reshape(s.shape[0], 2, n_str, SCAN_HEADS, hs, hs)
        return jnp.swapaxes(s, 3, 4).reshape(s.shape[0], 2, n_str, hs, SCAN_HEADS * hs)

    def from_side_by_side(s):
        s = s.reshape(s.shape[0], 2, n_str, hs, SCAN_HEADS, hs)
        return jnp.swapaxes(jnp.swapaxes(s, 3, 4).reshape(s.shape[0], 2, n_head, hs, hs), -1, -2)

    rw_t = router_w.T
    rw_hi_b = rw_t.astype(BF16)
    rw_lo_b = (rw_t - rw_hi_b.astype(F32)).astype(BF16)
    bias_b = jnp.broadcast_to(router_b.astype(F32)[:, None], (n_exp, MOE_TILE))
    tri = jnp.asarray(np.triu(np.ones((MOE_TILE, MOE_TILE)), 1), BF16)
    final_g2 = final_g.reshape(1, d)
    scan_consts = _scan_constants(SCAN_CHUNK, hs)

    n_ctx_tok = n_ctx * t_ctx
    groups = [
        dict(x=x_prompt.reshape(n_ctx_tok, d), t=t_ctx, cols=t_ctx, rows=(0, n_ctx_tok), s0=None),
        dict(x=x_sample.reshape(n_lat * t_lat, d), t=t_lat, cols=GRID_W, rows=(1, t_lat), s0=state_rwkv),
    ]
    new_states = []
    for l in range(n_layer):
        mod_l = mod[l]
        g1 = norm1_g[l].reshape(1, d)
        win = w_in[l]
        win_r = win[:, rw_lo:rw_hi].astype(BF16)
        win_c = jnp.concatenate([win[:, :rw_lo], win[:, rw_hi:]], axis=1).astype(BF16)
        zero_w = jnp.zeros((lora_w, c_rw), F32)
        w2cat = jnp.concatenate([jnp.concatenate([rwkv_w2[l, 0], zero_w], axis=1),
                                 jnp.concatenate([zero_w, rwkv_w2[l, 1]], axis=1)], axis=0).astype(BF16)
        a2cat = jnp.concatenate([jnp.concatenate([rwkv_a2[l, 0], zero_w], axis=1),
                                 jnp.concatenate([zero_w, rwkv_a2[l, 1]], axis=1)], axis=0).astype(BF16)
        bs_full = jnp.repeat(sgu_bs[l].T, sgu_c // n_grp, axis=1)
        k_a = rwkv_k_a[l].reshape(1, c_rw)
        weights = dict(
            g2=rwkv_g2[l].astype(BF16), ws=sgu_ws[l].astype(BF16), pc=proj_conv[l].astype(BF16),
            ps=proj_sgu[l].astype(BF16), pr=proj_rwkv[l].astype(BF16), wo=w_out[l].astype(BF16),
            w1=exp_w1[l].astype(BF16), w3=exp_w3[l].astype(BF16), w2=exp_w2[l].astype(BF16))
        for grp in groups:
            x, rows = grp["x"], grp["rows"]
            r, k, v, kk, lw, aa, bonus, g = _prep(
                rows, x, mod_l, g1, win_r,
                rwkv_w0[l].reshape(1, 2 * c_rw), w2cat, rwkv_a0[l].reshape(1, 2 * c_rw), a2cat,
                weights["g2"], rwkv_k_k[l].reshape(1, c_rw), k_a, rwkv_r_k[l].reshape(1, c_rw), seg)
            part, grw = _branch(
                rows, grp["cols"], x, mod_l, g1, win_c, conv_w[l], sgu_ln_g[l].reshape(1, sgu_c),
                sgu_ln_b[l].reshape(1, sgu_c), weights["ws"], bs_full, weights["pc"], weights["ps"])
            n_seq = x.shape[0] // grp["t"]
            if grp["s0"] is None:
                s0 = jnp.zeros((n_seq, 2, n_str, hs, SCAN_HEADS * hs), F32)
            else:
                s0 = side_by_side(grp["s0"][:, l].astype(F32))
            yf, yb, s_fin = _scan(grp["t"], r, k, v, kk, lw, aa, k_a, s0, scan_consts)
            if grp["s0"] is None:
                new_states.append(from_side_by_side(s_fin))
            x1, h2, tok, pos, blk = _merge(
                rows, x, yf, yb, bonus, g, part, grw, mod_l, rwkv_lnx_w[l].reshape(1, c_rw),
                rwkv_lnx_b[l].reshape(1, c_rw), seg, weights["pr"], weights["wo"],
                norm2_g[l].reshape(1, d), rw_hi_b, rw_lo_b, bias_b, tri, hs)
            grp["x"] = _moe(rows, blk[:, 0, :BLK_STRIDE].reshape(-1), h2, pos, tok, x1, mod_l, weights["w1"],
                            weights["w3"], weights["w2"], final_g2, final_norm=(l == n_layer - 1))

    y_prompt = groups[0]["x"].reshape(n_ctx, t_ctx, d)
    y_sample = groups[1]["x"].reshape(n_lat, t_lat, d)
    new_state = jnp.stack(new_states, axis=1).astype(x_prompt.dtype)
    return (y_prompt, y_sample, new_state)
```
